```python
import math
import jax, jax.numpy as jnp
from jax import lax
import numpy as np

D_MODEL = 1024
BATCH = 8
SEQ = 4096
DEPTH = 2
DEC_BATCH = 32
DEC_SEQ = 4
PAST_LEN = 16384
PAGE_SIZE = 128

HEAD_DIM = 64
W_A = D_MODEL // 2
W_B = D_MODEL // 4
W_C = D_MODEL // 4
H_A = W_A // HEAD_DIM
H_C = W_C // HEAD_DIM
MOBA_BLOCK = 256
MOBA_TOPK = 3
MOBA_Q_BLOCK = 32
CONV_B_WIDTH = 3
CONV_C_WIDTH = 4
GDN_CHUNK = 64
REL_BUCKETS = 32
REL_MAX_DIST = 128
RMS_EPS = 1e-6
IN_SIZES = (W_A, W_A, W_A, W_A, W_B, W_B, W_B, W_B, 3 * W_C, W_C, H_C, H_C)
N_IN = sum(IN_SIZES)

kernel_name = 'moba_conv_deltanet_hybrid_step'


def rms_norm(x, w):
    xf = x.astype(jnp.float32)
    y = xf * lax.rsqrt(jnp.mean(xf * xf, axis=-1, keepdims=True) + RMS_EPS)
    return (y * w.astype(jnp.float32)).astype(x.dtype)


def l2_normalize(x):
    xf = x.astype(jnp.float32)
    return xf * lax.rsqrt(jnp.sum(xf * xf, axis=-1, keepdims=True) + RMS_EPS)


def rel_bucket(n):
    n = jnp.maximum(n, 0)
    max_exact = REL_BUCKETS // 2
    nf = jnp.maximum(n, max_exact).astype(jnp.float32)
    large = max_exact + (jnp.log(nf / max_exact) / math.log(REL_MAX_DIST / max_exact)
                         * (REL_BUCKETS - max_exact)).astype(jnp.int32)
    return jnp.where(n < max_exact, n, jnp.minimum(large, REL_BUCKETS - 1))


def causal_dwconv(x, hist, w):
    k = w.shape[0]
    t = x.shape[1]
    xp = jnp.concatenate([hist.astype(x.dtype), x], axis=1)
    y = xp[:, 0:t] * w[0]
    for j in range(1, k):
        y = y + xp[:, j:j + t] * w[j]
    return y, xp[:, t:]


def moba_attention(q, k_all, v_all, q_pos, rel_bias):
    bsz, t, h, hd = q.shape
    l = k_all.shape[1]
    nb = -(-l // MOBA_BLOCK)
    pad = nb * MOBA_BLOCK - l

    def to_blocks(a):
        a = jnp.pad(a, ((0, 0), (0, pad), (0, 0), (0, 0)))
        return a.reshape(bsz, nb, MOBA_BLOCK, h, hd).transpose(0, 3, 1, 2, 4)

    kb = to_blocks(k_all)
    vb = to_blocks(v_all)
    kmean = jnp.mean(kb.astype(jnp.float32), axis=3)
    n_sel = min(MOBA_TOPK, nb - 1)
    qlen = math.gcd(t, MOBA_Q_BLOCK)
    nq = t // qlen
    q_chunks = q.reshape(bsz, nq, qlen, h, hd).transpose(1, 0, 3, 2, 4)
    p_chunks = q_pos.reshape(nq, qlen)
    bias_h = rel_bias.T.astype(jnp.float32)
    head_ix = jnp.arange(h)[None, :, None, None, None]
    offs = jnp.arange(MOBA_BLOCK, dtype=jnp.int32)
    gather = jax.vmap(jax.vmap(lambda blk, ix: blk[ix]))
    scale = hd ** -0.5

    def attend(args):
        qc, pc = args
        own = pc // MOBA_BLOCK
        own_b = jnp.broadcast_to(own[None, None, :, None], (bsz, h, qlen, 1))
        if n_sel > 0:
            gate = jnp.einsum('bhqd,bhnd->bhqn', qc.astype(jnp.float32), kmean)
            is_past = jnp.arange(nb)[None, :] < own[:, None]
            gate = jnp.where(is_past, gate, -jnp.inf)
            _, top = lax.top_k(gate, n_sel)
            idx = jnp.concatenate([top, own_b], axis=-1)
            ok = jnp.concatenate([top < own[:, None], jnp.ones(own_b.shape, bool)], axis=-1)
        else:
            idx = own_b
            ok = jnp.ones(own_b.shape, bool)
        kg = gather(kb, idx)
        vg = gather(vb, idx)
        dist = pc[:, None, None] - (idx[..., None] * MOBA_BLOCK + offs)
        logits = (jnp.einsum('bhqd,bhqskd->bhqsk', qc, kg).astype(jnp.float32) * scale
                  + bias_h[head_ix, rel_bucket(dist)])
        logits = jnp.where(ok[..., None] & (dist >= 0), logits, -jnp.inf)
        ns = idx.shape[-1]
        p = jax.nn.softmax(logits.reshape(bsz, h, qlen, ns * MOBA_BLOCK), axis=-1).reshape(logits.shape)
        return jnp.einsum('bhqsk,bhqskd->bhqd', p.astype(vg.dtype), vg)

    out = lax.map(attend, (q_chunks, p_chunks))
    return out.transpose(1, 0, 3, 2, 4).reshape(bsz, t, h, hd)


def gated_delta_rule(q, k, v, g, beta, s0, chunk):
    bsz, t, h, dk = q.shape
    dv = v.shape[-1]
    n = t // chunk
    f32 = jnp.float32

    def chunks(a):
        return a.astype(f32).reshape(bsz, n, chunk, h, -1).transpose(1, 0, 3, 2, 4)

    qc = chunks(q) * (dk ** -0.5)
    kc = chunks(k)
    vc = chunks(v)
    gc = jnp.cumsum(chunks(g[..., None])[..., 0], axis=-1)
    bc = chunks(beta[..., None])[..., 0]
    tri = jnp.tril(jnp.ones((chunk, chunk), bool))
    tri_s = jnp.tril(jnp.ones((chunk, chunk), bool), -1)
    decay = jnp.exp(jnp.where(tri, gc[..., :, None] - gc[..., None, :], -jnp.inf))
    kk = jnp.einsum('nbhid,nbhjd->nbhij', kc, kc)
    a_mat = jnp.where(tri_s, bc[..., :, None] * kk * decay, 0.0) + jnp.eye(chunk, dtype=f32)
    rhs = jnp.concatenate([vc * bc[..., None], kc * (bc * jnp.exp(gc))[..., None]], axis=-1)
    sol = lax.linalg.triangular_solve(a_mat, rhs, left_side=True, lower=True, unit_diagonal=True)
    u, w = sol[..., :dv], sol[..., dv:]
    qk = jnp.where(tri, jnp.einsum('nbhid,nbhjd->nbhij', qc, kc) * decay, 0.0)

    def step(s, xs):
        q_i, k_i, u_i, w_i, g_i, qk_i = xs
        v_new = u_i - jnp.einsum('bhck,bhkv->bhcv', w_i, s)
        o_i = (jnp.einsum('bhck,bhkv->bhcv', q_i * jnp.exp(g_i)[..., None], s)
               + jnp.einsum('bhij,bhjv->bhiv', qk_i, v_new))
        g_last = g_i[..., -1:]
        s = (s * jnp.exp(g_last)[..., None]
             + jnp.einsum('bhck,bhcv->bhkv', k_i * jnp.exp(g_last - g_i)[..., None], v_new))
        return s, o_i

    s, o = lax.scan(step, s0.astype(f32), (qc, kc, u, w, gc, qk))
    return o.transpose(1, 0, 3, 2, 4).reshape(bsz, t, h, dv), s


def hybrid_layer(x, c, pos, lw, rel_bias, conv_b_hist, conv_c_hist, delta_s0, k_past, v_past):
    (norm_w, ada_w, ada_b, w_in, q_norm_w, k_norm_w, conv_b_w, conv_c_w,
     a_log, dt_bias, o_norm_w, w_out) = lw
    bsz, t, _ = x.shape
    shift, scale, gate = jnp.split(c @ ada_w + ada_b, 3, axis=-1)
    hn = rms_norm(x, norm_w) * (1.0 + scale[:, None]) + shift[:, None]
    cuts = [int(i) for i in np.cumsum(IN_SIZES)[:-1]]
    (q_a, k_a, v_a, z_a, h_b, bgate_b, cgate_b, z_b,
     qkv_c, z_c, beta_c, a_c) = jnp.split(hn @ w_in, cuts, axis=-1)

    q_a = rms_norm(q_a.reshape(bsz, t, H_A, HEAD_DIM), q_norm_w)
    k_a = rms_norm(k_a.reshape(bsz, t, H_A, HEAD_DIM), k_norm_w)
    v_a = v_a.reshape(bsz, t, H_A, HEAD_DIM)
    if k_past is None:
        k_all, v_all = k_a, v_a
    else:
        k_all = jnp.concatenate([k_past.astype(k_a.dtype), k_a], axis=1)
        v_all = jnp.concatenate([v_past.astype(v_a.dtype), v_a], axis=1)
    att = moba_attention(q_a, k_all, v_all, pos, rel_bias)
    y_a = jax.nn.silu(z_a) * att.reshape(bsz, t, W_A)

    conv_b, new_conv_b = causal_dwconv(cgate_b * h_b, conv_b_hist, conv_b_w)
    y_b = jax.nn.silu(z_b) * bgate_b * conv_b

    qkv, new_conv_c = causal_dwconv(qkv_c, conv_c_hist, conv_c_w)
    qkv = jax.nn.silu(qkv).reshape(bsz, t, 3, H_C, HEAD_DIM)
    beta = jax.nn.sigmoid(beta_c.astype(jnp.float32))
    g = -jnp.exp(a_log.astype(jnp.float32)) * jax.nn.softplus(a_c.astype(jnp.float32) + dt_bias.astype(jnp.float32))
    o_c, s_new = gated_delta_rule(l2_normalize(qkv[:, :, 0]), l2_normalize(qkv[:, :, 1]), qkv[:, :, 2],
                                  g, beta, delta_s0, math.gcd(t, GDN_CHUNK))
    y_c = rms_norm(o_c, o_norm_w).reshape(bsz, t, W_C).astype(x.dtype) * jax.nn.silu(z_c)

    y = jnp.concatenate([y_a, y_b, y_c], axis=-1) @ w_out
    out = x + gate[:, None] * y
    return out, k_a, v_a, new_conv_b, new_conv_c, s_new.astype(x.dtype)


def setup_inputs(seed: int = 0) -> dict:
    key = jax.random.key(seed)
    ks = jax.random.split(key, 24)
    f32 = jnp.float32
    n_pages = PAST_LEN // PAGE_SIZE
    n_used = DEC_BATCH * n_pages
    n_pool = (5 * n_used + 3) // 4

    def nrm(k, shape, s):
        return jax.random.normal(k, shape, f32) * s

    dt = jnp.exp(jax.random.uniform(ks[20], (DEPTH, H_C), f32, math.log(1e-3), math.log(1e-1)))
    return {
        'x_prompt': nrm(ks[0], (BATCH, SEQ, D_MODEL), 1.0),
        'x_sample': nrm(ks[1], (DEC_BATCH, DEC_SEQ, D_MODEL), 1.0),
        'c_prompt': nrm(ks[2], (BATCH, D_MODEL), 1.0),
        'c_sample': nrm(ks[3], (DEC_BATCH, D_MODEL), 1.0),
        'cache_k': nrm(ks[4], (DEPTH, n_pool, PAGE_SIZE, H_A, HEAD_DIM), 1.0),
        'cache_v': nrm(ks[5], (DEPTH, n_pool, PAGE_SIZE, H_A, HEAD_DIM), 1.0),
        'page_table': jax.random.permutation(ks[6], n_pool)[:n_used].reshape(DEC_BATCH, n_pages).astype(jnp.int32),
        'state_conv_b': nrm(ks[7], (DEPTH, DEC_BATCH, CONV_B_WIDTH - 1, W_B), 1.0),
        'state_conv_c': nrm(ks[8], (DEPTH, DEC_BATCH, CONV_C_WIDTH - 1, 3 * W_C), 1.0),
        'state_delta': nrm(ks[9], (DEPTH, DEC_BATCH, H_C, HEAD_DIM, HEAD_DIM), 0.1),
        'norm_w': 1.0 + nrm(ks[10], (DEPTH, D_MODEL), 0.02),
        'ada_w': nrm(ks[11], (DEPTH, D_MODEL, 3 * D_MODEL), 0.5 * D_MODEL ** -0.5),
        'ada_b': nrm(ks[12], (DEPTH, 3 * D_MODEL), 0.02),
        'w_in': nrm(ks[13], (DEPTH, D_MODEL, N_IN), D_MODEL ** -0.5),
        'q_norm_w': 1.0 + nrm(ks[14], (DEPTH, HEAD_DIM), 0.02),
        'k_norm_w': 1.0 + nrm(ks[15], (DEPTH, HEAD_DIM), 0.02),
        'rel_bias': nrm(ks[16], (REL_BUCKETS, H_A), 0.5),
        'conv_b_w': nrm(ks[17], (DEPTH, CONV_B_WIDTH, W_B), CONV_B_WIDTH ** -0.5),
        'conv_c_w': nrm(ks[18], (DEPTH, CONV_C_WIDTH, 3 * W_C), CONV_C_WIDTH ** -0.5),
        'a_log': jnp.log(jax.random.uniform(ks[19], (DEPTH, H_C), f32, 1.0, 16.0)),
        'dt_bias': dt + jnp.log(-jnp.expm1(-dt)),
        'o_norm_w': 1.0 + nrm(ks[21], (DEPTH, HEAD_DIM), 0.02),
        'w_out': nrm(ks[22], (DEPTH, D_MODEL, D_MODEL), D_MODEL ** -0.5),
    }


def reference(x_prompt, x_sample, c_prompt, c_sample, cache_k, cache_v, page_table,
              state_conv_b, state_conv_c, state_delta, norm_w, ada_w, ada_b, w_in,
              q_norm_w, k_norm_w, rel_bias, conv_b_w, conv_c_w, a_log, dt_bias, o_norm_w, w_out):
    bp, tp, _ = x_prompt.shape
    bs, ts, _ = x_sample.shape
    past_len = page_table.shape[1] * cache_k.shape[2]
    pos_p = jnp.arange(tp, dtype=jnp.int32)
    pos_s = past_len + jnp.arange(ts, dtype=jnp.int32)
    hp, hs = x_prompt, x_sample
    kp_l, vp_l, ks_l, vs_l = [], [], [], []
    cbp_l, cbs_l, ccp_l, ccs_l, dp_l, ds_l = [], [], [], [], [], []
    for l in range(DEPTH):
        lw = (norm_w[l], ada_w[l], ada_b[l], w_in[l], q_norm_w[l], k_norm_w[l], conv_b_w[l],
              conv_c_w[l], a_log[l], dt_bias[l], o_norm_w[l], w_out[l])
        hp, kp, vp, cbp, ccp, dp = hybrid_layer(
            hp, c_prompt, pos_p, lw, rel_bias,
            jnp.zeros((bp, CONV_B_WIDTH - 1, W_B), hp.dtype),
            jnp.zeros((bp, CONV_C_WIDTH - 1, 3 * W_C), hp.dtype),
            jnp.zeros((bp, H_C, HEAD_DIM, HEAD_DIM), jnp.float32),
            None, None)
        k_past = cache_k[l, page_table].reshape(bs, past_len, H_A, HEAD_DIM)
        v_past = cache_v[l, page_table].reshape(bs, past_len, H_A, HEAD_DIM)
        hs, ksn, vsn, cbs, ccs, dsn = hybrid_layer(
            hs, c_sample, pos_s, lw, rel_bias,
            state_conv_b[l], state_conv_c[l], state_delta[l], k_past, v_past)
        kp_l.append(kp); vp_l.append(vp); ks_l.append(ksn); vs_l.append(vsn)
        cbp_l.append(cbp); cbs_l.append(cbs); ccp_l.append(ccp); ccs_l.append(ccs)
        dp_l.append(dp); ds_l.append(dsn)
    return (hp, hs, jnp.stack(kp_l), jnp.stack(vp_l), jnp.stack(ks_l), jnp.stack(vs_l),
            jnp.stack(cbp_l), jnp.stack(cbs_l), jnp.stack(ccp_l), jnp.stack(ccs_l),
            jnp.stack(dp_l), jnp.stack(ds_l))
```

```python
import functools
import math

import jax
import jax.numpy as jnp
import numpy as np
from jax import lax
from jax.experimental import pallas as pl
from jax.experimental.pallas import tpu as pltpu

F32 = jnp.float32
BF16 = jnp.bfloat16
HIGHEST = lax.Precision.HIGHEST

HEAD_DIM = 64
MOBA_BLOCK = 256
MOBA_TOPK = 3
GDN_CHUNK = 64
REL_BUCKETS = 32
REL_MAX_DIST = 128
RMS_EPS = 1e-6
ROW_TILE = 256
VMEM_LIMIT = 56 * 1024 * 1024
NEG_INF = float("-inf")

_NT = (((1,), (1,)), ((), ()))


def _cparams(*sem):
    return pltpu.CompilerParams(dimension_semantics=sem, vmem_limit_bytes=VMEM_LIMIT)


def _const_spec(shape):
    zeros = (0,) * len(shape)
    return pl.BlockSpec(shape, lambda *_: zeros)


def _silu(x):
    return x * jax.nn.sigmoid(x)


def _softplus(x):
    return jnp.maximum(x, 0.0) + jnp.log(1.0 + jnp.exp(-jnp.abs(x)))


def _split(a):
    hi = a.astype(BF16)
    lo = (a - hi.astype(F32)).astype(BF16)
    return hi, lo


def _mm3(a, b, dims=None):
    ah, al = _split(a)
    bh, bl = _split(b)
    if dims is None:
        dot = lambda x, y: jnp.dot(x, y, preferred_element_type=F32)
    else:
        dot = lambda x, y: lax.dot_general(x, y, dims, preferred_element_type=F32)
    return dot(ah, bh) + dot(al, bh) + dot(ah, bl)


def _group_sum(x, ones_bd):
    hi, lo = _split(x)
    return (jnp.dot(hi, ones_bd, preferred_element_type=F32)
            + jnp.dot(lo, ones_bd, preferred_element_type=F32))


def _mod_kernel(c_ref, w_ref, b_ref, o_ref):
    o_ref[0] = jnp.dot(c_ref[...], w_ref[0], precision=HIGHEST, preferred_element_type=F32) + b_ref[0]


def _modulation(c_all, ada_w, ada_b):
    depth, d, n3 = ada_w.shape
    rows = c_all.shape[0]
    tn = 512
    return pl.pallas_call(
        _mod_kernel,
        grid=(depth, n3 // tn),
        in_specs=[
            pl.BlockSpec((rows, d), lambda l, n: (0, 0)),
            pl.BlockSpec((1, d, tn), lambda l, n: (l, 0, n)),
            pl.BlockSpec((1, 1, tn), lambda l, n: (l, 0, n)),
        ],
        out_specs=pl.BlockSpec((1, rows, tn), lambda l, n: (l, 0, n)),
        out_shape=jax.ShapeDtypeStruct((depth, rows, n3), F32),
        compiler_params=_cparams("parallel", "parallel"),
        name="modulation",
    )(c_all, ada_w, ada_b.reshape(depth, 1, n3))


def _bias_kernel(dist_ref, tab_ref, o_ref):
    dist = dist_ref[...]
    n = jnp.maximum(dist, 0)
    max_exact = REL_BUCKETS // 2
    nf = jnp.maximum(n, max_exact).astype(F32)
    large = max_exact + (jnp.log(nf / max_exact) / math.log(REL_MAX_DIST / max_exact)
                         * (REL_BUCKETS - max_exact)).astype(jnp.int32)
    bucket = jnp.where(n < max_exact, n, jnp.minimum(large, REL_BUCKETS - 1))
    tab = tab_ref[...]
    acc = jnp.zeros(dist.shape, F32)
    for b in range(REL_BUCKETS):
        acc = jnp.where(bucket == b, tab[:, b:b + 1], acc)
    o_ref[...] = jnp.where(dist >= 0, acc, NEG_INF)


def _bias_tiles(dist, tab):
    rows, cols = dist.shape
    tr = min(rows, 256)
    return pl.pallas_call(
        _bias_kernel,
        grid=(rows // tr,),
        in_specs=[pl.BlockSpec((tr, cols), lambda i: (i, 0)),
                  pl.BlockSpec((tr, REL_BUCKETS), lambda i: (i, 0))],
        out_specs=pl.BlockSpec((tr, cols), lambda i: (i, 0)),
        out_shape=jax.ShapeDtypeStruct((rows, cols), F32),
        compiler_params=_cparams("parallel"),
        name="bias_tiles",
    )(dist, tab)


def _proj_common(x, shift, scale, normw, w_ref, qw, kw, bd512_ref, bd256_ref, alog, dtb, cum_ref):
    ms = jnp.mean(x * x, axis=-1, keepdims=True)
    hn = (x * lax.rsqrt(ms + RMS_EPS) * normw) * (1.0 + scale) + shift
    z = jnp.dot(hn.astype(BF16), w_ref[...], preferred_element_type=F32)
    wa = qw.shape[-1]
    wb = wa // 2
    o = 0
    q = z[:, o:o + wa]; o += wa
    k = z[:, o:o + wa]; o += wa
    v = z[:, o:o + wa]; o += wa
    za = z[:, o:o + wa]; o += wa
    hb = z[:, o:o + wb]; o += wb
    bg = z[:, o:o + wb]; o += wb
    cg = z[:, o:o + wb]; o += wb
    zb = z[:, o:o + wb]; o += wb
    qkv = z[:, o:o + 3 * wb]; o += 3 * wb
    zc = z[:, o:o + wb]; o += wb
    z8 = z[:, o:o + 8]

    bd512 = bd512_ref[...]
    qn = q * lax.rsqrt(_group_sum(q * q, bd512) * (1.0 / HEAD_DIM) + RMS_EPS) * qw
    kn = k * lax.rsqrt(_group_sum(k * k, bd512) * (1.0 / HEAD_DIM) + RMS_EPS) * kw

    lane8 = lax.broadcasted_iota(jnp.int32, z8.shape, 1)
    beta = jax.nn.sigmoid(z8)
    g = -jnp.exp(alog) * _softplus(z8 + dtb)
    bg8 = jnp.where(lane8 < 4, beta, g)
    cum = jnp.dot(cum_ref[...], bg8, precision=HIGHEST, preferred_element_type=F32)
    gb = jnp.where(lane8 < 4, bg8, cum)
    return dict(q=qn * (HEAD_DIM ** -0.5), k=kn, v=v, za=_silu(za), u=cg * hb, zbg=_silu(zb) * bg,
                qkv=qkv, zc=_silu(zc), gb=gb)


def _gdn_qkv(y, bd256_ref):
    wb = y.shape[-1] // 3
    y = _silu(y)
    bd256 = bd256_ref[...]
    qc, kc, vc = y[:, :wb], y[:, wb:2 * wb], y[:, 2 * wb:]
    qc = qc * lax.rsqrt(_group_sum(qc * qc, bd256) + RMS_EPS) * (HEAD_DIM ** -0.5)
    kc = kc * lax.rsqrt(_group_sum(kc * kc, bd256) + RMS_EPS)
    return qc, kc, vc


def _proj_prompt_kernel(x_ref, mod_ref, normw_ref, w_ref, qw_ref, kw_ref, bd512_ref, bd256_ref,
                        cbw_ref, ccw_ref, alog_ref, dtb_ref, cum_ref,
                        q_ref, k_ref, v_ref, km_ref, za_ref, yb_ref, cbs_ref,
                        qc_ref, kc_ref, vc_ref, zc_ref, gb_ref, ccs_ref,
                        carry_b, carry_c):
    ti = pl.program_id(1)
    d = x_ref.shape[-1]
    rows = x_ref.shape[1]

    @pl.when(ti == 0)
    def _():
        carry_b[...] = jnp.zeros_like(carry_b)
        carry_c[...] = jnp.zeros_like(carry_c)

    mod = mod_ref[0]
    p = _proj_common(x_ref[0], mod[:, :d], mod[:, d:2 * d], normw_ref[...], w_ref, qw_ref[...],
                     kw_ref[...], bd512_ref, bd256_ref, alog_ref[...], dtb_ref[...], cum_ref)
    q_ref[0] = p["q"]
    k_ref[0] = p["k"]
    v_ref[0] = p["v"]
    km_ref[0, 0] = jnp.mean(p["k"], axis=0, keepdims=True)
    za_ref[0] = p["za"]
    zc_ref[0] = p["zc"]
    gb_ref[0] = p["gb"]

    u = p["u"]
    ext = jnp.concatenate([carry_b[...], u], axis=0)
    cbw = cbw_ref[...]
    nb = cbw.shape[0]
    yb = u * cbw[nb - 1:nb]
    for j in range(nb - 1):
        s = nb - 1 - j
        yb = yb + ext[8 - s:8 - s + rows] * cbw[j:j + 1]
    yb_ref[0] = p["zbg"] * yb
    carry_b[...] = u[rows - 8:]
    cbs_ref[0] = u[rows - (nb - 1):]

    qkv = p["qkv"]
    extc = jnp.concatenate([carry_c[...], qkv], axis=0)
    ccw = ccw_ref[...]
    nc = ccw.shape[0]
    yc = qkv * ccw[nc - 1:nc]
    for j in range(nc - 1):
        s = nc - 1 - j
        yc = yc + extc[8 - s:8 - s + rows] * ccw[j:j + 1]
    carry_c[...] = qkv[rows - 8:]
    ccs_ref[0] = qkv[rows - (nc - 1):]
    qc, kc, vc = _gdn_qkv(yc, bd256_ref)
    qc_ref[0] = qc
    kc_ref[0] = kc
    vc_ref[0] = vc


def _proj_sample_kernel(x_ref, shift_ref, scale_ref, normw_ref, w_ref, qw_ref, kw_ref, bd512_ref,
                        bd256_ref, cbw_ref, ccw_ref, alog_ref, dtb_ref, cum_ref, hb_ref, hc_ref,
                        q_ref, k_ref, v_ref, za_ref, yb_ref, cbs_ref,
                        qc_ref, kc_ref, vc_ref, zc_ref, gb_ref, ccs_ref, *, nseq):
    rows = x_ref.shape[0]
    p = _proj_common(x_ref[...], shift_ref[...], scale_ref[...], normw_ref[...], w_ref, qw_ref[...],
                     kw_ref[...], bd512_ref, bd256_ref, alog_ref[...], dtb_ref[...], cum_ref)
    q_ref[...] = p["q"]
    k_ref[...] = p["k"]
    v_ref[...] = p["v"]
    za_ref[...] = p["za"]
    zc_ref[...] = p["zc"]
    gb_ref[...] = p["gb"]

    cbw = cbw_ref[...]
    nb = cbw.shape[0]
    ext = jnp.concatenate([hb_ref[...], p["u"]], axis=0)
    yb = ext[0:rows] * cbw[0:1]
    for j in range(1, nb):
        yb = yb + ext[j * nseq:j * nseq + rows] * cbw[j:j + 1]
    yb_ref[...] = p["zbg"] * yb
    cbs_ref[...] = ext[rows:]

    ccw = ccw_ref[...]
    nc = ccw.shape[0]
    extc = jnp.concatenate([hc_ref[...], p["qkv"]], axis=0)
    yc = extc[0:rows] * ccw[0:1]
    for j in range(1, nc):
        yc = yc + extc[j * nseq:j * nseq + rows] * ccw[j:j + 1]
    ccs_ref[...] = extc[rows:]
    qc, kc, vc = _gdn_qkv(yc, bd256_ref)
    qc_ref[...] = qc
    kc_ref[...] = kc
    vc_ref[...] = vc


def _proj_prompt(x, mod, lw):
    b, t, d = x.shape
    wcat = lw["wcat"]
    ncat = wcat.shape[1]
    wa, wb = d // 2, d // 4
    nt = t // ROW_TILE
    row = lambda w: pl.BlockSpec((1, ROW_TILE, w), lambda bi, ti: (bi, ti, 0))
    per_b = lambda r, w: pl.BlockSpec((1, r, w), lambda bi, ti: (bi, 0, 0))
    nb, nc = lw["conv_b_w"].shape[0], lw["conv_c_w"].shape[0]
    out_shape = [
        jax.ShapeDtypeStruct((b, t, wa), F32),
        jax.ShapeDtypeStruct((b, t, wa), F32),
        jax.ShapeDtypeStruct((b, t, wa), F32),
        jax.ShapeDtypeStruct((b, nt, 1, wa), F32),
        jax.ShapeDtypeStruct((b, t, wa), F32),
        jax.ShapeDtypeStruct((b, t, wb), F32),
        jax.ShapeDtypeStruct((b, nb - 1, wb), F32),
        jax.ShapeDtypeStruct((b, t, wb), F32),
        jax.ShapeDtypeStruct((b, t, wb), F32),
        jax.ShapeDtypeStruct((b, t, wb), F32),
        jax.ShapeDtypeStruct((b, t, wb), F32),
        jax.ShapeDtypeStruct((b, t, 8), F32),
        jax.ShapeDtypeStruct((b, nc - 1, 3 * wb), F32),
    ]
    out_specs = [
        row(wa), row(wa), row(wa),
        pl.BlockSpec((1, 1, 1, wa), lambda bi, ti: (bi, ti, 0, 0)),
        row(wa), row(wb), per_b(nb - 1, wb),
        row(wb), row(wb), row(wb), row(wb), row(8), per_b(nc - 1, 3 * wb),
    ]
    in_specs = [
        row(d),
        pl.BlockSpec((1, 1, 3 * d), lambda bi, ti: (bi, 0, 0)),
        _const_spec((1, d)),
        _const_spec((d, ncat)),
        _const_spec((1, wa)), _const_spec((1, wa)),
        _const_spec((wa, wa)), _const_spec((wb, wb)),
        _const_spec((nb, wb)), _const_spec((nc, 3 * wb)),
        _const_spec((1, 8)), _const_spec((1, 8)),
        _const_spec((ROW_TILE, ROW_TILE)),
    ]
    return pl.pallas_call(
        _proj_prompt_kernel,
        grid=(b, nt),
        in_specs=in_specs,
        out_specs=out_specs,
        out_shape=out_shape,
        scratch_shapes=[pltpu.VMEM((8, wb), F32), pltpu.VMEM((8, 3 * wb), F32)],
        compiler_params=_cparams("parallel", "arbitrary"),
        name="proj_prompt",
    )(x, mod.reshape(b, 1, 3 * d), lw["norm_w"], wcat, lw["qw"], lw["kw"], lw["bd512"], lw["bd256"],
      lw["conv_b_w"], lw["conv_c_w"], lw["alog8"], lw["dtb8"], lw["cum_prompt"])


def _proj_sample(x_tm, shift, scale, lw, hist_b, hist_c, nseq):
    rows, d = x_tm.shape
    wa, wb = d // 2, d // 4
    nb, nc = lw["conv_b_w"].shape[0], lw["conv_c_w"].shape[0]
    out_shape = [
        jax.ShapeDtypeStruct((rows, wa), F32), jax.ShapeDtypeStruct((rows, wa), F32),
        jax.ShapeDtypeStruct((rows, wa), F32), jax.ShapeDtypeStruct((rows, wa), F32),
        jax.ShapeDtypeStruct((rows, wb), F32),
        jax.ShapeDtypeStruct(((nb - 1) * nseq, wb), F32),
        jax.ShapeDtypeStruct((rows, wb), F32), jax.ShapeDtypeStruct((rows, wb), F32),
        jax.ShapeDtypeStruct((rows, wb), F32), jax.ShapeDtypeStruct((rows, wb), F32),
        jax.ShapeDtypeStruct((rows, 8), F32),
        jax.ShapeDtypeStruct(((nc - 1) * nseq, 3 * wb), F32),
    ]
    return pl.pallas_call(
        functools.partial(_proj_sample_kernel, nseq=nseq),
        out_shape=out_shape,
        compiler_params=pltpu.CompilerParams(vmem_limit_bytes=VMEM_LIMIT),
        name="proj_sample",
    )(x_tm, shift, scale, lw["norm_w"], lw["wcat"], lw["qw"], lw["kw"], lw["bd512"], lw["bd256"],
      lw["conv_b_w"], lw["conv_c_w"], lw["alog8"], lw["dtb8"], lw["cum_sample"], hist_b, hist_c)


def _attn_prompt_kernel(cfar_ref, q_ref, kb_ref, vt_ref, km_ref, bt_ref, o_ref,
                        sel_s, m_s, l_s, acc_s, *, nheads, nblk):
    i = pl.program_id(1)
    blk = q_ref.shape[1]
    q = q_ref[0]
    qb = q.astype(BF16)
    lane = lax.broadcasted_iota(jnp.int32, (1, 2 * HEAD_DIM), 1)
    half = [lane < HEAD_DIM, lane >= HEAD_DIM]

    km = km_ref[0]
    pieces = []
    for h in range(nheads):
        p, par = divmod(h, 2)
        kmp = jnp.where(half[par], km[:, p * 128:(p + 1) * 128], 0.0)
        pieces.append(lax.dot_general(kmp, q[:, p * 128:(p + 1) * 128], _NT,
                                      precision=HIGHEST, preferred_element_type=F32))
    gate = jnp.concatenate(pieces, axis=0)
    nrow = nheads * nblk
    n = lax.broadcasted_iota(jnp.int32, (nrow, blk), 0) % nblk
    past = n < i
    gate = jnp.where(past, gate, NEG_INF)
    rank = jnp.zeros((nrow, blk), F32)
    for s in range(1, nblk):
        below = pltpu.roll(gate, s, 0)
        rank = rank + jnp.where((n >= s) & (below >= gate), 1.0, 0.0)
        above = pltpu.roll(gate, nrow - s, 0)
        rank = rank + jnp.where((n + s < nblk) & (above > gate), 1.0, 0.0)
    sel_s[...] = jnp.where(past & (rank < MOBA_TOPK), 1.0, 0.0)

    qm = []
    for h in range(nheads):
        p, par = divmod(h, 2)
        qm.append(jnp.where(half[par], qb[:, p * 128:(p + 1) * 128], jnp.zeros((), BF16)))

    def scores(kblk, h):
        p = h // 2
        return lax.dot_general(kblk[:, p * 128:(p + 1) * 128], qm[h], _NT, preferred_element_type=F32)

    kblk = kb_ref[0, i]
    vblk = vt_ref[0, i]
    for h in range(nheads):
        st = scores(kblk, h) + bt_ref[0, h]
        m = jnp.max(st, axis=0, keepdims=True)
        pr = jnp.exp(st - m)
        m_s[h:h + 1, :] = m
        l_s[h:h + 1, :] = jnp.sum(pr, axis=0, keepdims=True)
        acc_s[h * HEAD_DIM:(h + 1) * HEAD_DIM, :] = jnp.dot(
            vblk[h * HEAD_DIM:(h + 1) * HEAD_DIM, :], pr.astype(BF16), preferred_element_type=F32)

    def update(j, near):
        kblk = kb_ref[0, j]
        vblk = vt_ref[0, j]
        for h in range(nheads):
            st = scores(kblk, h)
            if near:
                st = st + bt_ref[1, h]
                shift = 0.0
            else:
                shift = cfar_ref[h]
            on = sel_s[pl.ds(h * nblk + j, 1), :] > 0.0
            m_old = m_s[h:h + 1, :]
            m_blk = jnp.max(st, axis=0, keepdims=True) + shift
            m_new = jnp.maximum(m_old, jnp.where(on, m_blk, NEG_INF))
            alpha = jnp.exp(m_old - m_new)
            pr = jnp.exp(st - (m_new - shift))
            psum = jnp.sum(pr, axis=0, keepdims=True)
            pv = jnp.dot(vblk[h * HEAD_DIM:(h + 1) * HEAD_DIM, :], pr.astype(BF16),
                         preferred_element_type=F32)
            m_s[h:h + 1, :] = m_new
            l_s[h:h + 1, :] = alpha * l_s[h:h + 1, :] + jnp.where(on, psum, 0.0)
            rs = slice(h * HEAD_DIM, (h + 1) * HEAD_DIM)
            acc_s[rs, :] = alpha * acc_s[rs, :] + jnp.where(on, pv, 0.0)

    @pl.when(i >= 1)
    def _():
        update(i - 1, True)

    def far_body(j, carry):
        update(j, False)
        return carry

    lax.fori_loop(0, i - 1, far_body, 0)

    for h in range(nheads):
        rs = slice(h * HEAD_DIM, (h + 1) * HEAD_DIM)
        acc_s[rs, :] = acc_s[rs, :] / l_s[h:h + 1, :]
    o_ref[0] = acc_s[...].T


def _attn_prompt(q, kb, vt, kmean, bt, cfar):
    b, t, wa = q.shape
    nheads = wa // HEAD_DIM
    nblk = t // MOBA_BLOCK
    grid_spec = pltpu.PrefetchScalarGridSpec(
        num_scalar_prefetch=0,
        grid=(b, nblk),
        in_specs=[
            pl.BlockSpec(memory_space=pltpu.SMEM),
            pl.BlockSpec((1, MOBA_BLOCK, wa), lambda bi, i: (bi, i, 0)),
            pl.BlockSpec((1, nblk, MOBA_BLOCK, wa), lambda bi, i: (bi, 0, 0, 0)),
            pl.BlockSpec((1, nblk, wa, MOBA_BLOCK), lambda bi, i: (bi, 0, 0, 0)),
            pl.BlockSpec((1, nblk, wa), lambda bi, i: (bi, 0, 0)),
            pl.BlockSpec((2, nheads, MOBA_BLOCK, MOBA_BLOCK), lambda bi, i: (0, 0, 0, 0)),
        ],
        out_specs=pl.BlockSpec((1, MOBA_BLOCK, wa), lambda bi, i: (bi, i, 0)),
        scratch_shapes=[
            pltpu.VMEM((nheads * nblk, MOBA_BLOCK), F32),
            pltpu.VMEM((nheads, MOBA_BLOCK), F32),
            pltpu.VMEM((nheads, MOBA_BLOCK), F32),
            pltpu.VMEM((wa, MOBA_BLOCK), F32),
        ],
    )
    return pl.pallas_call(
        functools.partial(_attn_prompt_kernel, nheads=nheads, nblk=nblk),
        grid_spec=grid_spec,
        out_shape=jax.ShapeDtypeStruct((b, t, wa), F32),
        compiler_params=_cparams("parallel", "arbitrary"),
        name="attn_prompt",
    )(cfar, q, kb, vt, kmean, bt)


SAMPLE_BLOCKS_PER_STEP = 4


def _attn_sample_kernel(pt_ref, q_ref, kn_ref, vn_ref, blast_ref, cfar_ref, bown_ref, *rest,
                        nheads, nblk, page):
    npg = SAMPLE_BLOCKS_PER_STEP * (MOBA_BLOCK // page)
    k_refs = rest[:npg]
    v_refs = rest[npg:2 * npg]
    o_ref = rest[2 * npg]
    qm_s, qmb_s, km_s, m_s, l_s, acc_s = rest[2 * npg + 1:]
    s = pl.program_id(1)
    nrows = qm_s.shape[0]
    wa = qm_s.shape[1]
    ppb = MOBA_BLOCK // page

    @pl.when(s == 0)
    def _():
        r = lax.broadcasted_iota(jnp.int32, (nrows, wa), 0)
        c = lax.broadcasted_iota(jnp.int32, (nrows, wa), 1)
        qmask = jnp.where(c // HEAD_DIM == r % nheads, q_ref[0], 0.0)
        qm_s[...] = qmask
        qmb_s[...] = qmask.astype(BF16)
        km_s[...] = jnp.zeros(km_s.shape, F32)
        m_s[...] = jnp.full(m_s.shape, NEG_INF, F32)
        l_s[...] = jnp.zeros(l_s.shape, F32)

    lane = lax.broadcasted_iota(jnp.int32, m_s.shape, 1)
    qmb = qmb_s[...]
    for e in range(SAMPLE_BLOCKS_PER_STEP):
        n = s * SAMPLE_BLOCKS_PER_STEP + e
        kblk = jnp.concatenate([k_refs[e * ppb + i][0] for i in range(ppb)], axis=0)
        vblk = jnp.concatenate([v_refs[e * ppb + i][0] for i in range(ppb)], axis=0)
        km_s[pl.ds(n, 1), :] = jnp.mean(kblk, axis=0, keepdims=True)
        st = lax.dot_general(qmb, kblk.astype(BF16), _NT, preferred_element_type=F32)
        st = st + jnp.where(n == nblk - 1, blast_ref[...], cfar_ref[...])
        m = jnp.max(st, axis=1, keepdims=True)
        pr = jnp.exp(st - m)
        l = jnp.sum(pr, axis=1, keepdims=True)
        acc_s[n] = jnp.dot(pr.astype(BF16), vblk.astype(BF16), preferred_element_type=F32)
        m_s[...] = jnp.where(lane == n, m, m_s[...])
        l_s[...] = jnp.where(lane == n, l, l_s[...])

    @pl.when(s == pl.num_programs(1) - 1)
    def _():
        width = m_s.shape[1]
        gate = lax.dot_general(qm_s[...], km_s[...], _NT, precision=HIGHEST,
                               preferred_element_type=F32)
        gate = jnp.where(lane < nblk, gate, NEG_INF)
        rank = jnp.zeros(gate.shape, F32)
        for sh in range(1, width):
            other = pltpu.roll(gate, sh, 1)
            rank = rank + jnp.where(other > gate, 1.0, 0.0)
            rank = rank + jnp.where((other == gate) & (lane >= sh), 1.0, 0.0)
        on = (lane < nblk) & (rank < MOBA_TOPK)

        st = lax.dot_general(qmb, kn_ref[0].astype(BF16), _NT, preferred_element_type=F32) + bown_ref[...]
        m_own = jnp.max(st, axis=1, keepdims=True)
        pr = jnp.exp(st - m_own)
        l_own = jnp.sum(pr, axis=1, keepdims=True)
        acc_own = jnp.dot(pr.astype(BF16), vn_ref[0].astype(BF16), preferred_element_type=F32)

        m_all = m_s[...]
        m_tot = jnp.maximum(jnp.max(jnp.where(on, m_all, NEG_INF), axis=1, keepdims=True), m_own)
        w = jnp.where(on, jnp.exp(m_all - m_tot), 0.0)
        w_own = jnp.exp(m_own - m_tot)
        l_tot = jnp.sum(w * l_s[...], axis=1, keepdims=True) + w_own * l_own
        out = w_own * acc_own
        for nn in range(nblk):
            out = out + w[:, nn:nn + 1] * acc_s[nn]
        out = out / l_tot
        r = lax.broadcasted_iota(jnp.int32, (nrows, wa), 0)
        c = lax.broadcasted_iota(jnp.int32, (nrows, wa), 1)
        out = jnp.where(c // HEAD_DIM == r % nheads, out, 0.0)
        o_ref[0] = jnp.sum(out.reshape(nrows // nheads, nheads, wa), axis=1)


def _attn_sample(q, knew_pad, vnew_pad, cache_k2, cache_v2, page_table, page_off, blast, cfar, bown):
    nseq, nrows, wa = q.shape
    nheads = wa // HEAD_DIM
    tq = nrows // nheads
    page = cache_k2.shape[1]
    n_pages = page_table.shape[1]
    nblk = n_pages * page // MOBA_BLOCK
    npg = SAMPLE_BLOCKS_PER_STEP * (MOBA_BLOCK // page)
    nsteps = nblk // SAMPLE_BLOCKS_PER_STEP
    width = 128

    def page_spec(e):
        return pl.BlockSpec((1, page, wa), lambda b, s, pt: (pt[b, s * npg + e] + page_off, 0, 0))

    seq = lambda r, w: pl.BlockSpec((1, r, w), lambda b, s, pt: (b, 0, 0))
    whole = lambda shape: pl.BlockSpec(shape, lambda b, s, pt: (0,) * len(shape))
    grid_spec = pltpu.PrefetchScalarGridSpec(
        num_scalar_prefetch=1,
        grid=(nseq, nsteps),
        in_specs=[seq(nrows, wa), seq(knew_pad.shape[1], wa), seq(vnew_pad.shape[1], wa),
                  whole(blast.shape), whole(cfar.shape), whole(bown.shape)]
                 + [page_spec(e) for e in range(npg)] + [page_spec(e) for e in range(npg)],
        out_specs=seq(tq, wa),
        scratch_shapes=[
            pltpu.VMEM((nrows, wa), F32), pltpu.VMEM((nrows, wa), BF16),
            pltpu.VMEM((width, wa), F32),
            pltpu.VMEM((nrows, width), F32), pltpu.VMEM((nrows, width), F32),
            pltpu.VMEM((nblk, nrows, wa), F32),
        ],
    )
    return pl.pallas_call(
        functools.partial(_attn_sample_kernel, nheads=nheads, nblk=nblk, page=page),
        grid_spec=grid_spec,
        out_shape=jax.ShapeDtypeStruct((nseq, tq, wa), F32),
        compiler_params=_cparams("parallel", "arbitrary"),
        name="attn_sample",
    )(page_table, q, knew_pad, vnew_pad, blast, cfar, bown,
      *([cache_k2] * npg), *([cache_v2] * npg))


def _gdn_kernel(q_ref, k_ref, v_ref, kt_ref, gbc_ref, gbr_ref, s0_ref, onw_ref, o_ref, sout_ref,
                s_s, m_s, r_s, qe_s, oi_s, *, nheads, nchunk):
    tc = pl.program_id(1)
    c_len = GDN_CHUNK

    @pl.when(tc == 0)
    def _():
        s_s[...] = s0_ref[0]

    row = lax.broadcasted_iota(jnp.int32, (c_len, c_len), 0)
    col = lax.broadcasted_iota(jnp.int32, (c_len, c_len), 1)
    tri = row >= col
    tri_s = row > col
    eye = row == col

    def chunk_a(c, carry):
        gbc = gbc_ref[0, pl.ds(c * c_len, c_len), :]
        gbr = gbr_ref[0, c]
        for h in range(nheads):
            q = q_ref[0, h, pl.ds(c * c_len, c_len), :]
            k = k_ref[0, h, pl.ds(c * c_len, c_len), :]
            v = v_ref[0, h, pl.ds(c * c_len, c_len), :]
            kt = kt_ref[0, h, c]
            bcol = gbc[:, h:h + 1]
            gcol = gbc[:, nheads + h:nheads + h + 1]
            grow = gbr[nheads + h:nheads + h + 1, :]
            glast = gcol[c_len - 1:c_len, :]
            decay = jnp.where(tri, jnp.exp(gcol - grow), 0.0)
            egc = jnp.exp(gcol)
            kk = _mm3(k, k, _NT)
            qk = _mm3(q, k, _NT)
            nmat = jnp.where(tri_s, bcol * kk * decay, 0.0)
            tinv = jnp.where(eye, 1.0, 0.0) - nmat
            pw = nmat
            for _ in range(int(math.log2(c_len)) - 1):
                pw = _mm3(pw, pw)
                tinv = tinv + _mm3(tinv, pw)
            rhs = jnp.concatenate([v * bcol, k * (bcol * egc)], axis=1)
            uw = _mm3(tinv, rhs)
            qkd = jnp.where(tri, qk * decay, 0.0)
            kdt = kt * jnp.exp(glast - grow)
            rm = _mm3(kdt, uw)
            qu = _mm3(qkd, uw)
            m_s[c, h] = jnp.where(eye, jnp.exp(glast), 0.0) - rm[:, HEAD_DIM:]
            r_s[c, h] = rm[:, :HEAD_DIM]
            qe_s[c, h] = q * egc - qu[:, HEAD_DIM:]
            oi_s[c, h] = qu[:, :HEAD_DIM]
        return carry

    lax.fori_loop(0, nchunk, chunk_a, 0)

    onw = onw_ref[...]

    def chunk_b(c, carry):
        for h in range(nheads):
            s = s_s[h]
            o = _mm3(qe_s[c, h], s) + oi_s[c, h]
            s_s[h] = _mm3(m_s[c, h], s) + r_s[c, h]
            o = o * lax.rsqrt(jnp.mean(o * o, axis=-1, keepdims=True) + RMS_EPS) * onw
            o_ref[0, h, pl.ds(c * c_len, c_len), :] = o
        return carry

    lax.fori_loop(0, nchunk, chunk_b, 0)
    sout_ref[0] = s_s[...]


def _gdn(qh, kh, vh, kt, gbc, gbr, s0, onw, nchunk):
    b, nheads, t, hd = qh.shape
    tb = nchunk * GDN_CHUNK
    nt = t // tb
    head = pl.BlockSpec((1, nheads, tb, hd), lambda bi, ti: (bi, 0, ti, 0))
    state = pl.BlockSpec((1, nheads, hd, hd), lambda bi, ti: (bi, 0, 0, 0))
    return pl.pallas_call(
        functools.partial(_gdn_kernel, nheads=nheads, nchunk=nchunk),
        grid=(b, nt),
        in_specs=[head, head, head,
                  pl.BlockSpec((1, nheads, nchunk, hd, GDN_CHUNK), lambda bi, ti: (bi, 0, ti, 0, 0)),
                  pl.BlockSpec((1, tb, 8), lambda bi, ti: (bi, ti, 0)),
                  pl.BlockSpec((1, nchunk, 8, GDN_CHUNK), lambda bi, ti: (bi, ti, 0, 0)),
                  state, _const_spec((1, hd))],
        out_specs=[head, state],
        out_shape=[jax.ShapeDtypeStruct((b, nheads, t, hd), F32),
                   jax.ShapeDtypeStruct((b, nheads, hd, hd), F32)],
        scratch_shapes=[
            pltpu.VMEM((nheads, hd, hd), F32),
            pltpu.VMEM((nchunk, nheads, hd, hd), F32), pltpu.VMEM((nchunk, nheads, hd, hd), F32),
            pltpu.VMEM((nchunk, nheads, GDN_CHUNK, hd), F32), pltpu.VMEM((nchunk, nheads, GDN_CHUNK, hd), F32),
        ],
        compiler_params=_cparams("parallel", "arbitrary"),
        name="deltanet",
    )(qh, kh, vh, kt, gbc, gbr, s0, onw)


def _out_kernel(x_ref, gate_ref, att_ref, za_ref, yb_ref, yc_ref, zc_ref, w_ref, o_ref):
    cat = jnp.concatenate([za_ref[...] * att_ref[...], yb_ref[...], yc_ref[...] * zc_ref[...]], axis=-1)
    y = jnp.dot(cat.astype(BF16), w_ref[...], preferred_element_type=F32)
    o_ref[...] = x_ref[...] + gate_ref[...] * y


def _out_proj(x2, gate2, att2, za2, yb2, yc2, zc2, wout, rows_per_gate):
    rows, d = x2.shape
    tr = min(ROW_TILE, rows)
    wa, wb = att2.shape[1], yb2.shape[1]
    row = lambda w: pl.BlockSpec((tr, w), lambda i: (i, 0))
    if rows_per_gate == 1:
        gate_spec = row(d)
    else:
        per = rows_per_gate // tr
        gate2 = gate2.reshape(gate2.shape[0], 1, d)
        gate_spec = pl.BlockSpec((None, 1, d), lambda i: (i // per, 0, 0))
    return pl.pallas_call(
        _out_kernel,
        grid=(rows // tr,),
        in_specs=[row(d), gate_spec, row(wa), row(wa), row(wb), row(wb), row(wb), _const_spec((d, d))],
        out_specs=row(d),
        out_shape=jax.ShapeDtypeStruct((rows, d), F32),
        compiler_params=_cparams("parallel"),
        name="out_proj",
    )(x2, gate2, att2, za2, yb2, yc2, zc2, wout)


def _block_diag_ones(n, blk, dtype):
    r = np.arange(n)
    return jnp.asarray((r[:, None] // blk) == (r[None, :] // blk), dtype)


def _layer_weights(l, norm_w, w_in, q_norm_w, k_norm_w, conv_b_w, conv_c_w, a_log, dt_bias, o_norm_w,
                   w_out, nseq, ts):
    d = w_in.shape[1]
    wa, wb = d // 2, d // 4
    nheads_a = wa // HEAD_DIM
    hc = wb // HEAD_DIM
    n_main = 4 * wa + 4 * wb + 4 * wb
    w = w_in[l]
    wcat = jnp.concatenate([w[:, :n_main], jnp.pad(w[:, n_main:], ((0, 0), (0, 128 - 2 * hc)))], axis=1)
    r = np.arange(ROW_TILE)
    cum_prompt = ((r[:, None] // GDN_CHUNK == r[None, :] // GDN_CHUNK) & (r[:, None] >= r[None, :]))
    rs = np.arange(nseq * ts)
    cum_sample = ((rs[:, None] % nseq == rs[None, :] % nseq) & (rs[:, None] >= rs[None, :]))
    return dict(
        norm_w=norm_w[l].reshape(1, d),
        wcat=wcat.astype(BF16),
        qw=jnp.tile(q_norm_w[l], nheads_a).reshape(1, wa),
        kw=jnp.tile(k_norm_w[l], nheads_a).reshape(1, wa),
        bd512=_block_diag_ones(wa, HEAD_DIM, BF16),
        bd256=_block_diag_ones(wb, HEAD_DIM, BF16),
        conv_b_w=conv_b_w[l], conv_c_w=conv_c_w[l],
        alog8=jnp.concatenate([jnp.zeros((hc,), F32), a_log[l]]).reshape(1, 2 * hc),
        dtb8=jnp.concatenate([jnp.zeros((hc,), F32), dt_bias[l]]).reshape(1, 2 * hc),
        cum_prompt=jnp.asarray(cum_prompt, F32),
        cum_sample=jnp.asarray(cum_sample, F32),
        onw=o_norm_w[l].reshape(1, HEAD_DIM),
        wout=w_out[l].astype(BF16),
    )


def _heads(a, nheads):
    b, t, _ = a.shape
    return a.reshape(b, t, nheads, HEAD_DIM).transpose(0, 2, 1, 3)


def _gdn_layout(qc, kc, vc, gb, hc):
    b, t, _ = qc.shape
    nch = t // GDN_CHUNK
    kt = kc.reshape(b, nch, GDN_CHUNK, hc, HEAD_DIM).transpose(0, 3, 1, 4, 2)
    gbr = gb.reshape(b, nch, GDN_CHUNK, 2 * hc).transpose(0, 1, 3, 2)
    return _heads(qc, hc), _heads(kc, hc), _heads(vc, hc), kt, gb, gbr


def kernel(x_prompt, x_sample, c_prompt, c_sample, cache_k, cache_v, page_table, state_conv_b,
           state_conv_c, state_delta, norm_w, ada_w, ada_b, w_in, q_norm_w, k_norm_w, rel_bias,
           conv_b_w, conv_c_w, a_log, dt_bias, o_norm_w, w_out):
    bp, tp, d = x_prompt.shape
    bs, ts, _ = x_sample.shape
    depth = ada_w.shape[0]
    wa, wb = d // 2, d // 4
    ha, hc = wa // HEAD_DIM, wb // HEAD_DIM
    n_pool, page = cache_k.shape[1], cache_k.shape[2]
    past_len = page_table.shape[1] * page
    nblk_p = tp // MOBA_BLOCK
    nblk_s = past_len // MOBA_BLOCK
    assert tp % ROW_TILE == 0 and ROW_TILE == MOBA_BLOCK and past_len % MOBA_BLOCK == 0
    assert MOBA_BLOCK % page == 0 and nblk_s % SAMPLE_BLOCKS_PER_STEP == 0 and ts <= GDN_CHUNK

    mod = _modulation(jnp.concatenate([c_prompt, c_sample], axis=0), ada_w, ada_b)

    tab_h = rel_bias.T.astype(F32)
    kq = np.arange(MOBA_BLOCK)
    d_own = kq[None, :] - kq[:, None]
    dist_p = np.stack([np.broadcast_to(d_own + o, (ha, MOBA_BLOCK, MOBA_BLOCK)) for o in (0, MOBA_BLOCK)])
    bt = _bias_tiles(jnp.asarray(dist_p.reshape(-1, MOBA_BLOCK), jnp.int32),
                     jnp.repeat(jnp.tile(tab_h, (2, 1)), MOBA_BLOCK, axis=0))
    bt = bt.reshape(2, ha, MOBA_BLOCK, MOBA_BLOCK)
    cfar_p = bt[1, :, 0, MOBA_BLOCK - 1]

    tq = np.repeat(np.arange(ts), ha)
    pos_q = past_len + tq
    d_last = pos_q[:, None] - ((nblk_s - 1) * MOBA_BLOCK + kq[None, :])
    own_w = 128
    tk = np.arange(own_w)
    d_own_s = np.where(tk[None, :] < ts, tq[:, None] - tk[None, :], -1)
    d_far = np.broadcast_to(pos_q[:, None] - (nblk_s - 2) * MOBA_BLOCK - (MOBA_BLOCK - 1), (ts * ha, 128))
    tab_s = jnp.tile(tab_h, (ts, 1))
    blast = _bias_tiles(jnp.asarray(d_last, jnp.int32), tab_s)
    bown = _bias_tiles(jnp.asarray(d_own_s, jnp.int32), tab_s)
    cfar_s = _bias_tiles(jnp.asarray(d_far, jnp.int32), tab_s)[:, :1]

    cache_k2 = cache_k.reshape(depth * n_pool, page, wa)
    cache_v2 = cache_v.reshape(depth * n_pool, page, wa)

    hp = x_prompt
    hs_tm = x_sample.transpose(1, 0, 2).reshape(ts * bs, d)
    outs = [[] for _ in range(10)]
    for l in range(depth):
        lw = _layer_weights(l, norm_w, w_in, q_norm_w, k_norm_w, conv_b_w, conv_c_w, a_log, dt_bias,
                            o_norm_w, w_out, bs, ts)
        mod_p, mod_s = mod[l, :bp], mod[l, bp:]

        (q, k, v, kmean, za, yb, cbs, qc, kc, vc, zc, gb, ccs) = _proj_prompt(hp, mod_p, lw)
        kb = k.astype(BF16).reshape(bp, nblk_p, MOBA_BLOCK, wa)
        vt = v.astype(BF16).reshape(bp, nblk_p, MOBA_BLOCK, wa).transpose(0, 1, 3, 2)
        att = _attn_prompt(q, kb, vt, kmean.reshape(bp, nblk_p, wa), bt, cfar_p)
        nchunk_p = 8
        yc, s_new = _gdn(*_gdn_layout(qc, kc, vc, gb, hc), jnp.zeros((bp, hc, HEAD_DIM, HEAD_DIM), F32),
                         lw["onw"], nchunk_p)
        yc = yc.transpose(0, 2, 1, 3).reshape(bp * tp, wb)
        hp = _out_proj(hp.reshape(bp * tp, d), mod_p[:, 2 * d:], att.reshape(bp * tp, wa),
                       za.reshape(bp * tp, wa), yb.reshape(bp * tp, wb), yc, zc.reshape(bp * tp, wb),
                       lw["wout"], tp).reshape(bp, tp, d)
        outs[0].append(k.reshape(bp, tp, ha, HEAD_DIM)); outs[1].append(v.reshape(bp, tp, ha, HEAD_DIM))
        outs[4].append(cbs); outs[6].append(ccs); outs[8].append(s_new)

        rep = lambda a: jnp.tile(a, (ts, 1))
        hist_b = state_conv_b[l].transpose(1, 0, 2).reshape(-1, wb)
        hist_c = state_conv_c[l].transpose(1, 0, 2).reshape(-1, 3 * wb)
        (q, k, v, za, yb, cbs, qc, kc, vc, zc, gb, ccs) = _proj_sample(
            hs_tm, rep(mod_s[:, :d]), rep(mod_s[:, d:2 * d]), lw, hist_b, hist_c, bs)
        seq_major = lambda a: a.reshape(ts, bs, -1).transpose(1, 0, 2)
        pad_keys = lambda a: jnp.pad(seq_major(a), ((0, 0), (0, own_w - ts), (0, 0)))
        att = _attn_sample(jnp.repeat(seq_major(q), ha, axis=1), pad_keys(k), pad_keys(v),
                           cache_k2, cache_v2, page_table,
                           l * n_pool, blast, cfar_s, bown)
        att = att.transpose(1, 0, 2).reshape(ts * bs, wa)
        pad_t = lambda a: jnp.pad(seq_major(a), ((0, 0), (0, GDN_CHUNK - ts), (0, 0)))
        gbs = seq_major(gb)
        gb_pad = jnp.concatenate(
            [pad_t(gb[:, :hc]), jnp.pad(gbs[:, :, hc:], ((0, 0), (0, GDN_CHUNK - ts), (0, 0)), mode="edge")],
            axis=-1)
        yc, s_new = _gdn(*_gdn_layout(pad_t(qc), pad_t(kc), pad_t(vc), gb_pad, hc), state_delta[l],
                         lw["onw"], 1)
        yc = yc[:, :, :ts, :].transpose(2, 0, 1, 3).reshape(ts * bs, wb)
        hs_tm = _out_proj(hs_tm, rep(mod_s[:, 2 * d:]), att, za, yb, yc, zc, lw["wout"], 1)
        outs[2].append(seq_major(k).reshape(bs, ts, ha, HEAD_DIM))
        outs[3].append(seq_major(v).reshape(bs, ts, ha, HEAD_DIM))
        outs[5].append(cbs.reshape(-1, bs, wb).transpose(1, 0, 2))
        outs[7].append(ccs.reshape(-1, bs, 3 * wb).transpose(1, 0, 2))
        outs[9].append(s_new)

    y_sample = hs_tm.reshape(ts, bs, d).transpose(1, 0, 2)
    return (hp, y_sample) + tuple(jnp.stack(o) for o in outs)
```

```python
import functools
import math

import jax
import jax.numpy as jnp
import numpy as np
from jax import lax
from jax.experimental import pallas as pl
from jax.experimental.pallas import tpu as pltpu

F32 = jnp.float32
BF16 = jnp.bfloat16
HIGHEST = lax.Precision.HIGHEST

HEAD_DIM = 64
MOBA_BLOCK = 256
MOBA_TOPK = 3
GDN_CHUNK = 64
INV_BASE = 8
REL_BUCKETS = 32
REL_MAX_DIST = 128
RMS_EPS = 1e-6
ROW_TILE = 256
VMEM_LIMIT = 56 * 1024 * 1024
NEG_INF = float("-inf")

_NT = (((1,), (1,)), ((), ()))


def _cparams(*sem):
    return pltpu.CompilerParams(dimension_semantics=sem, vmem_limit_bytes=VMEM_LIMIT)


def _const_spec(shape):
    zeros = (0,) * len(shape)
    return pl.BlockSpec(shape, lambda *_: zeros)


def _silu(x):
    return x * jax.nn.sigmoid(x)


def _softplus(x):
    return jnp.maximum(x, 0.0) + jnp.log(1.0 + jnp.exp(-jnp.abs(x)))


def _split(a):
    hi = a.astype(BF16)
    lo = (a - hi.astype(F32)).astype(BF16)
    return hi, lo


def _dot(a, b, dims=None):
    if dims is None:
        return jnp.dot(a, b, preferred_element_type=F32)
    return lax.dot_general(a, b, dims, preferred_element_type=F32)


def _mm1(a, b, dims=None):
    return _dot(a.astype(BF16), b.astype(BF16), dims)


def _mm2(a, b, dims=None):
    ah = a.astype(BF16)
    bh, bl = _split(b)
    return _dot(ah, bh, dims) + _dot(ah, bl, dims)


def _mm3(a, b, dims=None):
    ah, al = _split(a)
    bh, bl = _split(b)
    return _dot(ah, bh, dims) + _dot(al, bh, dims) + _dot(ah, bl, dims)


def _group_sum(x, ones_bd):
    hi, lo = _split(x)
    return (jnp.dot(hi, ones_bd, preferred_element_type=F32)
            + jnp.dot(lo, ones_bd, preferred_element_type=F32))


def _mod_kernel(c_ref, w_ref, b_ref, o_ref):
    o_ref[0] = jnp.dot(c_ref[...], w_ref[0], precision=HIGHEST, preferred_element_type=F32) + b_ref[0]


def _modulation(c_all, ada_w, ada_b):
    depth, d, n3 = ada_w.shape
    rows = c_all.shape[0]
    tn = 512
    return pl.pallas_call(
        _mod_kernel,
        grid=(depth, n3 // tn),
        in_specs=[
            pl.BlockSpec((rows, d), lambda l, n: (0, 0)),
            pl.BlockSpec((1, d, tn), lambda l, n: (l, 0, n)),
            pl.BlockSpec((1, 1, tn), lambda l, n: (l, 0, n)),
        ],
        out_specs=pl.BlockSpec((1, rows, tn), lambda l, n: (l, 0, n)),
        out_shape=jax.ShapeDtypeStruct((depth, rows, n3), F32),
        compiler_params=_cparams("parallel", "parallel"),
        name="modulation",
    )(c_all, ada_w, ada_b.reshape(depth, 1, n3))


def _bias_kernel(dist_ref, tab_ref, o_ref):
    dist = dist_ref[...]
    n = jnp.maximum(dist, 0)
    max_exact = REL_BUCKETS // 2
    nf = jnp.maximum(n, max_exact).astype(F32)
    large = max_exact + (jnp.log(nf / max_exact) / math.log(REL_MAX_DIST / max_exact)
                         * (REL_BUCKETS - max_exact)).astype(jnp.int32)
    bucket = jnp.where(n < max_exact, n, jnp.minimum(large, REL_BUCKETS - 1))
    tab = tab_ref[...]
    acc = jnp.zeros(dist.shape, F32)
    for b in range(REL_BUCKETS):
        acc = jnp.where(bucket == b, tab[:, b:b + 1], acc)
    o_ref[...] = jnp.where(dist >= 0, acc, NEG_INF)


def _bias_tiles(dist, tab):
    rows, cols = dist.shape
    tr = min(rows, 256)
    return pl.pallas_call(
        _bias_kernel,
        grid=(rows // tr,),
        in_specs=[pl.BlockSpec((tr, cols), lambda i: (i, 0)),
                  pl.BlockSpec((tr, REL_BUCKETS), lambda i: (i, 0))],
        out_specs=pl.BlockSpec((tr, cols), lambda i: (i, 0)),
        out_shape=jax.ShapeDtypeStruct((rows, cols), F32),
        compiler_params=_cparams("parallel"),
        name="bias_tiles",
    )(dist, tab)


def _proj_common(x, shift, scale, normw, w_ref, qw, kw, bd512_ref, bd256_ref, alog, dtb, cum_ref):
    ms = jnp.mean(x * x, axis=-1, keepdims=True)
    hn = (x * lax.rsqrt(ms + RMS_EPS) * normw) * (1.0 + scale) + shift
    z = jnp.dot(hn.astype(BF16), w_ref[...], preferred_element_type=F32)
    wa = qw.shape[-1]
    wb = wa // 2
    o = 0
    q = z[:, o:o + wa]; o += wa
    k = z[:, o:o + wa]; o += wa
    v = z[:, o:o + wa]; o += wa
    za = z[:, o:o + wa]; o += wa
    hb = z[:, o:o + wb]; o += wb
    bg = z[:, o:o + wb]; o += wb
    cg = z[:, o:o + wb]; o += wb
    zb = z[:, o:o + wb]; o += wb
    qkv = z[:, o:o + 3 * wb]; o += 3 * wb
    zc = z[:, o:o + wb]; o += wb
    z8 = z[:, o:o + 8]

    bd512 = bd512_ref[...]
    qn = q * lax.rsqrt(_group_sum(q * q, bd512) * (1.0 / HEAD_DIM) + RMS_EPS) * qw
    kn = k * lax.rsqrt(_group_sum(k * k, bd512) * (1.0 / HEAD_DIM) + RMS_EPS) * kw

    lane8 = lax.broadcasted_iota(jnp.int32, z8.shape, 1)
    beta = jax.nn.sigmoid(z8)
    g = -jnp.exp(alog) * _softplus(z8 + dtb)
    bg8 = jnp.where(lane8 < 4, beta, g)
    cum = jnp.dot(cum_ref[...], bg8, precision=HIGHEST, preferred_element_type=F32)
    gb = jnp.where(lane8 < 4, bg8, cum)
    return dict(q=qn * (HEAD_DIM ** -0.5), k=kn, v=v, za=_silu(za), u=cg * hb, zbg=_silu(zb) * bg,
                qkv=qkv, zc=_silu(zc), gb=gb)


def _gdn_qkv(y, bd256_ref):
    wb = y.shape[-1] // 3
    y = _silu(y)
    bd256 = bd256_ref[...]
    qc, kc, vc = y[:, :wb], y[:, wb:2 * wb], y[:, 2 * wb:]
    qc = qc * lax.rsqrt(_group_sum(qc * qc, bd256) + RMS_EPS) * (HEAD_DIM ** -0.5)
    kc = kc * lax.rsqrt(_group_sum(kc * kc, bd256) + RMS_EPS)
    return qc, kc, vc


def _proj_prompt_kernel(x_ref, mod_ref, normw_ref, w_ref, qw_ref, kw_ref, bd512_ref, bd256_ref,
                        cbw_ref, ccw_ref, alog_ref, dtb_ref, cum_ref,
                        q_ref, k_ref, v_ref, km_ref, za_ref, yb_ref, cbs_ref,
                        qc_ref, kc_ref, vc_ref, zc_ref, gb_ref, ccs_ref,
                        carry_b, carry_c):
    ti = pl.program_id(1)
    d = x_ref.shape[-1]
    rows = x_ref.shape[1]

    @pl.when(ti == 0)
    def _():
        carry_b[...] = jnp.zeros_like(carry_b)
        carry_c[...] = jnp.zeros_like(carry_c)

    mod = mod_ref[0]
    p = _proj_common(x_ref[0], mod[:, :d], mod[:, d:2 * d], normw_ref[...], w_ref, qw_ref[...],
                     kw_ref[...], bd512_ref, bd256_ref, alog_ref[...], dtb_ref[...], cum_ref)
    q_ref[0] = p["q"]
    k_ref[0] = p["k"]
    v_ref[0] = p["v"]
    km_ref[0, 0] = jnp.mean(p["k"], axis=0, keepdims=True)
    za_ref[0] = p["za"]
    zc_ref[0] = p["zc"]
    gb_ref[0] = p["gb"]

    u = p["u"]
    ext = jnp.concatenate([carry_b[...], u], axis=0)
    cbw = cbw_ref[...]
    nb = cbw.shape[0]
    yb = u * cbw[nb - 1:nb]
    for j in range(nb - 1):
        s = nb - 1 - j
        yb = yb + ext[8 - s:8 - s + rows] * cbw[j:j + 1]
    yb_ref[0] = p["zbg"] * yb
    carry_b[...] = u[rows - 8:]
    cbs_ref[0] = u[rows - (nb - 1):]

    qkv = p["qkv"]
    extc = jnp.concatenate([carry_c[...], qkv], axis=0)
    ccw = ccw_ref[...]
    nc = ccw.shape[0]
    yc = qkv * ccw[nc - 1:nc]
    for j in range(nc - 1):
        s = nc - 1 - j
        yc = yc + extc[8 - s:8 - s + rows] * ccw[j:j + 1]
    carry_c[...] = qkv[rows - 8:]
    ccs_ref[0] = qkv[rows - (nc - 1):]
    qc, kc, vc = _gdn_qkv(yc, bd256_ref)
    qc_ref[0] = qc
    kc_ref[0] = kc
    vc_ref[0] = vc


def _proj_sample_kernel(x_ref, shift_ref, scale_ref, normw_ref, w_ref, qw_ref, kw_ref, bd512_ref,
                        bd256_ref, cbw_ref, ccw_ref, alog_ref, dtb_ref, cum_ref, hb_ref, hc_ref,
                        q_ref, k_ref, v_ref, za_ref, yb_ref, cbs_ref,
                        qc_ref, kc_ref, vc_ref, zc_ref, gb_ref, ccs_ref, *, nseq):
    rows = x_ref.shape[0]
    p = _proj_common(x_ref[...], shift_ref[...], scale_ref[...], normw_ref[...], w_ref, qw_ref[...],
                     kw_ref[...], bd512_ref, bd256_ref, alog_ref[...], dtb_ref[...], cum_ref)
    q_ref[...] = p["q"]
    k_ref[...] = p["k"]
    v_ref[...] = p["v"]
    za_ref[...] = p["za"]
    zc_ref[...] = p["zc"]
    gb_ref[...] = p["gb"]

    cbw = cbw_ref[...]
    nb = cbw.shape[0]
    ext = jnp.concatenate([hb_ref[...], p["u"]], axis=0)
    yb = ext[0:rows] * cbw[0:1]
    for j in range(1, nb):
        yb = yb + ext[j * nseq:j * nseq + rows] * cbw[j:j + 1]
    yb_ref[...] = p["zbg"] * yb
    cbs_ref[...] = ext[rows:]

    ccw = ccw_ref[...]
    nc = ccw.shape[0]
    extc = jnp.concatenate([hc_ref[...], p["qkv"]], axis=0)
    yc = extc[0:rows] * ccw[0:1]
    for j in range(1, nc):
        yc = yc + extc[j * nseq:j * nseq + rows] * ccw[j:j + 1]
    ccs_ref[...] = extc[rows:]
    qc, kc, vc = _gdn_qkv(yc, bd256_ref)
    qc_ref[...] = qc
    kc_ref[...] = kc
    vc_ref[...] = vc


def _proj_prompt(x, mod, lw):
    b, t, d = x.shape
    wcat = lw["wcat"]
    ncat = wcat.shape[1]
    wa, wb = d // 2, d // 4
    nt = t // ROW_TILE
    row = lambda w: pl.BlockSpec((1, ROW_TILE, w), lambda bi, ti: (bi, ti, 0))
    per_b = lambda r, w: pl.BlockSpec((1, r, w), lambda bi, ti: (bi, 0, 0))
    nb, nc = lw["conv_b_w"].shape[0], lw["conv_c_w"].shape[0]
    out_shape = [
        jax.ShapeDtypeStruct((b, t, wa), F32),
        jax.ShapeDtypeStruct((b, t, wa), F32),
        jax.ShapeDtypeStruct((b, t, wa), F32),
        jax.ShapeDtypeStruct((b, nt, 1, wa), F32),
        jax.ShapeDtypeStruct((b, t, wa), F32),
        jax.ShapeDtypeStruct((b, t, wb), F32),
        jax.ShapeDtypeStruct((b, nb - 1, wb), F32),
        jax.ShapeDtypeStruct((b, t, wb), F32),
        jax.ShapeDtypeStruct((b, t, wb), F32),
        jax.ShapeDtypeStruct((b, t, wb), F32),
        jax.ShapeDtypeStruct((b, t, wb), F32),
        jax.ShapeDtypeStruct((b, t, 8), F32),
        jax.ShapeDtypeStruct((b, nc - 1, 3 * wb), F32),
    ]
    out_specs = [
        row(wa), row(wa), row(wa),
        pl.BlockSpec((1, 1, 1, wa), lambda bi, ti: (bi, ti, 0, 0)),
        row(wa), row(wb), per_b(nb - 1, wb),
        row(wb), row(wb), row(wb), row(wb), row(8), per_b(nc - 1, 3 * wb),
    ]
    in_specs = [
        row(d),
        pl.BlockSpec((1, 1, 3 * d), lambda bi, ti: (bi, 0, 0)),
        _const_spec((1, d)),
        _const_spec((d, ncat)),
        _const_spec((1, wa)), _const_spec((1, wa)),
        _const_spec((wa, wa)), _const_spec((wb, wb)),
        _const_spec((nb, wb)), _const_spec((nc, 3 * wb)),
        _const_spec((1, 8)), _const_spec((1, 8)),
        _const_spec((ROW_TILE, ROW_TILE)),
    ]
    return pl.pallas_call(
        _proj_prompt_kernel,
        grid=(b, nt),
        in_specs=in_specs,
        out_specs=out_specs,
        out_shape=out_shape,
        scratch_shapes=[pltpu.VMEM((8, wb), F32), pltpu.VMEM((8, 3 * wb), F32)],
        compiler_params=_cparams("parallel", "arbitrary"),
        name="proj_prompt",
    )(x, mod.reshape(b, 1, 3 * d), lw["norm_w"], wcat, lw["qw"], lw["kw"], lw["bd512"], lw["bd256"],
      lw["conv_b_w"], lw["conv_c_w"], lw["alog8"], lw["dtb8"], lw["cum_prompt"])


def _proj_sample(x_tm, shift, scale, lw, hist_b, hist_c, nseq):
    rows, d = x_tm.shape
    wa, wb = d // 2, d // 4
    nb, nc = lw["conv_b_w"].shape[0], lw["conv_c_w"].shape[0]
    out_shape = [
        jax.ShapeDtypeStruct((rows, wa), F32), jax.ShapeDtypeStruct((rows, wa), F32),
        jax.ShapeDtypeStruct((rows, wa), F32), jax.ShapeDtypeStruct((rows, wa), F32),
        jax.ShapeDtypeStruct((rows, wb), F32),
        jax.ShapeDtypeStruct(((nb - 1) * nseq, wb), F32),
        jax.ShapeDtypeStruct((rows, wb), F32), jax.ShapeDtypeStruct((rows, wb), F32),
        jax.ShapeDtypeStruct((rows, wb), F32), jax.ShapeDtypeStruct((rows, wb), F32),
        jax.ShapeDtypeStruct((rows, 8), F32),
        jax.ShapeDtypeStruct(((nc - 1) * nseq, 3 * wb), F32),
    ]
    return pl.pallas_call(
        functools.partial(_proj_sample_kernel, nseq=nseq),
        out_shape=out_shape,
        compiler_params=pltpu.CompilerParams(vmem_limit_bytes=VMEM_LIMIT),
        name="proj_sample",
    )(x_tm, shift, scale, lw["norm_w"], lw["wcat"], lw["qw"], lw["kw"], lw["bd512"], lw["bd256"],
      lw["conv_b_w"], lw["conv_c_w"], lw["alog8"], lw["dtb8"], lw["cum_sample"], hist_b, hist_c)


def _attn_prompt_kernel(cfar_ref, q_ref, kb_ref, vt_ref, km_ref, bt_ref, o_ref,
                        sel_s, acc_s, *, nheads, nblk):
    i = pl.program_id(1)
    blk = q_ref.shape[1]
    qt = q_ref[0].T
    qtb = qt.astype(BF16)
    lane = lax.broadcasted_iota(jnp.int32, (1, 2 * HEAD_DIM), 1)
    sub = lax.broadcasted_iota(jnp.int32, (2 * HEAD_DIM, 1), 0)
    lane_half = [lane < HEAD_DIM, lane >= HEAD_DIM]
    sub_half = [sub < HEAD_DIM, sub >= HEAD_DIM]

    km = km_ref[0]
    pieces = []
    for h in range(nheads):
        p, par = divmod(h, 2)
        kmp = jnp.where(lane_half[par], km[:, p * 128:(p + 1) * 128], 0.0)
        pieces.append(jnp.dot(kmp, qt[p * 128:(p + 1) * 128, :], precision=HIGHEST,
                              preferred_element_type=F32))
    gate = jnp.concatenate(pieces, axis=0)
    nrow = nheads * nblk
    n = lax.broadcasted_iota(jnp.int32, (nrow, blk), 0) % nblk
    past = n < i
    gate = jnp.where(past, gate, NEG_INF)
    rank = jnp.zeros((nrow, blk), F32)
    for s in range(1, nblk):
        below = pltpu.roll(gate, s, 0)
        rank = rank + jnp.where((n >= s) & (below >= gate), 1.0, 0.0)
        above = pltpu.roll(gate, nrow - s, 0)
        rank = rank + jnp.where((n + s < nblk) & (above > gate), 1.0, 0.0)
    sel_s[...] = jnp.where(past & (rank < MOBA_TOPK), 1.0, 0.0)

    qm = []
    for h in range(nheads):
        p, par = divmod(h, 2)
        qm.append(jnp.where(sub_half[par], qtb[p * 128:(p + 1) * 128, :], jnp.zeros((), BF16)))

    def scores(kblk, h):
        p = h // 2
        return jnp.dot(kblk[:, p * 128:(p + 1) * 128], qm[h], preferred_element_type=F32)

    def pv_dot(vblk, h, pr):
        return jnp.dot(vblk[h * HEAD_DIM:(h + 1) * HEAD_DIM, :], pr.astype(BF16), preferred_element_type=F32)


    kblk = kb_ref[0, i]
    vblk = vt_ref[0, i]
    st_next = scores(kblk, 0)
    m_rows, l_rows = [], []
    for h in range(nheads):
        st = st_next + bt_ref[0, h]
        if h + 1 < nheads:
            st_next = scores(kblk, h + 1)
        m = jnp.max(st, axis=0, keepdims=True)
        pr = jnp.exp(st - m)
        m_rows.append(m)
        l_rows.append(jnp.sum(pr, axis=0, keepdims=True))
        acc_s[h * HEAD_DIM:(h + 1) * HEAD_DIM, :] = pv_dot(vblk, h, pr)
    m_all = jnp.concatenate(m_rows, axis=0)
    l_all = jnp.concatenate(l_rows, axis=0)

    def update(j, near, m_all, l_all):
        kblk = kb_ref[0, j]
        vblk = vt_ref[0, j]
        st_next = scores(kblk, 0)
        m_rows, l_rows = [], []
        for h in range(nheads):
            st = st_next
            if h + 1 < nheads:
                st_next = scores(kblk, h + 1)
            if near:
                st = st + bt_ref[1, h]
                shift = 0.0
            else:
                shift = cfar_ref[h]
            on = sel_s[pl.ds(h * nblk + j, 1), :] > 0.0
            m_old = m_all[h:h + 1, :]
            m_blk = jnp.max(st, axis=0, keepdims=True) + shift
            m_new = jnp.maximum(m_old, jnp.where(on, m_blk, NEG_INF))
            alpha = jnp.exp(m_old - m_new)
            pr = jnp.exp(st - (m_new - shift))
            psum = jnp.sum(pr, axis=0, keepdims=True)
            pv = pv_dot(vblk, h, pr)
            m_rows.append(m_new)
            l_rows.append(alpha * l_all[h:h + 1, :] + jnp.where(on, psum, 0.0))
            rs = slice(h * HEAD_DIM, (h + 1) * HEAD_DIM)
            acc_s[rs, :] = alpha * acc_s[rs, :] + jnp.where(on, pv, 0.0)
        return jnp.concatenate(m_rows, axis=0), jnp.concatenate(l_rows, axis=0)

    m_all, l_all = lax.cond(i >= 1, lambda m, l: update(i - 1, True, m, l), lambda m, l: (m, l), m_all, l_all)
    m_all, l_all = lax.fori_loop(0, i - 1, lambda j, c: update(j, False, *c), (m_all, l_all))

    for h in range(nheads):
        rs = slice(h * HEAD_DIM, (h + 1) * HEAD_DIM)
        acc_s[rs, :] = acc_s[rs, :] / l_all[h:h + 1, :]
    o_ref[0] = acc_s[...].T


def _attn_prompt(q, kb, vt, kmean, bt, cfar):
    b, t, wa = q.shape
    nheads = wa // HEAD_DIM
    nblk = t // MOBA_BLOCK
    grid_spec = pltpu.PrefetchScalarGridSpec(
        num_scalar_prefetch=0,
        grid=(b, nblk),
        in_specs=[
            pl.BlockSpec(memory_space=pltpu.SMEM),
            pl.BlockSpec((1, MOBA_BLOCK, wa), lambda bi, i: (bi, i, 0)),
            pl.BlockSpec((1, nblk, MOBA_BLOCK, wa), lambda bi, i: (bi, 0, 0, 0)),
            pl.BlockSpec((1, nblk, wa, MOBA_BLOCK), lambda bi, i: (bi, 0, 0, 0)),
            pl.BlockSpec((1, nblk, wa), lambda bi, i: (bi, 0, 0)),
            pl.BlockSpec((2, nheads, MOBA_BLOCK, MOBA_BLOCK), lambda bi, i: (0, 0, 0, 0)),
        ],
        out_specs=pl.BlockSpec((1, MOBA_BLOCK, wa), lambda bi, i: (bi, i, 0)),
        scratch_shapes=[
            pltpu.VMEM((nheads * nblk, MOBA_BLOCK), F32),
            pltpu.VMEM((wa, MOBA_BLOCK), F32),
        ],
    )
    return pl.pallas_call(
        functools.partial(_attn_prompt_kernel, nheads=nheads, nblk=nblk),
        grid_spec=grid_spec,
        out_shape=jax.ShapeDtypeStruct((b, t, wa), F32),
        compiler_params=_cparams("parallel", "arbitrary"),
        name="attn_prompt",
    )(cfar, q, kb, vt, kmean, bt)


SAMPLE_PAGES_PER_STEP = 16


def _sample_scores_kernel(pt_ref, qt_ref, *rest, nheads, nblk, page, tq):
    npg = SAMPLE_PAGES_PER_STEP
    k_refs = rest[:npg]
    s_ref, idx_ref, qb_s, gate_s = rest[npg:]
    s = pl.program_id(1)
    nrows = tq * nheads
    ppb = MOBA_BLOCK // page

    @pl.when(s == 0)
    def _():
        qt = qt_ref[0]
        for t in range(tq):
            for h in range(nheads):
                qb_s[t * nheads + h] = jnp.broadcast_to(qt[h * HEAD_DIM:(h + 1) * HEAD_DIM, t:t + 1],
                                                        (HEAD_DIM, page))
        gate_s[...] = jnp.full(gate_s.shape, NEG_INF, F32)

    lane = lax.broadcasted_iota(jnp.int32, gate_s.shape, 1)
    bsum = None
    for e in range(npg):
        groups = []
        for t in range(tq):
            rows = [jnp.sum(k_refs[e][h] * qb_s[t * nheads + h], axis=0, keepdims=True) for h in range(nheads)]
            groups.append(jnp.concatenate(rows, axis=0))
        sp = jnp.concatenate(groups, axis=0)
        s_ref[0, e // ppb, :, (e % ppb) * page:(e % ppb + 1) * page] = sp
        rs = jnp.sum(sp, axis=1, keepdims=True)
        bsum = rs if e % ppb == 0 else bsum + rs
        if e % ppb == ppb - 1:
            n = s * (npg // ppb) + e // ppb
            gate_s[...] = jnp.where(lane == n, bsum * (1.0 / MOBA_BLOCK), gate_s[...])

    @pl.when(s == pl.num_programs(1) - 1)
    def _():
        gate = gate_s[...]
        rank = jnp.zeros(gate.shape, F32)
        for sh in range(1, gate.shape[1]):
            other = pltpu.roll(gate, sh, 1)
            rank = rank + jnp.where(other > gate, 1.0, 0.0)
            rank = rank + jnp.where((other == gate) & (lane >= sh), 1.0, 0.0)
        lane_f = lane.astype(F32)
        out = jnp.zeros(gate.shape, F32)
        for j in range(MOBA_TOPK):
            pick = jnp.sum(jnp.where((rank == j) & (lane < nblk), lane_f, 0.0), axis=1, keepdims=True)
            out = jnp.where(lane == j, pick, out)
        idx_ref[0] = out.astype(jnp.int32)


def _sample_scores(qt, cache_kt, page_table, layer):
    nseq, wa, tq = qt.shape
    nheads = wa // HEAD_DIM
    page = cache_kt.shape[-1]
    n_pages = page_table.shape[1]
    nblk = n_pages * page // MOBA_BLOCK
    npg = SAMPLE_PAGES_PER_STEP
    nrows = tq * nheads

    def page_spec(e):
        return pl.BlockSpec((None, None, nheads, HEAD_DIM, page),
                            lambda b, s, pt: (layer, pt[b, s * npg + e], 0, 0, 0))

    grid_spec = pltpu.PrefetchScalarGridSpec(
        num_scalar_prefetch=1,
        grid=(nseq, n_pages // npg),
        in_specs=[pl.BlockSpec((1, wa, tq), lambda b, s, pt: (b, 0, 0))] + [page_spec(e) for e in range(npg)],
        out_specs=[pl.BlockSpec((1, npg * page // MOBA_BLOCK, nrows, MOBA_BLOCK), lambda b, s, pt: (b, s, 0, 0)),
                   pl.BlockSpec((1, nrows, 128), lambda b, s, pt: (b, 0, 0))],
        scratch_shapes=[pltpu.VMEM((nrows, HEAD_DIM, page), F32), pltpu.VMEM((nrows, 128), F32)],
    )
    return pl.pallas_call(
        functools.partial(_sample_scores_kernel, nheads=nheads, nblk=nblk, page=page, tq=tq),
        grid_spec=grid_spec,
        out_shape=[jax.ShapeDtypeStruct((nseq, nblk, nrows, MOBA_BLOCK), F32),
                   jax.ShapeDtypeStruct((nseq, nrows, 128), jnp.int32)],
        compiler_params=_cparams("parallel", "arbitrary"),
        name="sample_scores",
    )(page_table, qt, *([cache_kt] * npg))


def _sample_attend_kernel(pt_ref, idx_ref, s_ref, qt_ref, knt_ref, vnt_ref, blast_ref, cfar_ref, bown_ref,
                          cv_ref, o_ref, vbuf, sems, *, nheads, nblk, page, tq, layer):
    b = pl.program_id(0)
    nrows = tq * nheads
    ppb = MOBA_BLOCK // page

    def slab_copies(p):
        h = p % nheads
        out = []
        for j in range(MOBA_TOPK):
            n = idx_ref[b, p * MOBA_TOPK + j]
            for e in range(ppb):
                pg = pt_ref[b, n * ppb + e]
                out.append(pltpu.make_async_copy(cv_ref.at[layer, pg, h], vbuf.at[p, j * ppb + e], sems.at[p]))
        return out

    for p in range(nrows):
        for cp in slab_copies(p):
            cp.start()

    o_ref[0] = jnp.zeros(o_ref.shape[1:], F32)
    qt = qt_ref[0]
    for p in range(nrows):
        t, h = divmod(p, nheads)
        hs = slice(h * HEAD_DIM, (h + 1) * HEAD_DIM)
        qb = jnp.broadcast_to(qt[hs, t:t + 1], (HEAD_DIM, knt_ref.shape[2]))
        l_own = jnp.sum(knt_ref[0, hs, :] * qb, axis=0, keepdims=True) + bown_ref[p:p + 1, :]
        logits = []
        for j in range(MOBA_TOPK):
            n = idx_ref[b, p * MOBA_TOPK + j]
            bias = jnp.where(n == nblk - 1, blast_ref[p:p + 1, :], cfar_ref[p:p + 1, :])
            logits.append(s_ref[0, n, p:p + 1, :] + bias)
        m = jnp.max(l_own, axis=1, keepdims=True)
        for lg in logits:
            m = jnp.maximum(m, jnp.max(lg, axis=1, keepdims=True))
        p_own = jnp.exp(l_own - m)
        denom = jnp.sum(p_own, axis=1, keepdims=True)
        acc = p_own * vnt_ref[0, hs, :]
        for cp in slab_copies(p):
            cp.wait()
        for j in range(MOBA_TOPK):
            pj = jnp.exp(logits[j] - m)
            denom = denom + jnp.sum(pj, axis=1, keepdims=True)
            for e in range(ppb):
                acc = acc + pj[:, e * page:(e + 1) * page] * vbuf[p, j * ppb + e]
        o_ref[0, hs, t:t + 1] = jnp.sum(acc, axis=1, keepdims=True) / denom


def _sample_attend(scores, idx, qt, knt, vnt, cache_vt, page_table, layer, blast, cfar, bown):
    nseq, nblk, nrows, _ = scores.shape
    wa, tq = qt.shape[1], qt.shape[2]
    nheads = wa // HEAD_DIM
    page = cache_vt.shape[-1]
    assert knt.shape[2] == page and bown.shape[1] == page
    seq = lambda shape: pl.BlockSpec((1,) + shape, lambda b, pt, ix: (b,) + (0,) * len(shape))
    whole = lambda shape: pl.BlockSpec(shape, lambda b, pt, ix: (0,) * len(shape))
    grid_spec = pltpu.PrefetchScalarGridSpec(
        num_scalar_prefetch=2,
        grid=(nseq,),
        in_specs=[seq((nblk, nrows, MOBA_BLOCK)), seq((wa, tq)), seq((wa, page)), seq((wa, page)),
                  whole(blast.shape), whole(cfar.shape), whole(bown.shape),
                  pl.BlockSpec(memory_space=pl.ANY)],
        out_specs=seq((wa, 128)),
        scratch_shapes=[pltpu.VMEM((nrows, MOBA_TOPK * (MOBA_BLOCK // page), HEAD_DIM, page), F32),
                        pltpu.SemaphoreType.DMA((nrows,))],
    )
    return pl.pallas_call(
        functools.partial(_sample_attend_kernel, nheads=nheads, nblk=nblk, page=page, tq=tq, layer=layer),
        grid_spec=grid_spec,
        out_shape=jax.ShapeDtypeStruct((nseq, wa, 128), F32),
        compiler_params=_cparams("arbitrary"),
        name="sample_attend",
    )(page_table, idx, scores, qt, knt, vnt, blast, cfar, bown, cache_vt)


def _gdn_kernel(q_ref, k_ref, v_ref, kt_ref, gbc_ref, gbr_ref, s0_ref, onw_ref, o_ref, sout_ref,
                s_s, lhs_s, add_s, *, nheads, nchunk, chunks_per_iter):
    tc = pl.program_id(1)
    c_len = GDN_CHUNK
    hd = HEAD_DIM

    @pl.when(tc == 0)
    def _():
        s_s[...] = s0_ref[0]

    row = lax.broadcasted_iota(jnp.int32, (c_len, c_len), 0)
    col = lax.broadcasted_iota(jnp.int32, (c_len, c_len), 1)
    tri = row >= col
    tri_s = row > col
    eye = row == col
    eye_f = jnp.where(eye, 1.0, 0.0)
    blk = lambda size: (row // size) == (col // size)

    def chunk_a(it, carry):
        chains = [(it * chunks_per_iter + cc, h) for cc in range(chunks_per_iter) for h in range(nheads)]
        st = []
        for c, h in chains:
            gbc = gbc_ref[0, pl.ds(c * c_len, c_len), :]
            gbr = gbr_ref[0, c]
            q = q_ref[0, h, pl.ds(c * c_len, c_len), :]
            k = k_ref[0, h, pl.ds(c * c_len, c_len), :]
            v = v_ref[0, h, pl.ds(c * c_len, c_len), :]
            bcol = gbc[:, h:h + 1]
            gcol = gbc[:, nheads + h:nheads + h + 1]
            grow = gbr[nheads + h:nheads + h + 1, :]
            glast = gcol[c_len - 1:c_len, :]
            st.append(dict(c=c, h=h, q=q, k=k, v=v, bcol=bcol, grow=grow, glast=glast,
                           decay=jnp.where(tri, jnp.exp(gcol - grow), 0.0), egc=jnp.exp(gcol)))
        for d in st:
            d["kq"] = _mm1(jnp.concatenate([d["k"], d["q"]], axis=0), d["k"], _NT)
        for d in st:
            kk, qk = d["kq"][:c_len], d["kq"][c_len:]
            d["n"] = jnp.where(tri_s, d["bcol"] * kk * d["decay"], 0.0)
            d["qkd"] = jnp.where(tri, qk * d["decay"], 0.0)
            d["x"] = jnp.concatenate([d["v"] * d["bcol"], d["k"] * (d["bcol"] * d["egc"])], axis=1)
            d["a"] = -jnp.where(blk(INV_BASE), d["n"], 0.0)
        for d in st:
            d["p2"] = _mm2(d["a"], d["a"])
        for d in st:
            d["res"] = _mm2(jnp.concatenate([d["p2"], eye_f + d["a"]], axis=0), d["p2"])
        for d in st:
            d["p4"] = d["res"][:c_len]
            d["t"] = eye_f + d["a"] + d["res"][c_len:]
        for d in st:
            d["res"] = _mm2(d["t"], d["p4"])
        for d in st:
            d["t"] = d["t"] + d["res"]
        size = 2 * INV_BASE
        while size <= c_len:
            off = blk(size) & jnp.logical_not(blk(size // 2))
            for d in st:
                d["y"] = _mm1(jnp.where(off, d["n"], 0.0), d["t"])
            for d in st:
                d["res"] = _mm1(d["t"], d["y"])
            for d in st:
                d["t"] = d["t"] - d["res"]
            size *= 2
        for d in st:
            d["res"] = _mm2(d["t"], d["x"])
        for d in st:
            d["x"] = d["res"]
        for d in st:
            kdt = kt_ref[0, d["h"], d["c"]] * jnp.exp(d["glast"] - d["grow"])
            d["res"] = _mm2(jnp.concatenate([d["qkd"], kdt], axis=0), d["x"])
        for d in st:
            top = jnp.concatenate([d["q"] * d["egc"], jnp.where(eye, jnp.exp(d["glast"]), 0.0)], axis=0)
            lhs_s[d["c"], d["h"]] = top - d["res"][:, hd:]
            add_s[d["c"], d["h"]] = d["res"][:, :hd]
        return carry

    lax.fori_loop(0, nchunk // chunks_per_iter, chunk_a, 0)

    onw = onw_ref[...]

    def chunk_b(c, carry):
        res = [_mm3(lhs_s[c, h], s_s[h]) + add_s[c, h] for h in range(nheads)]
        for h in range(nheads):
            o = res[h][:c_len]
            s_s[h] = res[h][c_len:]
            o = o * lax.rsqrt(jnp.mean(o * o, axis=-1, keepdims=True) + RMS_EPS) * onw
            o_ref[0, h, pl.ds(c * c_len, c_len), :] = o
        return carry

    lax.fori_loop(0, nchunk, chunk_b, 0)
    sout_ref[0] = s_s[...]


def _gdn(qh, kh, vh, kt, gbc, gbr, s0, onw, nchunk):
    b, nheads, t, hd = qh.shape
    tb = nchunk * GDN_CHUNK
    nt = t // tb
    head = pl.BlockSpec((1, nheads, tb, hd), lambda bi, ti: (bi, 0, ti, 0))
    state = pl.BlockSpec((1, nheads, hd, hd), lambda bi, ti: (bi, 0, 0, 0))
    return pl.pallas_call(
        functools.partial(_gdn_kernel, nheads=nheads, nchunk=nchunk, chunks_per_iter=min(4, nchunk)),
        grid=(b, nt),
        in_specs=[head, head, head,
                  pl.BlockSpec((1, nheads, nchunk, hd, GDN_CHUNK), lambda bi, ti: (bi, 0, ti, 0, 0)),
                  pl.BlockSpec((1, tb, 8), lambda bi, ti: (bi, ti, 0)),
                  pl.BlockSpec((1, nchunk, 8, GDN_CHUNK), lambda bi, ti: (bi, ti, 0, 0)),
                  state, _const_spec((1, hd))],
        out_specs=[head, state],
        out_shape=[jax.ShapeDtypeStruct((b, nheads, t, hd), F32),
                   jax.ShapeDtypeStruct((b, nheads, hd, hd), F32)],
        scratch_shapes=[
            pltpu.VMEM((nheads, hd, hd), F32),
            pltpu.VMEM((nchunk, nheads, GDN_CHUNK + hd, hd), F32),
            pltpu.VMEM((nchunk, nheads, GDN_CHUNK + hd, hd), F32),
        ],
        compiler_params=_cparams("parallel", "arbitrary"),
        name="deltanet",
    )(qh, kh, vh, kt, gbc, gbr, s0, onw)


def _out_kernel(x_ref, gate_ref, att_ref, za_ref, yb_ref, yc_ref, zc_ref, w_ref, o_ref):
    cat = jnp.concatenate([za_ref[...] * att_ref[...], yb_ref[...], yc_ref[...] * zc_ref[...]], axis=-1)
    y = jnp.dot(cat.astype(BF16), w_ref[...], preferred_element_type=F32)
    o_ref[...] = x_ref[...] + gate_ref[...] * y


def _out_proj(x2, gate2, att2, za2, yb2, yc2, zc2, wout, rows_per_gate):
    rows, d = x2.shape
    tr = min(ROW_TILE, rows)
    wa, wb = att2.shape[1], yb2.shape[1]
    row = lambda w: pl.BlockSpec((tr, w), lambda i: (i, 0))
    if rows_per_gate == 1:
        gate_spec = row(d)
    else:
        per = rows_per_gate // tr
        gate2 = gate2.reshape(gate2.shape[0], 1, d)
        gate_spec = pl.BlockSpec((None, 1, d), lambda i: (i // per, 0, 0))
    return pl.pallas_call(
        _out_kernel,
        grid=(rows // tr,),
        in_specs=[row(d), gate_spec, row(wa), row(wa), row(wb), row(wb), row(wb), _const_spec((d, d))],
        out_specs=row(d),
        out_shape=jax.ShapeDtypeStruct((rows, d), F32),
        compiler_params=_cparams("parallel"),
        name="out_proj",
    )(x2, gate2, att2, za2, yb2, yc2, zc2, wout)


def _block_diag_ones(n, blk, dtype):
    r = np.arange(n)
    return jnp.asarray((r[:, None] // blk) == (r[None, :] // blk), dtype)


def _layer_weights(l, norm_w, w_in, q_norm_w, k_norm_w, conv_b_w, conv_c_w, a_log, dt_bias, o_norm_w,
                   w_out, nseq, ts):
    d = w_in.shape[1]
    wa, wb = d // 2, d // 4
    nheads_a = wa // HEAD_DIM
    hc = wb // HEAD_DIM
    n_main = 4 * wa + 4 * wb + 4 * wb
    w = w_in[l]
    wcat = jnp.concatenate([w[:, :n_main], jnp.pad(w[:, n_main:], ((0, 0), (0, 128 - 2 * hc)))], axis=1)
    r = np.arange(ROW_TILE)
    cum_prompt = ((r[:, None] // GDN_CHUNK == r[None, :] // GDN_CHUNK) & (r[:, None] >= r[None, :]))
    rs = np.arange(nseq * ts)
    cum_sample = ((rs[:, None] % nseq == rs[None, :] % nseq) & (rs[:, None] >= rs[None, :]))
    return dict(
        norm_w=norm_w[l].reshape(1, d),
        wcat=wcat.astype(BF16),
        qw=jnp.tile(q_norm_w[l], nheads_a).reshape(1, wa),
        kw=jnp.tile(k_norm_w[l], nheads_a).reshape(1, wa),
        bd512=_block_diag_ones(wa, HEAD_DIM, BF16),
        bd256=_block_diag_ones(wb, HEAD_DIM, BF16),
        conv_b_w=conv_b_w[l], conv_c_w=conv_c_w[l],
        alog8=jnp.concatenate([jnp.zeros((hc,), F32), a_log[l]]).reshape(1, 2 * hc),
        dtb8=jnp.concatenate([jnp.zeros((hc,), F32), dt_bias[l]]).reshape(1, 2 * hc),
        cum_prompt=jnp.asarray(cum_prompt, F32),
        cum_sample=jnp.asarray(cum_sample, F32),
        onw=o_norm_w[l].reshape(1, HEAD_DIM),
        wout=w_out[l].astype(BF16),
    )


def _heads(a, nheads):
    b, t, _ = a.shape
    return a.reshape(b, t, nheads, HEAD_DIM).transpose(0, 2, 1, 3)


def _gdn_layout(qc, kc, vc, gb, hc):
    b, t, _ = qc.shape
    nch = t // GDN_CHUNK
    kt = kc.reshape(b, nch, GDN_CHUNK, hc, HEAD_DIM).transpose(0, 3, 1, 4, 2)
    gbr = gb.reshape(b, nch, GDN_CHUNK, 2 * hc).transpose(0, 1, 3, 2)
    return _heads(qc, hc), _heads(kc, hc), _heads(vc, hc), kt, gb, gbr


def kernel(x_prompt, x_sample, c_prompt, c_sample, cache_k, cache_v, page_table, state_conv_b,
           state_conv_c, state_delta, norm_w, ada_w, ada_b, w_in, q_norm_w, k_norm_w, rel_bias,
           conv_b_w, conv_c_w, a_log, dt_bias, o_norm_w, w_out):
    bp, tp, d = x_prompt.shape
    bs, ts, _ = x_sample.shape
    depth = ada_w.shape[0]
    wa, wb = d // 2, d // 4
    ha, hc = wa // HEAD_DIM, wb // HEAD_DIM
    page = cache_k.shape[2]
    past_len = page_table.shape[1] * page
    nblk_p = tp // MOBA_BLOCK
    nblk_s = past_len // MOBA_BLOCK
    assert tp % ROW_TILE == 0 and ROW_TILE == MOBA_BLOCK and past_len % MOBA_BLOCK == 0
    assert MOBA_BLOCK % page == 0 and page_table.shape[1] % SAMPLE_PAGES_PER_STEP == 0 and ts <= GDN_CHUNK
    assert SAMPLE_PAGES_PER_STEP % (MOBA_BLOCK // page) == 0 and page == 128

    mod = _modulation(jnp.concatenate([c_prompt, c_sample], axis=0), ada_w, ada_b)

    tab_h = rel_bias.T.astype(F32)
    kq = np.arange(MOBA_BLOCK)
    d_own = kq[None, :] - kq[:, None]
    dist_p = np.stack([np.broadcast_to(d_own + o, (ha, MOBA_BLOCK, MOBA_BLOCK)) for o in (0, MOBA_BLOCK)])
    bt = _bias_tiles(jnp.asarray(dist_p.reshape(-1, MOBA_BLOCK), jnp.int32),
                     jnp.repeat(jnp.tile(tab_h, (2, 1)), MOBA_BLOCK, axis=0))
    bt = bt.reshape(2, ha, MOBA_BLOCK, MOBA_BLOCK)
    cfar_p = bt[1, :, 0, MOBA_BLOCK - 1]

    tq = np.repeat(np.arange(ts), ha)
    pos_q = past_len + tq
    d_last = pos_q[:, None] - ((nblk_s - 1) * MOBA_BLOCK + kq[None, :])
    own_w = 128
    tk = np.arange(own_w)
    d_own_s = np.where(tk[None, :] < ts, tq[:, None] - tk[None, :], -1)
    d_far = np.broadcast_to(pos_q[:, None] - (nblk_s - 2) * MOBA_BLOCK - (MOBA_BLOCK - 1), (ts * ha, 128))
    tab_s = jnp.tile(tab_h, (ts, 1))
    blast = _bias_tiles(jnp.asarray(d_last, jnp.int32), tab_s)
    bown = _bias_tiles(jnp.asarray(d_own_s, jnp.int32), tab_s)
    cfar_s = _bias_tiles(jnp.asarray(d_far, jnp.int32), tab_s)[:, :1]

    cache_kt = cache_k.transpose(0, 1, 3, 4, 2)
    cache_vt = cache_v.transpose(0, 1, 3, 4, 2)

    hp = x_prompt
    hs_tm = x_sample.transpose(1, 0, 2).reshape(ts * bs, d)
    outs = [[] for _ in range(10)]
    for l in range(depth):
        lw = _layer_weights(l, norm_w, w_in, q_norm_w, k_norm_w, conv_b_w, conv_c_w, a_log, dt_bias,
                            o_norm_w, w_out, bs, ts)
        mod_p, mod_s = mod[l, :bp], mod[l, bp:]

        (q, k, v, kmean, za, yb, cbs, qc, kc, vc, zc, gb, ccs) = _proj_prompt(hp, mod_p, lw)
        kb = k.astype(BF16).reshape(bp, nblk_p, MOBA_BLOCK, wa)
        vt = v.astype(BF16).reshape(bp, nblk_p, MOBA_BLOCK, wa).transpose(0, 1, 3, 2)
        att = _attn_prompt(q, kb, vt, kmean.reshape(bp, nblk_p, wa), bt, cfar_p)
        nchunk_p = 8
        yc, s_new = _gdn(*_gdn_layout(qc, kc, vc, gb, hc), jnp.zeros((bp, hc, HEAD_DIM, HEAD_DIM), F32),
                         lw["onw"], nchunk_p)
        yc = yc.transpose(0, 2, 1, 3).reshape(bp * tp, wb)
        hp = _out_proj(hp.reshape(bp * tp, d), mod_p[:, 2 * d:], att.reshape(bp * tp, wa),
                       za.reshape(bp * tp, wa), yb.reshape(bp * tp, wb), yc, zc.reshape(bp * tp, wb),
                       lw["wout"], tp).reshape(bp, tp, d)
        outs[0].append(k.reshape(bp, tp, ha, HEAD_DIM)); outs[1].append(v.reshape(bp, tp, ha, HEAD_DIM))
        outs[4].append(cbs); outs[6].append(ccs); outs[8].append(s_new)

        rep = lambda a: jnp.tile(a, (ts, 1))
        hist_b = state_conv_b[l].transpose(1, 0, 2).reshape(-1, wb)
        hist_c = state_conv_c[l].transpose(1, 0, 2).reshape(-1, 3 * wb)
        (q, k, v, za, yb, cbs, qc, kc, vc, zc, gb, ccs) = _proj_sample(
            hs_tm, rep(mod_s[:, :d]), rep(mod_s[:, d:2 * d]), lw, hist_b, hist_c, bs)
        seq_major = lambda a: a.reshape(ts, bs, -1).transpose(1, 0, 2)
        qt = seq_major(q).transpose(0, 2, 1)
        new_t = lambda a: jnp.pad(seq_major(a), ((0, 0), (0, own_w - ts), (0, 0))).transpose(0, 2, 1)
        scores, idx = _sample_scores(qt, cache_kt, page_table, l)
        att_t = _sample_attend(scores, idx[:, :, :MOBA_TOPK].reshape(bs, -1), qt, new_t(k), new_t(v),
                               cache_vt, page_table, l, blast, cfar_s, bown)
        att = att_t[:, :, :ts].transpose(2, 0, 1).reshape(ts * bs, wa)
        pad_t = lambda a: jnp.pad(seq_major(a), ((0, 0), (0, GDN_CHUNK - ts), (0, 0)))
        gbs = seq_major(gb)
        gb_pad = jnp.concatenate(
            [pad_t(gb[:, :hc]), jnp.pad(gbs[:, :, hc:], ((0, 0), (0, GDN_CHUNK - ts), (0, 0)), mode="edge")],
            axis=-1)
        yc, s_new = _gdn(*_gdn_layout(pad_t(qc), pad_t(kc), pad_t(vc), gb_pad, hc), state_delta[l],
                         lw["onw"], 1)
        yc = yc[:, :, :ts, :].transpose(2, 0, 1, 3).reshape(ts * bs, wb)
        hs_tm = _out_proj(hs_tm, rep(mod_s[:, 2 * d:]), att, za, yb, yc, zc, lw["wout"], 1)
        outs[2].append(seq_major(k).reshape(bs, ts, ha, HEAD_DIM))
        outs[3].append(seq_major(v).reshape(bs, ts, ha, HEAD_DIM))
        outs[5].append(cbs.reshape(-1, bs, wb).transpose(1, 0, 2))
        outs[7].append(ccs.reshape(-1, bs, 3 * wb).transpose(1, 0, 2))
        outs[9].append(s_new)

    y_sample = hs_tm.reshape(ts, bs, d).transpose(1, 0, 2)
    return (hp, y_sample) + tuple(jnp.stack(o) for o in outs)
```

```python
import functools
import math

import jax
import jax.numpy as jnp
import numpy as np
from jax import lax
from jax.experimental import pallas as pl
from jax.experimental.pallas import tpu as pltpu

F32 = jnp.float32
BF16 = jnp.bfloat16
HIGHEST = lax.Precision.HIGHEST

HEAD_DIM = 64
MOBA_BLOCK = 256
MOBA_TOPK = 3
GDN_CHUNK = 64
QK_AHEAD = 4
LOG2E = math.log2(math.e)
INV_BASE = 8
REL_BUCKETS = 32
REL_MAX_DIST = 128
RMS_EPS = 1e-6
ROW_TILE = 256
VMEM_LIMIT = 56 * 1024 * 1024
NEG_INF = float("-inf")

_NT = (((1,), (1,)), ((), ()))


def _cparams(*sem):
    return pltpu.CompilerParams(dimension_semantics=sem, vmem_limit_bytes=VMEM_LIMIT)


def _const_spec(shape):
    zeros = (0,) * len(shape)
    return pl.BlockSpec(shape, lambda *_: zeros)


def _silu(x):
    return x * jax.nn.sigmoid(x)


def _softplus(x):
    return jnp.maximum(x, 0.0) + jnp.log(1.0 + jnp.exp(-jnp.abs(x)))


def _split(a):
    hi = a.astype(BF16)
    lo = (a - hi.astype(F32)).astype(BF16)
    return hi, lo


def _dot(a, b, dims=None):
    if dims is None:
        return jnp.dot(a, b, preferred_element_type=F32)
    return lax.dot_general(a, b, dims, preferred_element_type=F32)


def _mm1(a, b, dims=None):
    return _dot(a.astype(BF16), b.astype(BF16), dims)


def _mm2(a, b, dims=None):
    ah = a.astype(BF16)
    bh, bl = _split(b)
    return _dot(ah, bh, dims) + _dot(ah, bl, dims)


def _mm3(a, b, dims=None):
    ah, al = _split(a)
    bh, bl = _split(b)
    return _dot(ah, bh, dims) + _dot(al, bh, dims) + _dot(ah, bl, dims)


def _group_sum(x, ones_bd):
    hi, lo = _split(x)
    return (jnp.dot(hi, ones_bd, preferred_element_type=F32)
            + jnp.dot(lo, ones_bd, preferred_element_type=F32))


def _mod_kernel(c_ref, w_ref, b_ref, o_ref):
    o_ref[0] = jnp.dot(c_ref[...], w_ref[0], precision=HIGHEST, preferred_element_type=F32) + b_ref[0]


def _modulation(c_all, ada_w, ada_b):
    depth, d, n3 = ada_w.shape
    rows = c_all.shape[0]
    tn = 512
    return pl.pallas_call(
        _mod_kernel,
        grid=(depth, n3 // tn),
        in_specs=[
            pl.BlockSpec((rows, d), lambda l, n: (0, 0)),
            pl.BlockSpec((1, d, tn), lambda l, n: (l, 0, n)),
            pl.BlockSpec((1, 1, tn), lambda l, n: (l, 0, n)),
        ],
        out_specs=pl.BlockSpec((1, rows, tn), lambda l, n: (l, 0, n)),
        out_shape=jax.ShapeDtypeStruct((depth, rows, n3), F32),
        compiler_params=_cparams("parallel", "parallel"),
        name="modulation",
    )(c_all, ada_w, ada_b.reshape(depth, 1, n3))


def _bias_kernel(dist_ref, tab_ref, o_ref):
    dist = dist_ref[...]
    n = jnp.maximum(dist, 0)
    max_exact = REL_BUCKETS // 2
    nf = jnp.maximum(n, max_exact).astype(F32)
    large = max_exact + (jnp.log(nf / max_exact) / math.log(REL_MAX_DIST / max_exact)
                         * (REL_BUCKETS - max_exact)).astype(jnp.int32)
    bucket = jnp.where(n < max_exact, n, jnp.minimum(large, REL_BUCKETS - 1))
    tab = tab_ref[...]
    acc = jnp.zeros(dist.shape, F32)
    for b in range(REL_BUCKETS):
        acc = jnp.where(bucket == b, tab[:, b:b + 1], acc)
    o_ref[...] = jnp.where(dist >= 0, acc, NEG_INF)


def _bias_tiles(dist, tab):
    rows, cols = dist.shape
    tr = min(rows, 256)
    return pl.pallas_call(
        _bias_kernel,
        grid=(rows // tr,),
        in_specs=[pl.BlockSpec((tr, cols), lambda i: (i, 0)),
                  pl.BlockSpec((tr, REL_BUCKETS), lambda i: (i, 0))],
        out_specs=pl.BlockSpec((tr, cols), lambda i: (i, 0)),
        out_shape=jax.ShapeDtypeStruct((rows, cols), F32),
        compiler_params=_cparams("parallel"),
        name="bias_tiles",
    )(dist, tab)


def _proj_common(x, shift, scale, normw, w_ref, qw, kw, bd512_ref, bd256_ref, alog, dtb, cum_ref):
    ms = jnp.mean(x * x, axis=-1, keepdims=True)
    hn = (x * lax.rsqrt(ms + RMS_EPS) * normw) * (1.0 + scale) + shift
    z = jnp.dot(hn.astype(BF16), w_ref[...], preferred_element_type=F32)
    wa = qw.shape[-1]
    wb = wa // 2
    o = 0
    q = z[:, o:o + wa]; o += wa
    k = z[:, o:o + wa]; o += wa
    v = z[:, o:o + wa]; o += wa
    za = z[:, o:o + wa]; o += wa
    hb = z[:, o:o + wb]; o += wb
    bg = z[:, o:o + wb]; o += wb
    cg = z[:, o:o + wb]; o += wb
    zb = z[:, o:o + wb]; o += wb
    qkv = z[:, o:o + 3 * wb]; o += 3 * wb
    zc = z[:, o:o + wb]; o += wb
    z8 = z[:, o:o + 8]

    bd512 = bd512_ref[...]
    qn = q * lax.rsqrt(_group_sum(q * q, bd512) * (1.0 / HEAD_DIM) + RMS_EPS) * qw
    kn = k * lax.rsqrt(_group_sum(k * k, bd512) * (1.0 / HEAD_DIM) + RMS_EPS) * kw

    lane8 = lax.broadcasted_iota(jnp.int32, z8.shape, 1)
    beta = jax.nn.sigmoid(z8)
    g = -jnp.exp(alog) * _softplus(z8 + dtb)
    bg8 = jnp.where(lane8 < 4, beta, g)
    cum = jnp.dot(cum_ref[...], bg8, precision=HIGHEST, preferred_element_type=F32)
    gb = jnp.where(lane8 < 4, bg8, cum)
    return dict(q=qn * (HEAD_DIM ** -0.5), k=kn, v=v, za=_silu(za), u=cg * hb, zbg=_silu(zb) * bg,
                qkv=qkv, zc=_silu(zc), gb=gb)


def _gdn_qkv(y, bd256_ref):
    wb = y.shape[-1] // 3
    y = _silu(y)
    bd256 = bd256_ref[...]
    qc, kc, vc = y[:, :wb], y[:, wb:2 * wb], y[:, 2 * wb:]
    qc = qc * lax.rsqrt(_group_sum(qc * qc, bd256) + RMS_EPS) * (HEAD_DIM ** -0.5)
    kc = kc * lax.rsqrt(_group_sum(kc * kc, bd256) + RMS_EPS)
    return qc, kc, vc


def _proj_prompt_kernel(x_ref, mod_ref, normw_ref, w_ref, qw_ref, kw_ref, bd512_ref, bd256_ref,
                        cbw_ref, ccw_ref, alog_ref, dtb_ref, cum_ref,
                        q_ref, k_ref, v_ref, kb_ref, vt_ref, km_ref, za_ref, yb_ref, cbs_ref,
                        qc_ref, kc_ref, vc_ref, kt_ref, zc_ref, gb_ref, gbr_ref, ccs_ref,
                        carry_b, carry_c):
    ti = pl.program_id(1)
    d = x_ref.shape[-1]
    rows = x_ref.shape[1]

    @pl.when(ti == 0)
    def _():
        carry_b[...] = jnp.zeros_like(carry_b)
        carry_c[...] = jnp.zeros_like(carry_c)

    mod = mod_ref[0]
    p = _proj_common(x_ref[0], mod[:, :d], mod[:, d:2 * d], normw_ref[...], w_ref, qw_ref[...],
                     kw_ref[...], bd512_ref, bd256_ref, alog_ref[...], dtb_ref[...], cum_ref)
    q_ref[0] = p["q"]
    k_ref[0] = p["k"]
    v_ref[0] = p["v"]
    km_ref[0, 0] = jnp.mean(p["k"], axis=0, keepdims=True)
    za_ref[0] = p["za"]
    zc_ref[0] = p["zc"]
    gb = p["gb"]
    gb_ref[0] = gb
    kb_ref[0, 0] = p["k"].astype(BF16)
    vt_ref[0, 0] = p["v"].T.astype(BF16)
    ncol = gb.shape[1]
    eye8 = jnp.where(lax.broadcasted_iota(jnp.int32, (ncol, ncol), 0)
                     == lax.broadcasted_iota(jnp.int32, (ncol, ncol), 1), 1.0, 0.0)
    gbr = lax.dot_general(eye8, gb, _NT, precision=HIGHEST, preferred_element_type=F32)
    for cc in range(rows // GDN_CHUNK):
        gbr_ref[0, cc] = gbr[:, cc * GDN_CHUNK:(cc + 1) * GDN_CHUNK]

    u = p["u"]
    ext = jnp.concatenate([carry_b[...], u], axis=0)
    cbw = cbw_ref[...]
    nb = cbw.shape[0]
    yb = u * cbw[nb - 1:nb]
    for j in range(nb - 1):
        s = nb - 1 - j
        yb = yb + ext[8 - s:8 - s + rows] * cbw[j:j + 1]
    yb_ref[0] = p["zbg"] * yb
    carry_b[...] = u[rows - 8:]
    cbs_ref[0] = u[rows - (nb - 1):]

    qkv = p["qkv"]
    extc = jnp.concatenate([carry_c[...], qkv], axis=0)
    ccw = ccw_ref[...]
    nc = ccw.shape[0]
    yc = qkv * ccw[nc - 1:nc]
    for j in range(nc - 1):
        s = nc - 1 - j
        yc = yc + extc[8 - s:8 - s + rows] * ccw[j:j + 1]
    carry_c[...] = qkv[rows - 8:]
    ccs_ref[0] = qkv[rows - (nc - 1):]
    qc, kc, vc = _gdn_qkv(yc, bd256_ref)
    kct = kc.T
    for h in range(qc.shape[1] // HEAD_DIM):
        hs = slice(h * HEAD_DIM, (h + 1) * HEAD_DIM)
        qc_ref[0, h] = qc[:, hs]
        kc_ref[0, h] = kc[:, hs]
        vc_ref[0, h] = vc[:, hs]
        for cc in range(rows // GDN_CHUNK):
            kt_ref[0, h, cc] = kct[hs, cc * GDN_CHUNK:(cc + 1) * GDN_CHUNK]


def _proj_sample_kernel(x_ref, shift_ref, scale_ref, normw_ref, w_ref, qw_ref, kw_ref, bd512_ref,
                        bd256_ref, cbw_ref, ccw_ref, alog_ref, dtb_ref, cum_ref, hb_ref, hc_ref,
                        q_ref, k_ref, v_ref, za_ref, yb_ref, cbs_ref,
                        qc_ref, kc_ref, vc_ref, zc_ref, gb_ref, ccs_ref, *, nseq):
    rows = x_ref.shape[0]
    p = _proj_common(x_ref[...], shift_ref[...], scale_ref[...], normw_ref[...], w_ref, qw_ref[...],
                     kw_ref[...], bd512_ref, bd256_ref, alog_ref[...], dtb_ref[...], cum_ref)
    q_ref[...] = p["q"]
    k_ref[...] = p["k"]
    v_ref[...] = p["v"]
    za_ref[...] = p["za"]
    zc_ref[...] = p["zc"]
    gb_ref[...] = p["gb"]

    cbw = cbw_ref[...]
    nb = cbw.shape[0]
    ext = jnp.concatenate([hb_ref[...], p["u"]], axis=0)
    yb = ext[0:rows] * cbw[0:1]
    for j in range(1, nb):
        yb = yb + ext[j * nseq:j * nseq + rows] * cbw[j:j + 1]
    yb_ref[...] = p["zbg"] * yb
    cbs_ref[...] = ext[rows:]

    ccw = ccw_ref[...]
    nc = ccw.shape[0]
    extc = jnp.concatenate([hc_ref[...], p["qkv"]], axis=0)
    yc = extc[0:rows] * ccw[0:1]
    for j in range(1, nc):
        yc = yc + extc[j * nseq:j * nseq + rows] * ccw[j:j + 1]
    ccs_ref[...] = extc[rows:]
    qc, kc, vc = _gdn_qkv(yc, bd256_ref)
    qc_ref[...] = qc
    kc_ref[...] = kc
    vc_ref[...] = vc


def _proj_prompt(x, mod, lw):
    b, t, d = x.shape
    wcat = lw["wcat"]
    ncat = wcat.shape[1]
    wa, wb = d // 2, d // 4
    nt = t // ROW_TILE
    row = lambda w: pl.BlockSpec((1, ROW_TILE, w), lambda bi, ti: (bi, ti, 0))
    per_b = lambda r, w: pl.BlockSpec((1, r, w), lambda bi, ti: (bi, 0, 0))
    nb, nc = lw["conv_b_w"].shape[0], lw["conv_c_w"].shape[0]
    hc = wb // HEAD_DIM
    cpt = ROW_TILE // GDN_CHUNK
    nch = t // GDN_CHUNK
    out_shape = [
        jax.ShapeDtypeStruct((b, t, wa), F32),
        jax.ShapeDtypeStruct((b, t, wa), F32),
        jax.ShapeDtypeStruct((b, t, wa), F32),
        jax.ShapeDtypeStruct((b, nt, ROW_TILE, wa), BF16),
        jax.ShapeDtypeStruct((b, nt, wa, ROW_TILE), BF16),
        jax.ShapeDtypeStruct((b, nt, 1, wa), F32),
        jax.ShapeDtypeStruct((b, t, wa), F32),
        jax.ShapeDtypeStruct((b, t, wb), F32),
        jax.ShapeDtypeStruct((b, nb - 1, wb), F32),
        jax.ShapeDtypeStruct((b, hc, t, HEAD_DIM), F32),
        jax.ShapeDtypeStruct((b, hc, t, HEAD_DIM), F32),
        jax.ShapeDtypeStruct((b, hc, t, HEAD_DIM), F32),
        jax.ShapeDtypeStruct((b, hc, nch, HEAD_DIM, GDN_CHUNK), F32),
        jax.ShapeDtypeStruct((b, t, wb), F32),
        jax.ShapeDtypeStruct((b, t, 8), F32),
        jax.ShapeDtypeStruct((b, nch, 8, GDN_CHUNK), F32),
        jax.ShapeDtypeStruct((b, nc - 1, 3 * wb), F32),
    ]
    blk4 = lambda r, c: pl.BlockSpec((1, 1, r, c), lambda bi, ti: (bi, ti, 0, 0))
    heads = pl.BlockSpec((1, hc, ROW_TILE, HEAD_DIM), lambda bi, ti: (bi, 0, ti, 0))
    out_specs = [
        row(wa), row(wa), row(wa), blk4(ROW_TILE, wa), blk4(wa, ROW_TILE), blk4(1, wa),
        row(wa), row(wb), per_b(nb - 1, wb),
        heads, heads, heads,
        pl.BlockSpec((1, hc, cpt, HEAD_DIM, GDN_CHUNK), lambda bi, ti: (bi, 0, ti, 0, 0)),
        row(wb), row(8),
        pl.BlockSpec((1, cpt, 8, GDN_CHUNK), lambda bi, ti: (bi, ti, 0, 0)),
        per_b(nc - 1, 3 * wb),
    ]
    in_specs = [
        row(d),
        pl.BlockSpec((1, 1, 3 * d), lambda bi, ti: (bi, 0, 0)),
        _const_spec((1, d)),
        _const_spec((d, ncat)),
        _const_spec((1, wa)), _const_spec((1, wa)),
        _const_spec((wa, wa)), _const_spec((wb, wb)),
        _const_spec((nb, wb)), _const_spec((nc, 3 * wb)),
        _const_spec((1, 8)), _const_spec((1, 8)),
        _const_spec((ROW_TILE, ROW_TILE)),
    ]
    return pl.pallas_call(
        _proj_prompt_kernel,
        grid=(b, nt),
        in_specs=in_specs,
        out_specs=out_specs,
        out_shape=out_shape,
        scratch_shapes=[pltpu.VMEM((8, wb), F32), pltpu.VMEM((8, 3 * wb), F32)],
        compiler_params=_cparams("parallel", "arbitrary"),
        name="proj_prompt",
    )(x, mod.reshape(b, 1, 3 * d), lw["norm_w"], wcat, lw["qw"], lw["kw"], lw["bd512"], lw["bd256"],
      lw["conv_b_w"], lw["conv_c_w"], lw["alog8"], lw["dtb8"], lw["cum_prompt"])


def _proj_sample(x_tm, shift, scale, lw, hist_b, hist_c, nseq):
    rows, d = x_tm.shape
    wa, wb = d // 2, d // 4
    nb, nc = lw["conv_b_w"].shape[0], lw["conv_c_w"].shape[0]
    out_shape = [
        jax.ShapeDtypeStruct((rows, wa), F32), jax.ShapeDtypeStruct((rows, wa), F32),
        jax.ShapeDtypeStruct((rows, wa), F32), jax.ShapeDtypeStruct((rows, wa), F32),
        jax.ShapeDtypeStruct((rows, wb), F32),
        jax.ShapeDtypeStruct(((nb - 1) * nseq, wb), F32),
        jax.ShapeDtypeStruct((rows, wb), F32), jax.ShapeDtypeStruct((rows, wb), F32),
        jax.ShapeDtypeStruct((rows, wb), F32), jax.ShapeDtypeStruct((rows, wb), F32),
        jax.ShapeDtypeStruct((rows, 8), F32),
        jax.ShapeDtypeStruct(((nc - 1) * nseq, 3 * wb), F32),
    ]
    return pl.pallas_call(
        functools.partial(_proj_sample_kernel, nseq=nseq),
        out_shape=out_shape,
        compiler_params=pltpu.CompilerParams(vmem_limit_bytes=VMEM_LIMIT),
        name="proj_sample",
    )(x_tm, shift, scale, lw["norm_w"], lw["wcat"], lw["qw"], lw["kw"], lw["bd512"], lw["bd256"],
      lw["conv_b_w"], lw["conv_c_w"], lw["alog8"], lw["dtb8"], lw["cum_sample"], hist_b, hist_c)


def _attn_prompt_kernel(cfar_ref, q_ref, kb_ref, vt_ref, km_ref, bt_ref, o_ref,
                        sel_s, acc_s, *, nheads, nblk):
    i = pl.program_id(1)
    blk = q_ref.shape[1]
    qt = q_ref[0].T * LOG2E
    qtb = qt.astype(BF16)
    lane = lax.broadcasted_iota(jnp.int32, (1, 2 * HEAD_DIM), 1)
    sub = lax.broadcasted_iota(jnp.int32, (2 * HEAD_DIM, 1), 0)
    lane_half = [lane < HEAD_DIM, lane >= HEAD_DIM]
    sub_half = [sub < HEAD_DIM, sub >= HEAD_DIM]

    km = km_ref[0]
    pieces = []
    for h in range(nheads):
        p, par = divmod(h, 2)
        kmp = jnp.where(lane_half[par], km[:, p * 128:(p + 1) * 128], 0.0)
        pieces.append(jnp.dot(kmp, qt[p * 128:(p + 1) * 128, :], precision=HIGHEST,
                              preferred_element_type=F32))
    gate = jnp.concatenate(pieces, axis=0)
    nrow = nheads * nblk
    n = lax.broadcasted_iota(jnp.int32, (nrow, blk), 0) % nblk
    past = n < i
    gate = jnp.where(past, gate, NEG_INF)
    rank = jnp.zeros((nrow, blk), F32)
    for s in range(1, nblk):
        below = pltpu.roll(gate, s, 0)
        rank = rank + jnp.where((n >= s) & (below >= gate), 1.0, 0.0)
        above = pltpu.roll(gate, nrow - s, 0)
        rank = rank + jnp.where((n + s < nblk) & (above > gate), 1.0, 0.0)
    sel_s[...] = jnp.where(past & (rank < MOBA_TOPK), 1.0, 0.0)

    qm = []
    for h in range(nheads):
        p, par = divmod(h, 2)
        qm.append(jnp.where(sub_half[par], qtb[p * 128:(p + 1) * 128, :], jnp.zeros((), BF16)))

    def scores(kblk, h):
        p = h // 2
        return jnp.dot(kblk[:, p * 128:(p + 1) * 128], qm[h], preferred_element_type=F32)

    def pv_dot(vblk, h, pr):
        return jnp.dot(vblk[h * HEAD_DIM:(h + 1) * HEAD_DIM, :], pr.astype(BF16), preferred_element_type=F32)


    def block(j, kind, m_all, l_all):
        kblk = kb_ref[0, j]
        vblk = vt_ref[0, j]
        sts = [scores(kblk, h) for h in range(min(QK_AHEAD, nheads))]
        m_rows, l_rows = [], []
        pend = None

        def finish(h, alpha, on, pv):
            rs = slice(h * HEAD_DIM, (h + 1) * HEAD_DIM)
            if kind == "own":
                acc_s[rs, :] = pv
            else:
                acc_s[rs, :] = alpha * acc_s[rs, :] + jnp.where(on, pv, 0.0)

        for h in range(nheads):
            st = sts[h]
            if h + QK_AHEAD < nheads:
                sts.append(scores(kblk, h + QK_AHEAD))
            if kind == "own":
                st = bt_ref[0, h] * LOG2E + st
                m_new = jnp.max(st, axis=0, keepdims=True)
                pr = jnp.exp2(st - m_new)
                alpha = on = None
                l_new = jnp.sum(pr, axis=0, keepdims=True)
            else:
                if kind == "near":
                    st = bt_ref[1, h] * LOG2E + st
                    shift = 0.0
                else:
                    shift = cfar_ref[h] * LOG2E
                on = sel_s[pl.ds(h * nblk + j, 1), :] > 0.0
                m_old = m_all[h:h + 1, :]
                m_blk = jnp.max(st, axis=0, keepdims=True) + shift
                m_new = jnp.maximum(m_old, jnp.where(on, m_blk, NEG_INF))
                alpha = jnp.exp2(m_old - m_new)
                pr = jnp.exp2(st - (m_new - shift))
                l_new = alpha * l_all[h:h + 1, :] + jnp.where(on, jnp.sum(pr, axis=0, keepdims=True), 0.0)
            pv = pv_dot(vblk, h, pr)
            m_rows.append(m_new)
            l_rows.append(l_new)
            if pend is not None:
                finish(*pend)
            pend = (h, alpha, on, pv)
        finish(*pend)
        return jnp.concatenate(m_rows, axis=0), jnp.concatenate(l_rows, axis=0)

    m_all, l_all = block(i, "own", None, None)
    update = lambda j, near, m, l: block(j, "near" if near else "far", m, l)

    m_all, l_all = lax.cond(i >= 1, lambda m, l: update(i - 1, True, m, l), lambda m, l: (m, l), m_all, l_all)
    m_all, l_all = lax.fori_loop(0, i - 1, lambda j, c: update(j, False, *c), (m_all, l_all))

    for h in range(nheads):
        rs = slice(h * HEAD_DIM, (h + 1) * HEAD_DIM)
        acc_s[rs, :] = acc_s[rs, :] / l_all[h:h + 1, :]
    o_ref[0] = acc_s[...].T


def _attn_prompt(q, kb, vt, kmean, bt, cfar):
    b, t, wa = q.shape
    nheads = wa // HEAD_DIM
    nblk = t // MOBA_BLOCK
    grid_spec = pltpu.PrefetchScalarGridSpec(
        num_scalar_prefetch=0,
        grid=(b, nblk),
        in_specs=[
            pl.BlockSpec(memory_space=pltpu.SMEM),
            pl.BlockSpec((1, MOBA_BLOCK, wa), lambda bi, i: (bi, i, 0)),
            pl.BlockSpec((1, nblk, MOBA_BLOCK, wa), lambda bi, i: (bi, 0, 0, 0)),
            pl.BlockSpec((1, nblk, wa, MOBA_BLOCK), lambda bi, i: (bi, 0, 0, 0)),
            pl.BlockSpec((1, nblk, wa), lambda bi, i: (bi, 0, 0)),
            pl.BlockSpec((2, nheads, MOBA_BLOCK, MOBA_BLOCK), lambda bi, i: (0, 0, 0, 0)),
        ],
        out_specs=pl.BlockSpec((1, MOBA_BLOCK, wa), lambda bi, i: (bi, i, 0)),
        scratch_shapes=[
            pltpu.VMEM((nheads * nblk, MOBA_BLOCK), F32),
            pltpu.VMEM((wa, MOBA_BLOCK), F32),
        ],
    )
    return pl.pallas_call(
        functools.partial(_attn_prompt_kernel, nheads=nheads, nblk=nblk),
        grid_spec=grid_spec,
        out_shape=jax.ShapeDtypeStruct((b, t, wa), F32),
        compiler_params=_cparams("parallel", "arbitrary"),
        name="attn_prompt",
    )(cfar, q, kb, vt, kmean, bt)


SAMPLE_PAGES_PER_STEP = 16


def _sample_scores_kernel(pt_ref, qt_ref, *rest, nheads, nblk, page, tq):
    npg = SAMPLE_PAGES_PER_STEP
    k_refs = rest[:npg]
    s_ref, idx_ref, qb_s, gate_s = rest[npg:]
    s = pl.program_id(1)
    nrows = tq * nheads
    ppb = MOBA_BLOCK // page

    @pl.when(s == 0)
    def _():
        qt = qt_ref[0]
        for t in range(tq):
            for h in range(nheads):
                qb_s[t * nheads + h] = jnp.broadcast_to(qt[h * HEAD_DIM:(h + 1) * HEAD_DIM, t:t + 1],
                                                        (HEAD_DIM, page))
        gate_s[...] = jnp.full(gate_s.shape, NEG_INF, F32)

    lane = lax.broadcasted_iota(jnp.int32, gate_s.shape, 1)
    bsum = None
    for e in range(npg):
        groups = []
        for t in range(tq):
            rows = [jnp.sum(k_refs[e][h] * qb_s[t * nheads + h], axis=0, keepdims=True) for h in range(nheads)]
            groups.append(jnp.concatenate(rows, axis=0))
        sp = jnp.concatenate(groups, axis=0)
        s_ref[0, e // ppb, :, (e % ppb) * page:(e % ppb + 1) * page] = sp
        rs = jnp.sum(sp, axis=1, keepdims=True)
        bsum = rs if e % ppb == 0 else bsum + rs
        if e % ppb == ppb - 1:
            n = s * (npg // ppb) + e // ppb
            gate_s[...] = jnp.where(lane == n, bsum * (1.0 / MOBA_BLOCK), gate_s[...])

    @pl.when(s == pl.num_programs(1) - 1)
    def _():
        gate = gate_s[...]
        rank = jnp.zeros(gate.shape, F32)
        for sh in range(1, gate.shape[1]):
            other = pltpu.roll(gate, sh, 1)
            rank = rank + jnp.where(other > gate, 1.0, 0.0)
            rank = rank + jnp.where((other == gate) & (lane >= sh), 1.0, 0.0)
        lane_f = lane.astype(F32)
        out = jnp.zeros(gate.shape, F32)
        for j in range(MOBA_TOPK):
            pick = jnp.sum(jnp.where((rank == j) & (lane < nblk), lane_f, 0.0), axis=1, keepdims=True)
            out = jnp.where(lane == j, pick, out)
        idx_ref[0] = out.astype(jnp.int32)


def _sample_scores(qt, cache_kt, page_table, layer):
    nseq, wa, tq = qt.shape
    nheads = wa // HEAD_DIM
    page = cache_kt.shape[-1]
    n_pages = page_table.shape[1]
    nblk = n_pages * page // MOBA_BLOCK
    npg = SAMPLE_PAGES_PER_STEP
    nrows = tq * nheads

    def page_spec(e):
        return pl.BlockSpec((None, None, nheads, HEAD_DIM, page),
                            lambda b, s, pt: (layer, pt[b, s * npg + e], 0, 0, 0))

    grid_spec = pltpu.PrefetchScalarGridSpec(
        num_scalar_prefetch=1,
        grid=(nseq, n_pages // npg),
        in_specs=[pl.BlockSpec((1, wa, tq), lambda b, s, pt: (b, 0, 0))] + [page_spec(e) for e in range(npg)],
        out_specs=[pl.BlockSpec((1, npg * page // MOBA_BLOCK, nrows, MOBA_BLOCK), lambda b, s, pt: (b, s, 0, 0)),
                   pl.BlockSpec((1, nrows, 128), lambda b, s, pt: (b, 0, 0))],
        scratch_shapes=[pltpu.VMEM((nrows, HEAD_DIM, page), F32), pltpu.VMEM((nrows, 128), F32)],
    )
    return pl.pallas_call(
        functools.partial(_sample_scores_kernel, nheads=nheads, nblk=nblk, page=page, tq=tq),
        grid_spec=grid_spec,
        out_shape=[jax.ShapeDtypeStruct((nseq, nblk, nrows, MOBA_BLOCK), F32),
                   jax.ShapeDtypeStruct((nseq, nrows, 128), jnp.int32)],
        compiler_params=_cparams("parallel", "arbitrary"),
        name="sample_scores",
    )(page_table, qt, *([cache_kt] * npg))


def _sample_attend_kernel(pt_ref, idx_ref, s_ref, qt_ref, knt_ref, vnt_ref, blast_ref, cfar_ref, bown_ref,
                          cv_ref, o_ref, vbuf, sems, *, nheads, nblk, page, tq, layer):
    b = pl.program_id(0)
    nrows = tq * nheads
    ppb = MOBA_BLOCK // page

    def slab_copies(p):
        h = p % nheads
        out = []
        for j in range(MOBA_TOPK):
            n = idx_ref[b, p * MOBA_TOPK + j]
            for e in range(ppb):
                pg = pt_ref[b, n * ppb + e]
                out.append(pltpu.make_async_copy(cv_ref.at[layer, pg, h], vbuf.at[p, j * ppb + e], sems.at[p]))
        return out

    for p in range(nrows):
        for cp in slab_copies(p):
            cp.start()

    o_ref[0] = jnp.zeros(o_ref.shape[1:], F32)
    qt = qt_ref[0]
    for p in range(nrows):
        t, h = divmod(p, nheads)
        hs = slice(h * HEAD_DIM, (h + 1) * HEAD_DIM)
        qb = jnp.broadcast_to(qt[hs, t:t + 1], (HEAD_DIM, knt_ref.shape[2]))
        l_own = jnp.sum(knt_ref[0, hs, :] * qb, axis=0, keepdims=True) + bown_ref[p:p + 1, :]
        logits = []
        for j in range(MOBA_TOPK):
            n = idx_ref[b, p * MOBA_TOPK + j]
            bias = jnp.where(n == nblk - 1, blast_ref[p:p + 1, :], cfar_ref[p:p + 1, :])
            logits.append(s_ref[0, n, p:p + 1, :] + bias)
        m = jnp.max(l_own, axis=1, keepdims=True)
        for lg in logits:
            m = jnp.maximum(m, jnp.max(lg, axis=1, keepdims=True))
        p_own = jnp.exp(l_own - m)
        denom = jnp.sum(p_own, axis=1, keepdims=True)
        acc = p_own * vnt_ref[0, hs, :]
        for cp in slab_copies(p):
            cp.wait()
        for j in range(MOBA_TOPK):
            pj = jnp.exp(logits[j] - m)
            denom = denom + jnp.sum(pj, axis=1, keepdims=True)
            for e in range(ppb):
                acc = acc + pj[:, e * page:(e + 1) * page] * vbuf[p, j * ppb + e]
        o_ref[0, hs, t:t + 1] = jnp.sum(acc, axis=1, keepdims=True) / denom


def _sample_attend(scores, idx, qt, knt, vnt, cache_vt, page_table, layer, blast, cfar, bown):
    nseq, nblk, nrows, _ = scores.shape
    wa, tq = qt.shape[1], qt.shape[2]
    nheads = wa // HEAD_DIM
    page = cache_vt.shape[-1]
    assert knt.shape[2] == page and bown.shape[1] == page
    seq = lambda shape: pl.BlockSpec((1,) + shape, lambda b, pt, ix: (b,) + (0,) * len(shape))
    whole = lambda shape: pl.BlockSpec(shape, lambda b, pt, ix: (0,) * len(shape))
    grid_spec = pltpu.PrefetchScalarGridSpec(
        num_scalar_prefetch=2,
        grid=(nseq,),
        in_specs=[seq((nblk, nrows, MOBA_BLOCK)), seq((wa, tq)), seq((wa, page)), seq((wa, page)),
                  whole(blast.shape), whole(cfar.shape), whole(bown.shape),
                  pl.BlockSpec(memory_space=pl.ANY)],
        out_specs=seq((wa, 128)),
        scratch_shapes=[pltpu.VMEM((nrows, MOBA_TOPK * (MOBA_BLOCK // page), HEAD_DIM, page), F32),
                        pltpu.SemaphoreType.DMA((nrows,))],
    )
    return pl.pallas_call(
        functools.partial(_sample_attend_kernel, nheads=nheads, nblk=nblk, page=page, tq=tq, layer=layer),
        grid_spec=grid_spec,
        out_shape=jax.ShapeDtypeStruct((nseq, wa, 128), F32),
        compiler_params=_cparams("arbitrary"),
        name="sample_attend",
    )(page_table, idx, scores, qt, knt, vnt, blast, cfar, bown, cache_vt)


def _gdn_kernel(q_ref, k_ref, v_ref, kt_ref, gbc_ref, gbr_ref, s0_ref, onw_ref, o_ref, sout_ref,
                s_s, lhs_s, add_s, *, nheads, nchunk, chunks_per_iter):
    tc = pl.program_id(1)
    c_len = GDN_CHUNK
    hd = HEAD_DIM

    @pl.when(tc == 0)
    def _():
        s_s[...] = s0_ref[0]

    row = lax.broadcasted_iota(jnp.int32, (c_len, c_len), 0)
    col = lax.broadcasted_iota(jnp.int32, (c_len, c_len), 1)
    tri = row >= col
    tri_s = row > col
    eye = row == col
    eye_f = jnp.where(eye, 1.0, 0.0)
    blk = lambda size: (row // size) == (col // size)

    def chunk_a(it, carry):
        chains = [(it * chunks_per_iter + cc, h) for cc in range(chunks_per_iter) for h in range(nheads)]
        st = []
        for c, h in chains:
            gbc = gbc_ref[0, pl.ds(c * c_len, c_len), :]
            gbr = gbr_ref[0, c]
            q = q_ref[0, h, pl.ds(c * c_len, c_len), :]
            k = k_ref[0, h, pl.ds(c * c_len, c_len), :]
            v = v_ref[0, h, pl.ds(c * c_len, c_len), :]
            bcol = gbc[:, h:h + 1]
            gcol = gbc[:, nheads + h:nheads + h + 1]
            grow = gbr[nheads + h:nheads + h + 1, :]
            glast = gcol[c_len - 1:c_len, :]
            st.append(dict(c=c, h=h, q=q, k=k, v=v, bcol=bcol, grow=grow, glast=glast,
                           decay=jnp.where(tri, jnp.exp(gcol - grow), 0.0), egc=jnp.exp(gcol)))
        for d in st:
            d["kq"] = _mm1(jnp.concatenate([d["k"], d["q"]], axis=0), d["k"], _NT)
        for d in st:
            kk, qk = d["kq"][:c_len], d["kq"][c_len:]
            d["n"] = jnp.where(tri_s, d["bcol"] * kk * d["decay"], 0.0)
            d["qkd"] = jnp.where(tri, qk * d["decay"], 0.0)
            d["x"] = jnp.concatenate([d["v"] * d["bcol"], d["k"] * (d["bcol"] * d["egc"])], axis=1)
            d["a"] = -jnp.where(blk(INV_BASE), d["n"], 0.0)
        for d in st:
            d["p2"] = _mm2(d["a"], d["a"])
        for d in st:
            d["res"] = _mm2(jnp.concatenate([d["p2"], eye_f + d["a"]], axis=0), d["p2"])
        for d in st:
            d["p4"] = d["res"][:c_len]
            d["t"] = eye_f + d["a"] + d["res"][c_len:]
        for d in st:
            d["res"] = _mm2(d["t"], d["p4"])
        for d in st:
            d["t"] = d["t"] + d["res"]
        size = 2 * INV_BASE
        while size <= c_len:
            off = blk(size) & jnp.logical_not(blk(size // 2))
            for d in st:
                d["y"] = _mm1(jnp.where(off, d["n"], 0.0), d["t"])
            for d in st:
                d["res"] = _mm1(d["t"], d["y"])
            for d in st:
                d["t"] = d["t"] - d["res"]
            size *= 2
        for d in st:
            d["res"] = _mm2(d["t"], d["x"])
        for d in st:
            d["x"] = d["res"]
        for d in st:
            kdt = kt_ref[0, d["h"], d["c"]] * jnp.exp(d["glast"] - d["grow"])
            d["res"] = _mm2(jnp.concatenate([d["qkd"], kdt], axis=0), d["x"])
        for d in st:
            top = jnp.concatenate([d["q"] * d["egc"], jnp.where(eye, jnp.exp(d["glast"]), 0.0)], axis=0)
            lhs_s[d["c"], d["h"]] = top - d["res"][:, hd:]
            add_s[d["c"], d["h"]] = d["res"][:, :hd]
        return carry

    lax.fori_loop(0, nchunk // chunks_per_iter, chunk_a, 0)

    onw = onw_ref[...]

    def chunk_b(c, carry):
        res = [_mm3(lhs_s[c, h], s_s[h]) + add_s[c, h] for h in range(nheads)]
        for h in range(nheads):
            o = res[h][:c_len]
            s_s[h] = res[h][c_len:]
            o = o * lax.rsqrt(jnp.mean(o * o, axis=-1, keepdims=True) + RMS_EPS) * onw
            o_ref[0, h, pl.ds(c * c_len, c_len), :] = o
        return carry

    lax.fori_loop(0, nchunk, chunk_b, 0)
    sout_ref[0] = s_s[...]


def _gdn(qh, kh, vh, kt, gbc, gbr, s0, onw, nchunk):
    b, nheads, t, hd = qh.shape
    tb = nchunk * GDN_CHUNK
    nt = t // tb
    head = pl.BlockSpec((1, nheads, tb, hd), lambda bi, ti: (bi, 0, ti, 0))
    state = pl.BlockSpec((1, nheads, hd, hd), lambda bi, ti: (bi, 0, 0, 0))
    return pl.pallas_call(
        functools.partial(_gdn_kernel, nheads=nheads, nchunk=nchunk, chunks_per_iter=min(4, nchunk)),
        grid=(b, nt),
        in_specs=[head, head, head,
                  pl.BlockSpec((1, nheads, nchunk, hd, GDN_CHUNK), lambda bi, ti: (bi, 0, ti, 0, 0)),
                  pl.BlockSpec((1, tb, 8), lambda bi, ti: (bi, ti, 0)),
                  pl.BlockSpec((1, nchunk, 8, GDN_CHUNK), lambda bi, ti: (bi, ti, 0, 0)),
                  state, _const_spec((1, hd))],
        out_specs=[head, state],
        out_shape=[jax.ShapeDtypeStruct((b, nheads, t, hd), F32),
                   jax.ShapeDtypeStruct((b, nheads, hd, hd), F32)],
        scratch_shapes=[
            pltpu.VMEM((nheads, hd, hd), F32),
            pltpu.VMEM((nchunk, nheads, GDN_CHUNK + hd, hd), F32),
            pltpu.VMEM((nchunk, nheads, GDN_CHUNK + hd, hd), F32),
        ],
        compiler_params=_cparams("parallel", "arbitrary"),
        name="deltanet",
    )(qh, kh, vh, kt, gbc, gbr, s0, onw)


def _out_kernel(x_ref, gate_ref, att_ref, za_ref, yb_ref, yc_ref, zc_ref, w_ref, o_ref):
    if len(yc_ref.shape) == 3:
        yc = jnp.concatenate([yc_ref[h] for h in range(yc_ref.shape[0])], axis=-1)
    else:
        yc = yc_ref[...]
    cat = jnp.concatenate([za_ref[...] * att_ref[...], yb_ref[...], yc * zc_ref[...]], axis=-1)
    y = jnp.dot(cat.astype(BF16), w_ref[...], preferred_element_type=F32)
    o_ref[...] = x_ref[...] + gate_ref[...] * y


def _out_proj(x2, gate2, att2, za2, yb2, yc, zc2, wout, rows_per_gate):
    rows, d = x2.shape
    tr = min(ROW_TILE, rows)
    wa, wb = att2.shape[1], yb2.shape[1]
    row = lambda w: pl.BlockSpec((tr, w), lambda i: (i, 0))
    if rows_per_gate == 1:
        gate_spec = row(d)
    else:
        per = rows_per_gate // tr
        gate2 = gate2.reshape(gate2.shape[0], 1, d)
        gate_spec = pl.BlockSpec((None, 1, d), lambda i: (i // per, 0, 0))
    if yc.ndim == 4:
        per_seq = yc.shape[2] // tr
        yc_spec = pl.BlockSpec((None, yc.shape[1], tr, yc.shape[3]), lambda i: (i // per_seq, 0, i % per_seq, 0))
    else:
        yc_spec = row(wb)
    return pl.pallas_call(
        _out_kernel,
        grid=(rows // tr,),
        in_specs=[row(d), gate_spec, row(wa), row(wa), row(wb), yc_spec, row(wb), _const_spec((d, d))],
        out_specs=row(d),
        out_shape=jax.ShapeDtypeStruct((rows, d), F32),
        compiler_params=_cparams("parallel"),
        name="out_proj",
    )(x2, gate2, att2, za2, yb2, yc, zc2, wout)


def _block_diag_ones(n, blk, dtype):
    r = np.arange(n)
    return jnp.asarray((r[:, None] // blk) == (r[None, :] // blk), dtype)


def _layer_weights(l, norm_w, w_in, q_norm_w, k_norm_w, conv_b_w, conv_c_w, a_log, dt_bias, o_norm_w,
                   w_out, nseq, ts):
    d = w_in.shape[1]
    wa, wb = d // 2, d // 4
    nheads_a = wa // HEAD_DIM
    hc = wb // HEAD_DIM
    n_main = 4 * wa + 4 * wb + 4 * wb
    w = w_in[l]
    wcat = jnp.concatenate([w[:, :n_main], jnp.pad(w[:, n_main:], ((0, 0), (0, 128 - 2 * hc)))], axis=1)
    r = np.arange(ROW_TILE)
    cum_prompt = ((r[:, None] // GDN_CHUNK == r[None, :] // GDN_CHUNK) & (r[:, None] >= r[None, :]))
    rs = np.arange(nseq * ts)
    cum_sample = ((rs[:, None] % nseq == rs[None, :] % nseq) & (rs[:, None] >= rs[None, :]))
    return dict(
        norm_w=norm_w[l].reshape(1, d),
        wcat=wcat.astype(BF16),
        qw=jnp.tile(q_norm_w[l], nheads_a).reshape(1, wa),
        kw=jnp.tile(k_norm_w[l], nheads_a).reshape(1, wa),
        bd512=_block_diag_ones(wa, HEAD_DIM, BF16),
        bd256=_block_diag_ones(wb, HEAD_DIM, BF16),
        conv_b_w=conv_b_w[l], conv_c_w=conv_c_w[l],
        alog8=jnp.concatenate([jnp.zeros((hc,), F32), a_log[l]]).reshape(1, 2 * hc),
        dtb8=jnp.concatenate([jnp.zeros((hc,), F32), dt_bias[l]]).reshape(1, 2 * hc),
        cum_prompt=jnp.asarray(cum_prompt, F32),
        cum_sample=jnp.asarray(cum_sample, F32),
        onw=o_norm_w[l].reshape(1, HEAD_DIM),
        wout=w_out[l].astype(BF16),
    )


def _heads(a, nheads):
    b, t, _ = a.shape
    return a.reshape(b, t, nheads, HEAD_DIM).transpose(0, 2, 1, 3)


def _gdn_layout(qc, kc, vc, gb, hc):
    b, t, _ = qc.shape
    nch = t // GDN_CHUNK
    kt = kc.reshape(b, nch, GDN_CHUNK, hc, HEAD_DIM).transpose(0, 3, 1, 4, 2)
    gbr = gb.reshape(b, nch, GDN_CHUNK, 2 * hc).transpose(0, 1, 3, 2)
    return _heads(qc, hc), _heads(kc, hc), _heads(vc, hc), kt, gb, gbr


def kernel(x_prompt, x_sample, c_prompt, c_sample, cache_k, cache_v, page_table, state_conv_b,
           state_conv_c, state_delta, norm_w, ada_w, ada_b, w_in, q_norm_w, k_norm_w, rel_bias,
           conv_b_w, conv_c_w, a_log, dt_bias, o_norm_w, w_out):
    bp, tp, d = x_prompt.shape
    bs, ts, _ = x_sample.shape
    depth = ada_w.shape[0]
    wa, wb = d // 2, d // 4
    ha, hc = wa // HEAD_DIM, wb // HEAD_DIM
    page = cache_k.shape[2]
    past_len = page_table.shape[1] * page
    nblk_p = tp // MOBA_BLOCK
    nblk_s = past_len // MOBA_BLOCK
    assert tp % ROW_TILE == 0 and ROW_TILE == MOBA_BLOCK and past_len % MOBA_BLOCK == 0
    assert MOBA_BLOCK % page == 0 and page_table.shape[1] % SAMPLE_PAGES_PER_STEP == 0 and ts <= GDN_CHUNK
    assert SAMPLE_PAGES_PER_STEP % (MOBA_BLOCK // page) == 0 and page == 128

    mod = _modulation(jnp.concatenate([c_prompt, c_sample], axis=0), ada_w, ada_b)

    tab_h = rel_bias.T.astype(F32)
    kq = np.arange(MOBA_BLOCK)
    d_own = kq[None, :] - kq[:, None]
    dist_p = np.stack([np.broadcast_to(d_own + o, (ha, MOBA_BLOCK, MOBA_BLOCK)) for o in (0, MOBA_BLOCK)])
    bt = _bias_tiles(jnp.asarray(dist_p.reshape(-1, MOBA_BLOCK), jnp.int32),
                     jnp.repeat(jnp.tile(tab_h, (2, 1)), MOBA_BLOCK, axis=0))
    bt = bt.reshape(2, ha, MOBA_BLOCK, MOBA_BLOCK)
    cfar_p = bt[1, :, 0, MOBA_BLOCK - 1]

    tq = np.repeat(np.arange(ts), ha)
    pos_q = past_len + tq
    d_last = pos_q[:, None] - ((nblk_s - 1) * MOBA_BLOCK + kq[None, :])
    own_w = 128
    tk = np.arange(own_w)
    d_own_s = np.where(tk[None, :] < ts, tq[:, None] - tk[None, :], -1)
    d_far = np.broadcast_to(pos_q[:, None] - (nblk_s - 2) * MOBA_BLOCK - (MOBA_BLOCK - 1), (ts * ha, 128))
    tab_s = jnp.tile(tab_h, (ts, 1))
    blast = _bias_tiles(jnp.asarray(d_last, jnp.int32), tab_s)
    bown = _bias_tiles(jnp.asarray(d_own_s, jnp.int32), tab_s)
    cfar_s = _bias_tiles(jnp.asarray(d_far, jnp.int32), tab_s)[:, :1]

    cache_kt = cache_k.transpose(0, 1, 3, 4, 2)
    cache_vt = cache_v.transpose(0, 1, 3, 4, 2)

    hp = x_prompt
    hs_tm = x_sample.transpose(1, 0, 2).reshape(ts * bs, d)
    outs = [[] for _ in range(10)]
    for l in range(depth):
        lw = _layer_weights(l, norm_w, w_in, q_norm_w, k_norm_w, conv_b_w, conv_c_w, a_log, dt_bias,
                            o_norm_w, w_out, bs, ts)
        mod_p, mod_s = mod[l, :bp], mod[l, bp:]

        (q, k, v, kb, vt, kmean, za, yb, cbs, qh, kh, vh, kt, zc, gb, gbr, ccs) = _proj_prompt(hp, mod_p, lw)
        att = _attn_prompt(q, kb, vt, kmean.reshape(bp, nblk_p, wa), bt, cfar_p)
        nchunk_p = 8
        yc, s_new = _gdn(qh, kh, vh, kt, gb, gbr, jnp.zeros((bp, hc, HEAD_DIM, HEAD_DIM), F32),
                         lw["onw"], nchunk_p)
        hp = _out_proj(hp.reshape(bp * tp, d), mod_p[:, 2 * d:], att.reshape(bp * tp, wa),
                       za.reshape(bp * tp, wa), yb.reshape(bp * tp, wb), yc, zc.reshape(bp * tp, wb),
                       lw["wout"], tp).reshape(bp, tp, d)
        outs[0].append(k.reshape(bp, tp, ha, HEAD_DIM)); outs[1].append(v.reshape(bp, tp, ha, HEAD_DIM))
        outs[4].append(cbs); outs[6].append(ccs); outs[8].append(s_new)

        rep = lambda a: jnp.tile(a, (ts, 1))
        hist_b = state_conv_b[l].transpose(1, 0, 2).reshape(-1, wb)
        hist_c = state_conv_c[l].transpose(1, 0, 2).reshape(-1, 3 * wb)
        (q, k, v, za, yb, cbs, qc, kc, vc, zc, gb, ccs) = _proj_sample(
            hs_tm, rep(mod_s[:, :d]), rep(mod_s[:, d:2 * d]), lw, hist_b, hist_c, bs)
        seq_major = lambda a: a.reshape(ts, bs, -1).transpose(1, 0, 2)
        qt = seq_major(q).transpose(0, 2, 1)
        new_t = lambda a: jnp.pad(seq_major(a), ((0, 0), (0, own_w - ts), (0, 0))).transpose(0, 2, 1)
        scores, idx = _sample_scores(qt, cache_kt, page_table, l)
        att_t = _sample_attend(scores, idx[:, :, :MOBA_TOPK].reshape(bs, -1), qt, new_t(k), new_t(v),
                               cache_vt, page_table, l, blast, cfar_s, bown)
        att = att_t[:, :, :ts].transpose(2, 0, 1).reshape(ts * bs, wa)
        pad_t = lambda a: jnp.pad(seq_major(a), ((0, 0), (0, GDN_CHUNK - ts), (0, 0)))
        gbs = seq_major(gb)
        gb_pad = jnp.concatenate(
            [pad_t(gb[:, :hc]), jnp.pad(gbs[:, :, hc:], ((0, 0), (0, GDN_CHUNK - ts), (0, 0)), mode="edge")],
            axis=-1)
        yc, s_new = _gdn(*_gdn_layout(pad_t(qc), pad_t(kc), pad_t(vc), gb_pad, hc), state_delta[l],
                         lw["onw"], 1)
        yc = yc[:, :, :ts, :].transpose(2, 0, 1, 3).reshape(ts * bs, wb)
        hs_tm = _out_proj(hs_tm, rep(mod_s[:, 2 * d:]), att, za, yb, yc, zc, lw["wout"], 1)
        outs[2].append(seq_major(k).reshape(bs, ts, ha, HEAD_DIM))
        outs[3].append(seq_major(v).reshape(bs, ts, ha, HEAD_DIM))
        outs[5].append(cbs.reshape(-1, bs, wb).transpose(1, 0, 2))
        outs[7].append(ccs.reshape(-1, bs, 3 * wb).transpose(1, 0, 2))
        outs[9].append(s_new)

    y_sample = hs_tm.reshape(ts, bs, d).transpose(1, 0, 2)
    return (hp, y_sample) + tuple(jnp.stack(o) for o in outs)
```

```python
import functools
import math

import jax
import jax.numpy as jnp
import numpy as np
from jax import lax
from jax.experimental import pallas as pl
from jax.experimental.pallas import tpu as pltpu

F32 = jnp.float32
BF16 = jnp.bfloat16
HIGHEST = lax.Precision.HIGHEST

HEAD_DIM = 64
MOBA_BLOCK = 256
MOBA_TOPK = 3
GDN_CHUNK = 64
VT_PAD = 16
VT_ROWS = HEAD_DIM + VT_PAD
QK_AHEAD = 4
LOG2E = math.log2(math.e)
INV_BASE = 8
REL_BUCKETS = 32
REL_MAX_DIST = 128
RMS_EPS = 1e-6
ROW_TILE = 256
VMEM_LIMIT = 56 * 1024 * 1024
NEG_INF = float("-inf")

_NT = (((1,), (1,)), ((), ()))


def _cparams(*sem):
    return pltpu.CompilerParams(dimension_semantics=sem, vmem_limit_bytes=VMEM_LIMIT)


def _const_spec(shape):
    zeros = (0,) * len(shape)
    return pl.BlockSpec(shape, lambda *_: zeros)


def _silu(x):
    return x * jax.nn.sigmoid(x)


def _softplus(x):
    return jnp.maximum(x, 0.0) + jnp.log(1.0 + jnp.exp(-jnp.abs(x)))


def _split(a):
    hi = a.astype(BF16)
    lo = (a - hi.astype(F32)).astype(BF16)
    return hi, lo


def _dot(a, b, dims=None):
    if dims is None:
        return jnp.dot(a, b, preferred_element_type=F32)
    return lax.dot_general(a, b, dims, preferred_element_type=F32)


def _mm1(a, b, dims=None):
    return _dot(a.astype(BF16), b.astype(BF16), dims)


def _mm2(a, b, dims=None):
    ah = a.astype(BF16)
    bh, bl = _split(b)
    return _dot(ah, bh, dims) + _dot(ah, bl, dims)


def _mm3(a, b, dims=None):
    ah, al = _split(a)
    bh, bl = _split(b)
    return _dot(ah, bh, dims) + _dot(al, bh, dims) + _dot(ah, bl, dims)


def _group_sum(x):
    lane = lax.broadcasted_iota(jnp.int32, (1, 2 * HEAD_DIM), 1)
    first = lane < HEAD_DIM
    outs = []
    for c in range(x.shape[1] // (2 * HEAD_DIM)):
        xc = x[:, c * 2 * HEAD_DIM:(c + 1) * 2 * HEAD_DIM]
        s0 = jnp.sum(jnp.where(first, xc, 0.0), axis=-1, keepdims=True)
        s1 = jnp.sum(jnp.where(first, 0.0, xc), axis=-1, keepdims=True)
        outs.append(jnp.where(first, s0, s1))
    return jnp.concatenate(outs, axis=1)


def _mod_kernel(c_ref, w_ref, b_ref, o_ref):
    o_ref[0] = jnp.dot(c_ref[...], w_ref[0], precision=HIGHEST, preferred_element_type=F32) + b_ref[0]


def _modulation(c_all, ada_w, ada_b):
    depth, d, n3 = ada_w.shape
    rows = c_all.shape[0]
    tn = 512
    return pl.pallas_call(
        _mod_kernel,
        grid=(depth, n3 // tn),
        in_specs=[
            pl.BlockSpec((rows, d), lambda l, n: (0, 0)),
            pl.BlockSpec((1, d, tn), lambda l, n: (l, 0, n)),
            pl.BlockSpec((1, 1, tn), lambda l, n: (l, 0, n)),
        ],
        out_specs=pl.BlockSpec((1, rows, tn), lambda l, n: (l, 0, n)),
        out_shape=jax.ShapeDtypeStruct((depth, rows, n3), F32),
        compiler_params=_cparams("parallel", "parallel"),
        name="modulation",
    )(c_all, ada_w, ada_b.reshape(depth, 1, n3))


def _bias_kernel(dist_ref, tab_ref, o_ref):
    dist = dist_ref[...]
    n = jnp.maximum(dist, 0)
    max_exact = REL_BUCKETS // 2
    nf = jnp.maximum(n, max_exact).astype(F32)
    large = max_exact + (jnp.log(nf / max_exact) / math.log(REL_MAX_DIST / max_exact)
                         * (REL_BUCKETS - max_exact)).astype(jnp.int32)
    bucket = jnp.where(n < max_exact, n, jnp.minimum(large, REL_BUCKETS - 1))
    tab = tab_ref[...]
    acc = jnp.zeros(dist.shape, F32)
    for b in range(REL_BUCKETS):
        acc = jnp.where(bucket == b, tab[:, b:b + 1], acc)
    o_ref[...] = jnp.where(dist >= 0, acc, NEG_INF)


def _bias_tiles(dist, tab):
    rows, cols = dist.shape
    tr = min(rows, 256)
    return pl.pallas_call(
        _bias_kernel,
        grid=(rows // tr,),
        in_specs=[pl.BlockSpec((tr, cols), lambda i: (i, 0)),
                  pl.BlockSpec((tr, REL_BUCKETS), lambda i: (i, 0))],
        out_specs=pl.BlockSpec((tr, cols), lambda i: (i, 0)),
        out_shape=jax.ShapeDtypeStruct((rows, cols), F32),
        compiler_params=_cparams("parallel"),
        name="bias_tiles",
    )(dist, tab)


def _proj_common(x, shift, scale, normw, w_ref, qw, kw, alog, dtb, cum_ref):
    ms = jnp.mean(x * x, axis=-1, keepdims=True)
    hn = (x * lax.rsqrt(ms + RMS_EPS) * normw) * (1.0 + scale) + shift
    z = jnp.dot(hn.astype(BF16), w_ref[...], preferred_element_type=F32)
    wa = qw.shape[-1]
    wb = wa // 2
    o = 0
    q = z[:, o:o + wa]; o += wa
    k = z[:, o:o + wa]; o += wa
    v = z[:, o:o + wa]; o += wa
    za = z[:, o:o + wa]; o += wa
    hb = z[:, o:o + wb]; o += wb
    bg = z[:, o:o + wb]; o += wb
    cg = z[:, o:o + wb]; o += wb
    zb = z[:, o:o + wb]; o += wb
    qkv = z[:, o:o + 3 * wb]; o += 3 * wb
    zc = z[:, o:o + wb]; o += wb
    z8 = z[:, o:o + 8]

    qn = q * lax.rsqrt(_group_sum(q * q) * (1.0 / HEAD_DIM) + RMS_EPS) * qw
    kn = k * lax.rsqrt(_group_sum(k * k) * (1.0 / HEAD_DIM) + RMS_EPS) * kw

    lane8 = lax.broadcasted_iota(jnp.int32, z8.shape, 1)
    beta = jax.nn.sigmoid(z8)
    g = -jnp.exp(alog) * _softplus(z8 + dtb)
    bg8 = jnp.where(lane8 < 4, beta, g)
    cum = jnp.dot(cum_ref[...], bg8, precision=HIGHEST, preferred_element_type=F32)
    gb = jnp.where(lane8 < 4, bg8, cum)
    return dict(q=qn * (HEAD_DIM ** -0.5), k=kn, v=v, za=_silu(za), u=cg * hb, zbg=_silu(zb) * bg,
                qkv=qkv, zc=_silu(zc), gb=gb)


def _gdn_qkv(y):
    wb = y.shape[-1] // 3
    y = _silu(y)
    qc, kc, vc = y[:, :wb], y[:, wb:2 * wb], y[:, 2 * wb:]
    qc = qc * lax.rsqrt(_group_sum(qc * qc) + RMS_EPS) * (HEAD_DIM ** -0.5)
    kc = kc * lax.rsqrt(_group_sum(kc * kc) + RMS_EPS)
    return qc, kc, vc


def _proj_prompt_kernel(x_ref, mod_ref, normw_ref, w_ref, qw_ref, kw_ref,
                        cbw_ref, ccw_ref, alog_ref, dtb_ref, cum_ref,
                        q_ref, k_ref, v_ref, kb_ref, vt_ref, km_ref, za_ref, yb_ref, cbs_ref,
                        qc_ref, kc_ref, vc_ref, kt_ref, zc_ref, gb_ref, gbr_ref, ccs_ref,
                        carry_b, carry_c):
    ti = pl.program_id(1)
    d = x_ref.shape[-1]
    rows = x_ref.shape[1]

    @pl.when(ti == 0)
    def _():
        carry_b[...] = jnp.zeros_like(carry_b)
        carry_c[...] = jnp.zeros_like(carry_c)

    mod = mod_ref[0]
    p = _proj_common(x_ref[0], mod[:, :d], mod[:, d:2 * d], normw_ref[...], w_ref, qw_ref[...],
                     kw_ref[...], alog_ref[...], dtb_ref[...], cum_ref)
    q_ref[0] = p["q"]
    k_ref[0] = p["k"]
    v_ref[0] = p["v"]
    km_ref[0, 0] = jnp.mean(p["k"], axis=0, keepdims=True)
    za_ref[0] = p["za"]
    zc_ref[0] = p["zc"]
    gb = p["gb"]
    gb_ref[0] = gb
    kb_ref[0, 0] = p["k"].astype(BF16)
    vtb = p["v"].T.astype(BF16)
    ones = jnp.ones((VT_PAD, rows), BF16)
    pieces = []
    for h in range(vtb.shape[0] // HEAD_DIM):
        pieces += [vtb[h * HEAD_DIM:(h + 1) * HEAD_DIM], ones]
    vt_ref[0, 0] = jnp.concatenate(pieces, axis=0)
    ncol = gb.shape[1]
    eye8 = jnp.where(lax.broadcasted_iota(jnp.int32, (ncol, ncol), 0)
                     == lax.broadcasted_iota(jnp.int32, (ncol, ncol), 1), 1.0, 0.0)
    gbr = lax.dot_general(eye8, gb, _NT, precision=HIGHEST, preferred_element_type=F32)
    for cc in range(rows // GDN_CHUNK):
        gbr_ref[0, cc] = gbr[:, cc * GDN_CHUNK:(cc + 1) * GDN_CHUNK]

    u = p["u"]
    ext = jnp.concatenate([carry_b[...], u], axis=0)
    cbw = cbw_ref[...]
    nb = cbw.shape[0]
    yb = u * cbw[nb - 1:nb]
    for j in range(nb - 1):
        s = nb - 1 - j
        yb = yb + ext[8 - s:8 - s + rows] * cbw[j:j + 1]
    yb_ref[0] = p["zbg"] * yb
    carry_b[...] = u[rows - 8:]
    cbs_ref[0] = u[rows - (nb - 1):]

    qkv = p["qkv"]
    extc = jnp.concatenate([carry_c[...], qkv], axis=0)
    ccw = ccw_ref[...]
    nc = ccw.shape[0]
    yc = qkv * ccw[nc - 1:nc]
    for j in range(nc - 1):
        s = nc - 1 - j
        yc = yc + extc[8 - s:8 - s + rows] * ccw[j:j + 1]
    carry_c[...] = qkv[rows - 8:]
    ccs_ref[0] = qkv[rows - (nc - 1):]
    qc, kc, vc = _gdn_qkv(yc)
    kct = kc.T
    for h in range(qc.shape[1] // HEAD_DIM):
        hs = slice(h * HEAD_DIM, (h + 1) * HEAD_DIM)
        qc_ref[0, h] = qc[:, hs]
        kc_ref[0, h] = kc[:, hs]
        vc_ref[0, h] = vc[:, hs]
        for cc in range(rows // GDN_CHUNK):
            kt_ref[0, h, cc] = kct[hs, cc * GDN_CHUNK:(cc + 1) * GDN_CHUNK]


def _proj_sample_kernel(x_ref, shift_ref, scale_ref, normw_ref, w_ref, qw_ref, kw_ref,
                        cbw_ref, ccw_ref, alog_ref, dtb_ref, cum_ref, hb_ref, hc_ref,
                        q_ref, k_ref, v_ref, za_ref, yb_ref, cbs_ref,
                        qc_ref, kc_ref, vc_ref, zc_ref, gb_ref, ccs_ref, *, nseq):
    rows = x_ref.shape[0]
    p = _proj_common(x_ref[...], shift_ref[...], scale_ref[...], normw_ref[...], w_ref, qw_ref[...],
                     kw_ref[...], alog_ref[...], dtb_ref[...], cum_ref)
    q_ref[...] = p["q"]
    k_ref[...] = p["k"]
    v_ref[...] = p["v"]
    za_ref[...] = p["za"]
    zc_ref[...] = p["zc"]
    gb_ref[...] = p["gb"]

    cbw = cbw_ref[...]
    nb = cbw.shape[0]
    ext = jnp.concatenate([hb_ref[...], p["u"]], axis=0)
    yb = ext[0:rows] * cbw[0:1]
    for j in range(1, nb):
        yb = yb + ext[j * nseq:j * nseq + rows] * cbw[j:j + 1]
    yb_ref[...] = p["zbg"] * yb
    cbs_ref[...] = ext[rows:]

    ccw = ccw_ref[...]
    nc = ccw.shape[0]
    extc = jnp.concatenate([hc_ref[...], p["qkv"]], axis=0)
    yc = extc[0:rows] * ccw[0:1]
    for j in range(1, nc):
        yc = yc + extc[j * nseq:j * nseq + rows] * ccw[j:j + 1]
    ccs_ref[...] = extc[rows:]
    qc, kc, vc = _gdn_qkv(yc)
    qc_ref[...] = qc
    kc_ref[...] = kc
    vc_ref[...] = vc


def _proj_prompt(x, mod, lw):
    b, t, d = x.shape
    wcat = lw["wcat"]
    ncat = wcat.shape[1]
    wa, wb = d // 2, d // 4
    nt = t // ROW_TILE
    row = lambda w: pl.BlockSpec((1, ROW_TILE, w), lambda bi, ti: (bi, ti, 0))
    per_b = lambda r, w: pl.BlockSpec((1, r, w), lambda bi, ti: (bi, 0, 0))
    nb, nc = lw["conv_b_w"].shape[0], lw["conv_c_w"].shape[0]
    hc = wb // HEAD_DIM
    cpt = ROW_TILE // GDN_CHUNK
    nch = t // GDN_CHUNK
    out_shape = [
        jax.ShapeDtypeStruct((b, t, wa), F32),
        jax.ShapeDtypeStruct((b, t, wa), F32),
        jax.ShapeDtypeStruct((b, t, wa), F32),
        jax.ShapeDtypeStruct((b, nt, ROW_TILE, wa), BF16),
        jax.ShapeDtypeStruct((b, nt, wa // HEAD_DIM * VT_ROWS, ROW_TILE), BF16),
        jax.ShapeDtypeStruct((b, nt, 1, wa), F32),
        jax.ShapeDtypeStruct((b, t, wa), F32),
        jax.ShapeDtypeStruct((b, t, wb), F32),
        jax.ShapeDtypeStruct((b, nb - 1, wb), F32),
        jax.ShapeDtypeStruct((b, hc, t, HEAD_DIM), F32),
        jax.ShapeDtypeStruct((b, hc, t, HEAD_DIM), F32),
        jax.ShapeDtypeStruct((b, hc, t, HEAD_DIM), F32),
        jax.ShapeDtypeStruct((b, hc, nch, HEAD_DIM, GDN_CHUNK), F32),
        jax.ShapeDtypeStruct((b, t, wb), F32),
        jax.ShapeDtypeStruct((b, t, 8), F32),
        jax.ShapeDtypeStruct((b, nch, 8, GDN_CHUNK), F32),
        jax.ShapeDtypeStruct((b, nc - 1, 3 * wb), F32),
    ]
    blk4 = lambda r, c: pl.BlockSpec((1, 1, r, c), lambda bi, ti: (bi, ti, 0, 0))
    heads = pl.BlockSpec((1, hc, ROW_TILE, HEAD_DIM), lambda bi, ti: (bi, 0, ti, 0))
    out_specs = [
        row(wa), row(wa), row(wa), blk4(ROW_TILE, wa), blk4(wa // HEAD_DIM * VT_ROWS, ROW_TILE), blk4(1, wa),
        row(wa), row(wb), per_b(nb - 1, wb),
        heads, heads, heads,
        pl.BlockSpec((1, hc, cpt, HEAD_DIM, GDN_CHUNK), lambda bi, ti: (bi, 0, ti, 0, 0)),
        row(wb), row(8),
        pl.BlockSpec((1, cpt, 8, GDN_CHUNK), lambda bi, ti: (bi, ti, 0, 0)),
        per_b(nc - 1, 3 * wb),
    ]
    in_specs = [
        row(d),
        pl.BlockSpec((1, 1, 3 * d), lambda bi, ti: (bi, 0, 0)),
        _const_spec((1, d)),
        _const_spec((d, ncat)),
        _const_spec((1, wa)), _const_spec((1, wa)),
        _const_spec((nb, wb)), _const_spec((nc, 3 * wb)),
        _const_spec((1, 8)), _const_spec((1, 8)),
        _const_spec((ROW_TILE, ROW_TILE)),
    ]
    return pl.pallas_call(
        _proj_prompt_kernel,
        grid=(b, nt),
        in_specs=in_specs,
        out_specs=out_specs,
        out_shape=out_shape,
        scratch_shapes=[pltpu.VMEM((8, wb), F32), pltpu.VMEM((8, 3 * wb), F32)],
        compiler_params=_cparams("parallel", "arbitrary"),
        name="proj_prompt",
    )(x, mod.reshape(b, 1, 3 * d), lw["norm_w"], wcat, lw["qw"], lw["kw"],
      lw["conv_b_w"], lw["conv_c_w"], lw["alog8"], lw["dtb8"], lw["cum_prompt"])


def _proj_sample(x_tm, shift, scale, lw, hist_b, hist_c, nseq):
    rows, d = x_tm.shape
    wa, wb = d // 2, d // 4
    nb, nc = lw["conv_b_w"].shape[0], lw["conv_c_w"].shape[0]
    out_shape = [
        jax.ShapeDtypeStruct((rows, wa), F32), jax.ShapeDtypeStruct((rows, wa), F32),
        jax.ShapeDtypeStruct((rows, wa), F32), jax.ShapeDtypeStruct((rows, wa), F32),
        jax.ShapeDtypeStruct((rows, wb), F32),
        jax.ShapeDtypeStruct(((nb - 1) * nseq, wb), F32),
        jax.ShapeDtypeStruct((rows, wb), F32), jax.ShapeDtypeStruct((rows, wb), F32),
        jax.ShapeDtypeStruct((rows, wb), F32), jax.ShapeDtypeStruct((rows, wb), F32),
        jax.ShapeDtypeStruct((rows, 8), F32),
        jax.ShapeDtypeStruct(((nc - 1) * nseq, 3 * wb), F32),
    ]
    return pl.pallas_call(
        functools.partial(_proj_sample_kernel, nseq=nseq),
        out_shape=out_shape,
        compiler_params=pltpu.CompilerParams(vmem_limit_bytes=VMEM_LIMIT),
        name="proj_sample",
    )(x_tm, shift, scale, lw["norm_w"], lw["wcat"], lw["qw"], lw["kw"],
      lw["conv_b_w"], lw["conv_c_w"], lw["alog8"], lw["dtb8"], lw["cum_sample"], hist_b, hist_c)


def _attn_prompt_kernel(cfar_ref, q_ref, kb_ref, vt_ref, km_ref, bt_ref, o_ref,
                        sel_s, acc_s, *, nheads, nblk):
    i = pl.program_id(1)
    blk = q_ref.shape[1]
    qt = q_ref[0].T * LOG2E
    qtb = qt.astype(BF16)
    lane = lax.broadcasted_iota(jnp.int32, (1, 2 * HEAD_DIM), 1)
    sub = lax.broadcasted_iota(jnp.int32, (2 * HEAD_DIM, 1), 0)
    lane_half = [lane < HEAD_DIM, lane >= HEAD_DIM]
    sub_half = [sub < HEAD_DIM, sub >= HEAD_DIM]

    km = km_ref[0]
    n = lax.broadcasted_iota(jnp.int32, (nblk, blk), 0)
    n_f = n.astype(F32)
    past = n < i
    for h in range(nheads):
        p, par = divmod(h, 2)
        kmp = jnp.where(lane_half[par], km[:, p * 128:(p + 1) * 128], 0.0)
        gate = jnp.dot(kmp, qt[p * 128:(p + 1) * 128, :], precision=HIGHEST, preferred_element_type=F32)
        gate = jnp.where(past, gate, NEG_INF)
        picked = jnp.zeros((nblk, blk), F32)
        for _ in range(MOBA_TOPK):
            top = jnp.max(gate, axis=0, keepdims=True)
            first = jnp.min(jnp.where(gate == top, n_f, float(nblk)), axis=0, keepdims=True)
            chosen = n_f == first
            picked = jnp.where(chosen, 1.0, picked)
            gate = jnp.where(chosen, NEG_INF, gate)
        sel_s[h * nblk:(h + 1) * nblk, :] = jnp.where(past, picked, 0.0)

    qm = []
    for h in range(nheads):
        p, par = divmod(h, 2)
        qm.append(jnp.where(sub_half[par], qtb[p * 128:(p + 1) * 128, :], jnp.zeros((), BF16)))

    def scores(kblk, h):
        p = h // 2
        return jnp.dot(kblk[:, p * 128:(p + 1) * 128], qm[h], preferred_element_type=F32)

    def pv_dot(vblk, h, pr):
        res = jnp.dot(vblk[h * VT_ROWS:(h + 1) * VT_ROWS, :], pr.astype(BF16), preferred_element_type=F32)
        return res[:HEAD_DIM], res[HEAD_DIM:HEAD_DIM + 1]


    def block(js, kind, m_all, l_all):
        nb = len(js)
        kblk = kb_ref[0, js[0]] if nb == 1 else jnp.concatenate([kb_ref[0, j] for j in js], axis=0)
        sts = [scores(kblk, h) for h in range(min(QK_AHEAD, nheads))]
        m_rows, l_rows = [], []
        pend = None

        def finish(h, alpha, ons, pvs):
            rs = slice(h * HEAD_DIM, (h + 1) * HEAD_DIM)
            if kind == "own":
                acc_s[rs, :] = pvs[0]
            else:
                new = alpha * acc_s[rs, :]
                for on, pv in zip(ons, pvs):
                    new = new + jnp.where(on, pv, 0.0)
                acc_s[rs, :] = new

        for h in range(nheads):
            st = sts[h]
            if h + QK_AHEAD < nheads:
                sts.append(scores(kblk, h + QK_AHEAD))
            if kind == "own":
                st = bt_ref[0, h] * LOG2E + st
                m_new = jnp.max(st, axis=0, keepdims=True)
                pv, l_new = pv_dot(vt_ref[0, js[0]], h, jnp.exp2(st - m_new))
                alpha, ons, pvs = None, None, [pv]
            else:
                if kind == "near":
                    st = bt_ref[1, h] * LOG2E + st
                    shift = 0.0
                else:
                    shift = cfar_ref[h] * LOG2E
                parts = [st[s * blk:(s + 1) * blk] for s in range(nb)]
                ons = [sel_s[pl.ds(h * nblk + j, 1), :] > 0.0 for j in js]
                m_old = m_all[h:h + 1, :]
                m_new = m_old
                for on, part in zip(ons, parts):
                    m_new = jnp.maximum(m_new, jnp.where(on, jnp.max(part, axis=0, keepdims=True) + shift, NEG_INF))
                alpha = jnp.exp2(m_old - m_new)
                l_new = alpha * l_all[h:h + 1, :]
                pvs = []
                for j, on, part in zip(js, ons, parts):
                    pv, psum = pv_dot(vt_ref[0, j], h, jnp.exp2(part - (m_new - shift)))
                    pvs.append(pv)
                    l_new = l_new + jnp.where(on, psum, 0.0)
            m_rows.append(m_new)
            l_rows.append(l_new)
            if pend is not None:
                finish(*pend)
            pend = (h, alpha, ons, pvs)
        finish(*pend)
        return jnp.concatenate(m_rows, axis=0), jnp.concatenate(l_rows, axis=0)

    keep = lambda m, l: (m, l)
    m_all, l_all = block([i], "own", None, None)
    m_all, l_all = lax.cond(i >= 1, lambda m, l: block([i - 1], "near", m, l), keep, m_all, l_all)
    nfar = i - 1
    odd = (nfar >= 1) & ((nfar & 1) == 1)
    m_all, l_all = lax.cond(odd, lambda m, l: block([nfar - 1], "far", m, l), keep, m_all, l_all)
    m_all, l_all = lax.fori_loop(0, nfar // 2, lambda jj, c: block([2 * jj, 2 * jj + 1], "far", *c), (m_all, l_all))

    for h in range(nheads):
        rs = slice(h * HEAD_DIM, (h + 1) * HEAD_DIM)
        acc_s[rs, :] = acc_s[rs, :] / l_all[h:h + 1, :]
    o_ref[0] = acc_s[...].T


def _attn_prompt(q, kb, vt, kmean, bt, cfar):
    b, t, wa = q.shape
    nheads = wa // HEAD_DIM
    nblk = t // MOBA_BLOCK
    grid_spec = pltpu.PrefetchScalarGridSpec(
        num_scalar_prefetch=0,
        grid=(b, nblk),
        in_specs=[
            pl.BlockSpec(memory_space=pltpu.SMEM),
            pl.BlockSpec((1, MOBA_BLOCK, wa), lambda bi, i: (bi, i, 0)),
            pl.BlockSpec((1, nblk, MOBA_BLOCK, wa), lambda bi, i: (bi, 0, 0, 0)),
            pl.BlockSpec((1, nblk, nheads * VT_ROWS, MOBA_BLOCK), lambda bi, i: (bi, 0, 0, 0)),
            pl.BlockSpec((1, nblk, wa), lambda bi, i: (bi, 0, 0)),
            pl.BlockSpec((2, nheads, MOBA_BLOCK, MOBA_BLOCK), lambda bi, i: (0, 0, 0, 0)),
        ],
        out_specs=pl.BlockSpec((1, MOBA_BLOCK, wa), lambda bi, i: (bi, i, 0)),
        scratch_shapes=[
            pltpu.VMEM((nheads * nblk, MOBA_BLOCK), F32),
            pltpu.VMEM((wa, MOBA_BLOCK), F32),
        ],
    )
    return pl.pallas_call(
        functools.partial(_attn_prompt_kernel, nheads=nheads, nblk=nblk),
        grid_spec=grid_spec,
        out_shape=jax.ShapeDtypeStruct((b, t, wa), F32),
        compiler_params=_cparams("parallel", "arbitrary"),
        name="attn_prompt",
    )(cfar, q, kb, vt, kmean, bt)


SAMPLE_PAGES_PER_STEP = 16


def _sample_scores_kernel(pt_ref, qt_ref, *rest, nheads, nblk, page, tq):
    npg = SAMPLE_PAGES_PER_STEP
    k_refs = rest[:npg]
    s_ref, idx_ref, qb_s, gate_s = rest[npg:]
    s = pl.program_id(1)
    nrows = tq * nheads
    ppb = MOBA_BLOCK // page

    @pl.when(s == 0)
    def _():
        qt = qt_ref[0]
        for t in range(tq):
            for h in range(nheads):
                qb_s[t * nheads + h] = jnp.broadcast_to(qt[h * HEAD_DIM:(h + 1) * HEAD_DIM, t:t + 1],
                                                        (HEAD_DIM, page))
        gate_s[...] = jnp.full(gate_s.shape, NEG_INF, F32)

    lane = lax.broadcasted_iota(jnp.int32, gate_s.shape, 1)
    bsum = None
    for e in range(npg):
        groups = []
        for t in range(tq):
            rows = [jnp.sum(k_refs[e][h] * qb_s[t * nheads + h], axis=0, keepdims=True) for h in range(nheads)]
            groups.append(jnp.concatenate(rows, axis=0))
        sp = jnp.concatenate(groups, axis=0)
        s_ref[0, e // ppb, :, (e % ppb) * page:(e % ppb + 1) * page] = sp
        rs = jnp.sum(sp, axis=1, keepdims=True)
        bsum = rs if e % ppb == 0 else bsum + rs
        if e % ppb == ppb - 1:
            n = s * (npg // ppb) + e // ppb
            gate_s[...] = jnp.where(lane == n, bsum * (1.0 / MOBA_BLOCK), gate_s[...])

    @pl.when(s == pl.num_programs(1) - 1)
    def _():
        gate = gate_s[...]
        rank = jnp.zeros(gate.shape, F32)
        for sh in range(1, gate.shape[1]):
            other = pltpu.roll(gate, sh, 1)
            rank = rank + jnp.where(other > gate, 1.0, 0.0)
            rank = rank + jnp.where((other == gate) & (lane >= sh), 1.0, 0.0)
        lane_f = lane.astype(F32)
        out = jnp.zeros(gate.shape, F32)
        for j in range(MOBA_TOPK):
            pick = jnp.sum(jnp.where((rank == j) & (lane < nblk), lane_f, 0.0), axis=1, keepdims=True)
            out = jnp.where(lane == j, pick, out)
        idx_ref[0] = out.astype(jnp.int32)


def _sample_scores(qt, cache_kt, page_table, layer):
    nseq, wa, tq = qt.shape
    nheads = wa // HEAD_DIM
    page = cache_kt.shape[-1]
    n_pages = page_table.shape[1]
    nblk = n_pages * page // MOBA_BLOCK
    npg = SAMPLE_PAGES_PER_STEP
    nrows = tq * nheads

    def page_spec(e):
        return pl.BlockSpec((None, None, nheads, HEAD_DIM, page),
                            lambda b, s, pt: (layer, pt[b, s * npg + e], 0, 0, 0))

    grid_spec = pltpu.PrefetchScalarGridSpec(
        num_scalar_prefetch=1,
        grid=(nseq, n_pages // npg),
        in_specs=[pl.BlockSpec((1, wa, tq), lambda b, s, pt: (b, 0, 0))] + [page_spec(e) for e in range(npg)],
        out_specs=[pl.BlockSpec((1, npg * page // MOBA_BLOCK, nrows, MOBA_BLOCK), lambda b, s, pt: (b, s, 0, 0)),
                   pl.BlockSpec((1, nrows, 128), lambda b, s, pt: (b, 0, 0))],
        scratch_shapes=[pltpu.VMEM((nrows, HEAD_DIM, page), F32), pltpu.VMEM((nrows, 128), F32)],
    )
    return pl.pallas_call(
        functools.partial(_sample_scores_kernel, nheads=nheads, nblk=nblk, page=page, tq=tq),
        grid_spec=grid_spec,
        out_shape=[jax.ShapeDtypeStruct((nseq, nblk, nrows, MOBA_BLOCK), F32),
                   jax.ShapeDtypeStruct((nseq, nrows, 128), jnp.int32)],
        compiler_params=_cparams("parallel", "arbitrary"),
        name="sample_scores",
    )(page_table, qt, *([cache_kt] * npg))


def _sample_attend_kernel(pt_ref, idx_ref, s_ref, qt_ref, knt_ref, vnt_ref, blast_ref, cfar_ref, bown_ref,
                          cv_ref, o_ref, vbuf, sems, *, nheads, nblk, page, tq, layer):
    b = pl.program_id(0)
    nrows = tq * nheads
    ppb = MOBA_BLOCK // page

    def slab_copies(p):
        h = p % nheads
        out = []
        for j in range(MOBA_TOPK):
            n = idx_ref[b, p * MOBA_TOPK + j]
            for e in range(ppb):
                pg = pt_ref[b, n * ppb + e]
                out.append(pltpu.make_async_copy(cv_ref.at[layer, pg, h], vbuf.at[p, j * ppb + e], sems.at[p]))
        return out

    copies = [slab_copies(p) for p in range(nrows)]
    for row_copies in copies:
        for cp in row_copies:
            cp.start()

    qt = qt_ref[0]
    own, past = [], [[] for _ in range(MOBA_TOPK)]
    for p in range(nrows):
        t, h = divmod(p, nheads)
        hs = slice(h * HEAD_DIM, (h + 1) * HEAD_DIM)
        qb = jnp.broadcast_to(qt[hs, t:t + 1], (HEAD_DIM, knt_ref.shape[2]))
        own.append(jnp.sum(knt_ref[0, hs, :] * qb, axis=0, keepdims=True) + bown_ref[p:p + 1, :])
        for j in range(MOBA_TOPK):
            n = idx_ref[b, p * MOBA_TOPK + j]
            bias = jnp.where(n == nblk - 1, blast_ref[p:p + 1, :], cfar_ref[p:p + 1, :])
            past[j].append(s_ref[0, n, p:p + 1, :] + bias)
    l_own = jnp.concatenate(own, axis=0)
    l_past = [jnp.concatenate(rows, axis=0) for rows in past]
    m = jnp.max(l_own, axis=1, keepdims=True)
    for lg in l_past:
        m = jnp.maximum(m, jnp.max(lg, axis=1, keepdims=True))
    w_own = jnp.exp(l_own - m)
    w_past = [jnp.exp(lg - m) for lg in l_past]
    denom = jnp.sum(w_own, axis=1, keepdims=True)
    for w in w_past:
        denom = denom + jnp.sum(w, axis=1, keepdims=True)
    inv = 1.0 / denom
    w_own = w_own * inv
    w_past = [w * inv for w in w_past]

    for row_copies in copies:
        for cp in row_copies:
            cp.wait()

    o_ref[0] = jnp.zeros(o_ref.shape[1:], F32)
    for p in range(nrows):
        t, h = divmod(p, nheads)
        hs = slice(h * HEAD_DIM, (h + 1) * HEAD_DIM)
        acc = w_own[p:p + 1, :] * vnt_ref[0, hs, :]
        for j in range(MOBA_TOPK):
            for e in range(ppb):
                acc = acc + w_past[j][p:p + 1, e * page:(e + 1) * page] * vbuf[p, j * ppb + e]
        o_ref[0, hs, t:t + 1] = jnp.sum(acc, axis=1, keepdims=True)


def _sample_attend(scores, idx, qt, knt, vnt, cache_vt, page_table, layer, blast, cfar, bown):
    nseq, nblk, nrows, _ = scores.shape
    wa, tq = qt.shape[1], qt.shape[2]
    nheads = wa // HEAD_DIM
    page = cache_vt.shape[-1]
    assert knt.shape[2] == page and bown.shape[1] == page
    seq = lambda shape: pl.BlockSpec((1,) + shape, lambda b, pt, ix: (b,) + (0,) * len(shape))
    whole = lambda shape: pl.BlockSpec(shape, lambda b, pt, ix: (0,) * len(shape))
    grid_spec = pltpu.PrefetchScalarGridSpec(
        num_scalar_prefetch=2,
        grid=(nseq,),
        in_specs=[seq((nblk, nrows, MOBA_BLOCK)), seq((wa, tq)), seq((wa, page)), seq((wa, page)),
                  whole(blast.shape), whole(cfar.shape), whole(bown.shape),
                  pl.BlockSpec(memory_space=pl.ANY)],
        out_specs=seq((wa, 128)),
        scratch_shapes=[pltpu.VMEM((nrows, MOBA_TOPK * (MOBA_BLOCK // page), HEAD_DIM, page), F32),
                        pltpu.SemaphoreType.DMA((nrows,))],
    )
    return pl.pallas_call(
        functools.partial(_sample_attend_kernel, nheads=nheads, nblk=nblk, page=page, tq=tq, layer=layer),
        grid_spec=grid_spec,
        out_shape=jax.ShapeDtypeStruct((nseq, wa, 128), F32),
        compiler_params=_cparams("arbitrary"),
        name="sample_attend",
    )(page_table, idx, scores, qt, knt, vnt, blast, cfar, bown, cache_vt)


def _gdn_kernel(q_ref, k_ref, v_ref, kt_ref, gbc_ref, gbr_ref, s0_ref, onw_ref, o_ref, sout_ref,
                s_s, lhs_s, add_s, *, nheads, nchunk, chunks_per_iter):
    tc = pl.program_id(1)
    c_len = GDN_CHUNK
    hd = HEAD_DIM

    @pl.when(tc == 0)
    def _():
        s_s[...] = s0_ref[0]

    row = lax.broadcasted_iota(jnp.int32, (c_len, c_len), 0)
    col = lax.broadcasted_iota(jnp.int32, (c_len, c_len), 1)
    tri = row >= col
    tri_s = row > col
    eye = row == col
    eye_f = jnp.where(eye, 1.0, 0.0)
    blk = lambda size: (row // size) == (col // size)

    def chunk_a(it, carry):
        chains = [(it * chunks_per_iter + cc, h) for cc in range(chunks_per_iter) for h in range(nheads)]
        st = []
        for c, h in chains:
            gbc = gbc_ref[0, pl.ds(c * c_len, c_len), :]
            gbr = gbr_ref[0, c]
            q = q_ref[0, h, pl.ds(c * c_len, c_len), :]
            k = k_ref[0, h, pl.ds(c * c_len, c_len), :]
            v = v_ref[0, h, pl.ds(c * c_len, c_len), :]
            bcol = gbc[:, h:h + 1]
            gcol = gbc[:, nheads + h:nheads + h + 1]
            grow = gbr[nheads + h:nheads + h + 1, :]
            glast = gcol[c_len - 1:c_len, :]
            st.append(dict(c=c, h=h, q=q, k=k, v=v, bcol=bcol, grow=grow, glast=glast,
                           decay=jnp.where(tri, jnp.exp(gcol - grow), 0.0), egc=jnp.exp(gcol)))
        for d in st:
            d["kq"] = _mm1(jnp.concatenate([d["k"], d["q"]], axis=0), d["k"], _NT)
        for d in st:
            kk, qk = d["kq"][:c_len], d["kq"][c_len:]
            d["n"] = jnp.where(tri_s, d["bcol"] * kk * d["decay"], 0.0)
            d["qkd"] = jnp.where(tri, qk * d["decay"], 0.0)
            d["x"] = jnp.concatenate([d["v"] * d["bcol"], d["k"] * (d["bcol"] * d["egc"])], axis=1)
            d["a"] = -jnp.where(blk(INV_BASE), d["n"], 0.0)
        for d in st:
            d["p2"] = _mm2(d["a"], d["a"])
        for d in st:
            d["res"] = _mm2(jnp.concatenate([d["p2"], eye_f + d["a"]], axis=0), d["p2"])
        for d in st:
            d["p4"] = d["res"][:c_len]
            d["t"] = eye_f + d["a"] + d["res"][c_len:]
        for d in st:
            d["res"] = _mm2(d["t"], d["p4"])
        for d in st:
            d["t"] = d["t"] + d["res"]
        size = 2 * INV_BASE
        while size <= c_len:
            off = blk(size) & jnp.logical_not(blk(size // 2))
            for d in st:
                d["y"] = _mm1(jnp.where(off, d["n"], 0.0), d["t"])
            for d in st:
                d["res"] = _mm1(d["t"], d["y"])
            for d in st:
                d["t"] = d["t"] - d["res"]
            size *= 2
        for d in st:
            d["res"] = _mm2(d["t"], d["x"])
        for d in st:
            d["x"] = d["res"]
        for d in st:
            kdt = kt_ref[0, d["h"], d["c"]] * jnp.exp(d["glast"] - d["grow"])
            d["res"] = _mm2(jnp.concatenate([d["qkd"], kdt], axis=0), d["x"])
        for d in st:
            top = jnp.concatenate([d["q"] * d["egc"], jnp.where(eye, jnp.exp(d["glast"]), 0.0)], axis=0)
            lhs_s[d["c"], d["h"]] = top - d["res"][:, hd:]
            add_s[d["c"], d["h"]] = d["res"][:, :hd]
        return carry

    lax.fori_loop(0, nchunk // chunks_per_iter, chunk_a, 0)

    onw = onw_ref[...]

    def chunk_b(c, carry):
        res = [_mm3(lhs_s[c, h], s_s[h]) + add_s[c, h] for h in range(nheads)]
        for h in range(nheads):
            o = res[h][:c_len]
            s_s[h] = res[h][c_len:]
            o = o * lax.rsqrt(jnp.mean(o * o, axis=-1, keepdims=True) + RMS_EPS) * onw
            o_ref[0, h, pl.ds(c * c_len, c_len), :] = o
        return carry

    lax.fori_loop(0, nchunk, chunk_b, 0)
    sout_ref[0] = s_s[...]


def _gdn(qh, kh, vh, kt, gbc, gbr, s0, onw, nchunk):
    b, nheads, t, hd = qh.shape
    tb = nchunk * GDN_CHUNK
    nt = t // tb
    head = pl.BlockSpec((1, nheads, tb, hd), lambda bi, ti: (bi, 0, ti, 0))
    state = pl.BlockSpec((1, nheads, hd, hd), lambda bi, ti: (bi, 0, 0, 0))
    return pl.pallas_call(
        functools.partial(_gdn_kernel, nheads=nheads, nchunk=nchunk, chunks_per_iter=min(4, nchunk)),
        grid=(b, nt),
        in_specs=[head, head, head,
                  pl.BlockSpec((1, nheads, nchunk, hd, GDN_CHUNK), lambda bi, ti: (bi, 0, ti, 0, 0)),
                  pl.BlockSpec((1, tb, 8), lambda bi, ti: (bi, ti, 0)),
                  pl.BlockSpec((1, nchunk, 8, GDN_CHUNK), lambda bi, ti: (bi, ti, 0, 0)),
                  state, _const_spec((1, hd))],
        out_specs=[head, state],
        out_shape=[jax.ShapeDtypeStruct((b, nheads, t, hd), F32),
                   jax.ShapeDtypeStruct((b, nheads, hd, hd), F32)],
        scratch_shapes=[
            pltpu.VMEM((nheads, hd, hd), F32),
            pltpu.VMEM((nchunk, nheads, GDN_CHUNK + hd, hd), F32),
            pltpu.VMEM((nchunk, nheads, GDN_CHUNK + hd, hd), F32),
        ],
        compiler_params=_cparams("parallel", "arbitrary"),
        name="deltanet",
    )(qh, kh, vh, kt, gbc, gbr, s0, onw)


def _out_kernel(x_ref, gate_ref, att_ref, za_ref, yb_ref, yc_ref, zc_ref, w_ref, o_ref):
    if len(yc_ref.shape) == 3:
        yc = jnp.concatenate([yc_ref[h] for h in range(yc_ref.shape[0])], axis=-1)
    else:
        yc = yc_ref[...]
    cat = jnp.concatenate([za_ref[...] * att_ref[...], yb_ref[...], yc * zc_ref[...]], axis=-1)
    y = jnp.dot(cat.astype(BF16), w_ref[...], preferred_element_type=F32)
    o_ref[...] = x_ref[...] + gate_ref[...] * y


def _out_proj(x2, gate2, att2, za2, yb2, yc, zc2, wout, rows_per_gate):
    rows, d = x2.shape
    tr = min(ROW_TILE, rows)
    wa, wb = att2.shape[1], yb2.shape[1]
    row = lambda w: pl.BlockSpec((tr, w), lambda i: (i, 0))
    if rows_per_gate == 1:
        gate_spec = row(d)
    else:
        per = rows_per_gate // tr
        gate2 = gate2.reshape(gate2.shape[0], 1, d)
        gate_spec = pl.BlockSpec((None, 1, d), lambda i: (i // per, 0, 0))
    if yc.ndim == 4:
        per_seq = yc.shape[2] // tr
        yc_spec = pl.BlockSpec((None, yc.shape[1], tr, yc.shape[3]), lambda i: (i // per_seq, 0, i % per_seq, 0))
    else:
        yc_spec = row(wb)
    return pl.pallas_call(
        _out_kernel,
        grid=(rows // tr,),
        in_specs=[row(d), gate_spec, row(wa), row(wa), row(wb), yc_spec, row(wb), _const_spec((d, d))],
        out_specs=row(d),
        out_shape=jax.ShapeDtypeStruct((rows, d), F32),
        compiler_params=_cparams("parallel"),
        name="out_proj",
    )(x2, gate2, att2, za2, yb2, yc, zc2, wout)


def _layer_weights(l, norm_w, w_in, q_norm_w, k_norm_w, conv_b_w, conv_c_w, a_log, dt_bias, o_norm_w,
                   w_out, nseq, ts):
    d = w_in.shape[1]
    wa, wb = d // 2, d // 4
    nheads_a = wa // HEAD_DIM
    hc = wb // HEAD_DIM
    n_main = 4 * wa + 4 * wb + 4 * wb
    w = w_in[l]
    wcat = jnp.concatenate([w[:, :n_main], jnp.pad(w[:, n_main:], ((0, 0), (0, 128 - 2 * hc)))], axis=1)
    r = np.arange(ROW_TILE)
    cum_prompt = ((r[:, None] // GDN_CHUNK == r[None, :] // GDN_CHUNK) & (r[:, None] >= r[None, :]))
    rs = np.arange(nseq * ts)
    cum_sample = ((rs[:, None] % nseq == rs[None, :] % nseq) & (rs[:, None] >= rs[None, :]))
    return dict(
        norm_w=norm_w[l].reshape(1, d),
        wcat=wcat.astype(BF16),
        qw=jnp.tile(q_norm_w[l], nheads_a).reshape(1, wa),
        kw=jnp.tile(k_norm_w[l], nheads_a).reshape(1, wa),
        conv_b_w=conv_b_w[l], conv_c_w=conv_c_w[l],
        alog8=jnp.concatenate([jnp.zeros((hc,), F32), a_log[l]]).reshape(1, 2 * hc),
        dtb8=jnp.concatenate([jnp.zeros((hc,), F32), dt_bias[l]]).reshape(1, 2 * hc),
        cum_prompt=jnp.asarray(cum_prompt, F32),
        cum_sample=jnp.asarray(cum_sample, F32),
        onw=o_norm_w[l].reshape(1, HEAD_DIM),
        wout=w_out[l].astype(BF16),
    )


def _heads(a, nheads):
    b, t, _ = a.shape
    return a.reshape(b, t, nheads, HEAD_DIM).transpose(0, 2, 1, 3)


def _gdn_layout(qc, kc, vc, gb, hc):
    b, t, _ = qc.shape
    nch = t // GDN_CHUNK
    kt = kc.reshape(b, nch, GDN_CHUNK, hc, HEAD_DIM).transpose(0, 3, 1, 4, 2)
    gbr = gb.reshape(b, nch, GDN_CHUNK, 2 * hc).transpose(0, 1, 3, 2)
    return _heads(qc, hc), _heads(kc, hc), _heads(vc, hc), kt, gb, gbr


def kernel(x_prompt, x_sample, c_prompt, c_sample, cache_k, cache_v, page_table, state_conv_b,
           state_conv_c, state_delta, norm_w, ada_w, ada_b, w_in, q_norm_w, k_norm_w, rel_bias,
           conv_b_w, conv_c_w, a_log, dt_bias, o_norm_w, w_out):
    bp, tp, d = x_prompt.shape
    bs, ts, _ = x_sample.shape
    depth = ada_w.shape[0]
    wa, wb = d // 2, d // 4
    ha, hc = wa // HEAD_DIM, wb // HEAD_DIM
    page = cache_k.shape[2]
    past_len = page_table.shape[1] * page
    nblk_p = tp // MOBA_BLOCK
    nblk_s = past_len // MOBA_BLOCK
    assert tp % ROW_TILE == 0 and ROW_TILE == MOBA_BLOCK and past_len % MOBA_BLOCK == 0
    assert MOBA_BLOCK % page == 0 and page_table.shape[1] % SAMPLE_PAGES_PER_STEP == 0 and ts <= GDN_CHUNK
    assert SAMPLE_PAGES_PER_STEP % (MOBA_BLOCK // page) == 0 and page == 128

    mod = _modulation(jnp.concatenate([c_prompt, c_sample], axis=0), ada_w, ada_b)

    tab_h = rel_bias.T.astype(F32)
    kq = np.arange(MOBA_BLOCK)
    d_own = kq[None, :] - kq[:, None]
    dist_p = np.stack([np.broadcast_to(d_own + o, (ha, MOBA_BLOCK, MOBA_BLOCK)) for o in (0, MOBA_BLOCK)])
    bt = _bias_tiles(jnp.asarray(dist_p.reshape(-1, MOBA_BLOCK), jnp.int32),
                     jnp.repeat(jnp.tile(tab_h, (2, 1)), MOBA_BLOCK, axis=0))
    bt = bt.reshape(2, ha, MOBA_BLOCK, MOBA_BLOCK)
    cfar_p = bt[1, :, 0, MOBA_BLOCK - 1]

    tq = np.repeat(np.arange(ts), ha)
    pos_q = past_len + tq
    d_last = pos_q[:, None] - ((nblk_s - 1) * MOBA_BLOCK + kq[None, :])
    own_w = 128
    tk = np.arange(own_w)
    d_own_s = np.where(tk[None, :] < ts, tq[:, None] - tk[None, :], -1)
    d_far = np.broadcast_to(pos_q[:, None] - (nblk_s - 2) * MOBA_BLOCK - (MOBA_BLOCK - 1), (ts * ha, 128))
    tab_s = jnp.tile(tab_h, (ts, 1))
    blast = _bias_tiles(jnp.asarray(d_last, jnp.int32), tab_s)
    bown = _bias_tiles(jnp.asarray(d_own_s, jnp.int32), tab_s)
    cfar_s = _bias_tiles(jnp.asarray(d_far, jnp.int32), tab_s)[:, :1]

    cache_kt = cache_k.transpose(0, 1, 3, 4, 2)
    cache_vt = cache_v.transpose(0, 1, 3, 4, 2)

    hp = x_prompt
    hs_tm = x_sample.transpose(1, 0, 2).reshape(ts * bs, d)
    outs = [[] for _ in range(10)]
    for l in range(depth):
        lw = _layer_weights(l, norm_w, w_in, q_norm_w, k_norm_w, conv_b_w, conv_c_w, a_log, dt_bias,
                            o_norm_w, w_out, bs, ts)
        mod_p, mod_s = mod[l, :bp], mod[l, bp:]

        (q, k, v, kb, vt, kmean, za, yb, cbs, qh, kh, vh, kt, zc, gb, gbr, ccs) = _proj_prompt(hp, mod_p, lw)
        att = _attn_prompt(q, kb, vt, kmean.reshape(bp, nblk_p, wa), bt, cfar_p)
        nchunk_p = 8
        yc, s_new = _gdn(qh, kh, vh, kt, gb, gbr, jnp.zeros((bp, hc, HEAD_DIM, HEAD_DIM), F32),
                         lw["onw"], nchunk_p)
        hp = _out_proj(hp.reshape(bp * tp, d), mod_p[:, 2 * d:], att.reshape(bp * tp, wa),
                       za.reshape(bp * tp, wa), yb.reshape(bp * tp, wb), yc, zc.reshape(bp * tp, wb),
                       lw["wout"], tp).reshape(bp, tp, d)
        outs[0].append(k.reshape(bp, tp, ha, HEAD_DIM)); outs[1].append(v.reshape(bp, tp, ha, HEAD_DIM))
        outs[4].append(cbs); outs[6].append(ccs); outs[8].append(s_new)

        rep = lambda a: jnp.tile(a, (ts, 1))
        hist_b = state_conv_b[l].transpose(1, 0, 2).reshape(-1, wb)
        hist_c = state_conv_c[l].transpose(1, 0, 2).reshape(-1, 3 * wb)
        (q, k, v, za, yb, cbs, qc, kc, vc, zc, gb, ccs) = _proj_sample(
            hs_tm, rep(mod_s[:, :d]), rep(mod_s[:, d:2 * d]), lw, hist_b, hist_c, bs)
        seq_major = lambda a: a.reshape(ts, bs, -1).transpose(1, 0, 2)
        qt = seq_major(q).transpose(0, 2, 1)
        new_t = lambda a: jnp.pad(seq_major(a), ((0, 0), (0, own_w - ts), (0, 0))).transpose(0, 2, 1)
        scores, idx = _sample_scores(qt, cache_kt, page_table, l)
        att_t = _sample_attend(scores, idx[:, :, :MOBA_TOPK].reshape(bs, -1), qt, new_t(k), new_t(v),
                               cache_vt, page_table, l, blast, cfar_s, bown)
        att = att_t[:, :, :ts].transpose(2, 0, 1).reshape(ts * bs, wa)
        pad_t = lambda a: jnp.pad(seq_major(a), ((0, 0), (0, GDN_CHUNK - ts), (0, 0)))
        gbs = seq_major(gb)
        gb_pad = jnp.concatenate(
            [pad_t(gb[:, :hc]), jnp.pad(gbs[:, :, hc:], ((0, 0), (0, GDN_CHUNK - ts), (0, 0)), mode="edge")],
            axis=-1)
        yc, s_new = _gdn(*_gdn_layout(pad_t(qc), pad_t(kc), pad_t(vc), gb_pad, hc), state_delta[l],
                         lw["onw"], 1)
        yc = yc[:, :, :ts, :].transpose(2, 0, 1, 3).reshape(ts * bs, wb)
        hs_tm = _out_proj(hs_tm, rep(mod_s[:, 2 * d:]), att, za, yb, yc, zc, lw["wout"], 1)
        outs[2].append(seq_major(k).reshape(bs, ts, ha, HEAD_DIM))
        outs[3].append(seq_major(v).reshape(bs, ts, ha, HEAD_DIM))
        outs[5].append(cbs.reshape(-1, bs, wb).transpose(1, 0, 2))
        outs[7].append(ccs.reshape(-1, bs, 3 * wb).transpose(1, 0, 2))
        outs[9].append(s_new)

    y_sample = hs_tm.reshape(ts, bs, d).transpose(1, 0, 2)
    return (hp, y_sample) + tuple(jnp.stack(o) for o in outs)
```

```python
import functools
import math

import jax
import jax.numpy as jnp
import numpy as np
from jax import lax
from jax.experimental import pallas as pl
from jax.experimental.pallas import tpu as pltpu

F32 = jnp.float32
BF16 = jnp.bfloat16
HIGHEST = lax.Precision.HIGHEST

HEAD_DIM = 64
MOBA_BLOCK = 256
MOBA_TOPK = 3
GDN_CHUNK = 64
VT_PAD = 16
VT_ROWS = HEAD_DIM + VT_PAD
QK_AHEAD = 4
LOG2E = math.log2(math.e)
INV_BASE = 8
REL_BUCKETS = 32
REL_MAX_DIST = 128
RMS_EPS = 1e-6
ROW_TILE = 256
VMEM_LIMIT = 56 * 1024 * 1024
NEG_INF = float("-inf")

_NT = (((1,), (1,)), ((), ()))


def _cparams(*sem):
    return pltpu.CompilerParams(dimension_semantics=sem, vmem_limit_bytes=VMEM_LIMIT)


def _const_spec(shape):
    zeros = (0,) * len(shape)
    return pl.BlockSpec(shape, lambda *_: zeros)


def _silu(x):
    return x * jax.nn.sigmoid(x)


def _softplus(x):
    return jnp.maximum(x, 0.0) + jnp.log(1.0 + jnp.exp(-jnp.abs(x)))


def _split(a):
    hi = a.astype(BF16)
    lo = (a - hi.astype(F32)).astype(BF16)
    return hi, lo


def _dot(a, b, dims=None):
    if dims is None:
        return jnp.dot(a, b, preferred_element_type=F32)
    return lax.dot_general(a, b, dims, preferred_element_type=F32)


def _mm1(a, b, dims=None):
    return _dot(a.astype(BF16), b.astype(BF16), dims)


def _mm2(a, b, dims=None):
    ah = a.astype(BF16)
    bh, bl = _split(b)
    return _dot(ah, bh, dims) + _dot(ah, bl, dims)


def _mm3(a, b, dims=None):
    ah, al = _split(a)
    bh, bl = _split(b)
    return _dot(ah, bh, dims) + _dot(al, bh, dims) + _dot(ah, bl, dims)


def _group_sum(x):
    lane = lax.broadcasted_iota(jnp.int32, (1, 2 * HEAD_DIM), 1)
    first = lane < HEAD_DIM
    outs = []
    for c in range(x.shape[1] // (2 * HEAD_DIM)):
        xc = x[:, c * 2 * HEAD_DIM:(c + 1) * 2 * HEAD_DIM]
        s0 = jnp.sum(jnp.where(first, xc, 0.0), axis=-1, keepdims=True)
        s1 = jnp.sum(jnp.where(first, 0.0, xc), axis=-1, keepdims=True)
        outs.append(jnp.where(first, s0, s1))
    return jnp.concatenate(outs, axis=1)


def _mod_kernel(c_ref, w_ref, b_ref, o_ref):
    o_ref[0] = jnp.dot(c_ref[...], w_ref[0], precision=HIGHEST, preferred_element_type=F32) + b_ref[0]


def _modulation(c_all, ada_w, ada_b):
    depth, d, n3 = ada_w.shape
    rows = c_all.shape[0]
    tn = 512
    return pl.pallas_call(
        _mod_kernel,
        grid=(depth, n3 // tn),
        in_specs=[
            pl.BlockSpec((rows, d), lambda l, n: (0, 0)),
            pl.BlockSpec((1, d, tn), lambda l, n: (l, 0, n)),
            pl.BlockSpec((1, 1, tn), lambda l, n: (l, 0, n)),
        ],
        out_specs=pl.BlockSpec((1, rows, tn), lambda l, n: (l, 0, n)),
        out_shape=jax.ShapeDtypeStruct((depth, rows, n3), F32),
        compiler_params=_cparams("parallel", "parallel"),
        name="modulation",
    )(c_all, ada_w, ada_b.reshape(depth, 1, n3))


def _bias_kernel(dist_ref, tab_ref, o_ref):
    dist = dist_ref[...]
    n = jnp.maximum(dist, 0)
    max_exact = REL_BUCKETS // 2
    nf = jnp.maximum(n, max_exact).astype(F32)
    large = max_exact + (jnp.log(nf / max_exact) / math.log(REL_MAX_DIST / max_exact)
                         * (REL_BUCKETS - max_exact)).astype(jnp.int32)
    bucket = jnp.where(n < max_exact, n, jnp.minimum(large, REL_BUCKETS - 1))
    tab = tab_ref[...]
    acc = jnp.zeros(dist.shape, F32)
    for b in range(REL_BUCKETS):
        acc = jnp.where(bucket == b, tab[:, b:b + 1], acc)
    o_ref[...] = jnp.where(dist >= 0, acc, NEG_INF)


def _bias_tiles(dist, tab):
    rows, cols = dist.shape
    tr = min(rows, 256)
    return pl.pallas_call(
        _bias_kernel,
        grid=(rows // tr,),
        in_specs=[pl.BlockSpec((tr, cols), lambda i: (i, 0)),
                  pl.BlockSpec((tr, REL_BUCKETS), lambda i: (i, 0))],
        out_specs=pl.BlockSpec((tr, cols), lambda i: (i, 0)),
        out_shape=jax.ShapeDtypeStruct((rows, cols), F32),
        compiler_params=_cparams("parallel"),
        name="bias_tiles",
    )(dist, tab)


def _proj_common(x, shift, scale, normw, w_ref, qw, kw, alog, dtb, cum_ref):
    ms = jnp.mean(x * x, axis=-1, keepdims=True)
    hn = (x * lax.rsqrt(ms + RMS_EPS) * normw) * (1.0 + scale) + shift
    z = jnp.dot(hn.astype(BF16), w_ref[...], preferred_element_type=F32)
    wa = qw.shape[-1]
    wb = wa // 2
    o = 0
    q = z[:, o:o + wa]; o += wa
    k = z[:, o:o + wa]; o += wa
    v = z[:, o:o + wa]; o += wa
    za = z[:, o:o + wa]; o += wa
    hb = z[:, o:o + wb]; o += wb
    bg = z[:, o:o + wb]; o += wb
    cg = z[:, o:o + wb]; o += wb
    zb = z[:, o:o + wb]; o += wb
    qkv = z[:, o:o + 3 * wb]; o += 3 * wb
    zc = z[:, o:o + wb]; o += wb
    z8 = z[:, o:o + 8]

    qn = q * lax.rsqrt(_group_sum(q * q) * (1.0 / HEAD_DIM) + RMS_EPS) * qw
    kn = k * lax.rsqrt(_group_sum(k * k) * (1.0 / HEAD_DIM) + RMS_EPS) * kw

    lane8 = lax.broadcasted_iota(jnp.int32, z8.shape, 1)
    beta = jax.nn.sigmoid(z8)
    g = -jnp.exp(alog) * _softplus(z8 + dtb)
    bg8 = jnp.where(lane8 < 4, beta, g)
    cum = jnp.dot(cum_ref[...], bg8, precision=HIGHEST, preferred_element_type=F32)
    gb = jnp.where(lane8 < 4, bg8, cum)
    return dict(q=qn * (HEAD_DIM ** -0.5), k=kn, v=v, za=_silu(za), u=cg * hb, zbg=_silu(zb) * bg,
                qkv=qkv, zc=_silu(zc), gb=gb)


def _gdn_qkv(y):
    wb = y.shape[-1] // 3
    y = _silu(y)
    qc, kc, vc = y[:, :wb], y[:, wb:2 * wb], y[:, 2 * wb:]
    qc = qc * lax.rsqrt(_group_sum(qc * qc) + RMS_EPS) * (HEAD_DIM ** -0.5)
    kc = kc * lax.rsqrt(_group_sum(kc * kc) + RMS_EPS)
    return qc, kc, vc


def _proj_prompt_kernel(x_ref, mod_ref, normw_ref, w_ref, qw_ref, kw_ref,
                        cbw_ref, ccw_ref, alog_ref, dtb_ref, cum_ref,
                        q_ref, k_ref, v_ref, kb_ref, vt_ref, km_ref, za_ref, yb_ref, cbs_ref,
                        qc_ref, kc_ref, vc_ref, kt_ref, zc_ref, gb_ref, gbr_ref, ccs_ref,
                        carry_b, carry_c):
    ti = pl.program_id(1)
    d = x_ref.shape[-1]
    rows = x_ref.shape[1]

    @pl.when(ti == 0)
    def _():
        carry_b[...] = jnp.zeros_like(carry_b)
        carry_c[...] = jnp.zeros_like(carry_c)

    mod = mod_ref[0]
    p = _proj_common(x_ref[0], mod[:, :d], mod[:, d:2 * d], normw_ref[...], w_ref, qw_ref[...],
                     kw_ref[...], alog_ref[...], dtb_ref[...], cum_ref)
    q_ref[0] = p["q"]
    k_ref[0] = p["k"]
    v_ref[0] = p["v"]
    km_ref[0, 0] = jnp.mean(p["k"], axis=0, keepdims=True)
    za_ref[0] = p["za"].astype(za_ref.dtype)
    zc_ref[0] = p["zc"].astype(zc_ref.dtype)
    gb = p["gb"]
    gb_ref[0] = gb
    kb_ref[0, 0] = p["k"].astype(BF16)
    vtb = p["v"].T.astype(BF16)
    ones = jnp.ones((VT_PAD, rows), BF16)
    pieces = []
    for h in range(vtb.shape[0] // HEAD_DIM):
        pieces += [vtb[h * HEAD_DIM:(h + 1) * HEAD_DIM], ones]
    vt_ref[0, 0] = jnp.concatenate(pieces, axis=0)
    ncol = gb.shape[1]
    eye8 = jnp.where(lax.broadcasted_iota(jnp.int32, (ncol, ncol), 0)
                     == lax.broadcasted_iota(jnp.int32, (ncol, ncol), 1), 1.0, 0.0)
    gbr = lax.dot_general(eye8, gb, _NT, precision=HIGHEST, preferred_element_type=F32)
    for cc in range(rows // GDN_CHUNK):
        gbr_ref[0, cc] = gbr[:, cc * GDN_CHUNK:(cc + 1) * GDN_CHUNK]

    u = p["u"]
    ext = jnp.concatenate([carry_b[...], u], axis=0)
    cbw = cbw_ref[...]
    nb = cbw.shape[0]
    yb = u * cbw[nb - 1:nb]
    for j in range(nb - 1):
        s = nb - 1 - j
        yb = yb + ext[8 - s:8 - s + rows] * cbw[j:j + 1]
    yb_ref[0] = (p["zbg"] * yb).astype(yb_ref.dtype)
    carry_b[...] = u[rows - 8:]
    cbs_ref[0] = u[rows - (nb - 1):]

    qkv = p["qkv"]
    extc = jnp.concatenate([carry_c[...], qkv], axis=0)
    ccw = ccw_ref[...]
    nc = ccw.shape[0]
    yc = qkv * ccw[nc - 1:nc]
    for j in range(nc - 1):
        s = nc - 1 - j
        yc = yc + extc[8 - s:8 - s + rows] * ccw[j:j + 1]
    carry_c[...] = qkv[rows - 8:]
    ccs_ref[0] = qkv[rows - (nc - 1):]
    qc, kc, vc = _gdn_qkv(yc)
    kct = kc.T
    for h in range(qc.shape[1] // HEAD_DIM):
        hs = slice(h * HEAD_DIM, (h + 1) * HEAD_DIM)
        qc_ref[0, h] = qc[:, hs]
        kc_ref[0, h] = kc[:, hs]
        vc_ref[0, h] = vc[:, hs]
        for cc in range(rows // GDN_CHUNK):
            kt_ref[0, h, cc] = kct[hs, cc * GDN_CHUNK:(cc + 1) * GDN_CHUNK]


def _proj_sample_kernel(x_ref, shift_ref, scale_ref, normw_ref, w_ref, qw_ref, kw_ref,
                        cbw_ref, ccw_ref, alog_ref, dtb_ref, cum_ref, hb_ref, hc_ref,
                        q_ref, k_ref, v_ref, za_ref, yb_ref, cbs_ref,
                        qc_ref, kc_ref, vc_ref, zc_ref, gb_ref, ccs_ref, *, nseq):
    rows = x_ref.shape[0]
    p = _proj_common(x_ref[...], shift_ref[...], scale_ref[...], normw_ref[...], w_ref, qw_ref[...],
                     kw_ref[...], alog_ref[...], dtb_ref[...], cum_ref)
    q_ref[...] = p["q"]
    k_ref[...] = p["k"]
    v_ref[...] = p["v"]
    za_ref[...] = p["za"]
    zc_ref[...] = p["zc"]
    gb_ref[...] = p["gb"]

    cbw = cbw_ref[...]
    nb = cbw.shape[0]
    ext = jnp.concatenate([hb_ref[...], p["u"]], axis=0)
    yb = ext[0:rows] * cbw[0:1]
    for j in range(1, nb):
        yb = yb + ext[j * nseq:j * nseq + rows] * cbw[j:j + 1]
    yb_ref[...] = p["zbg"] * yb
    cbs_ref[...] = ext[rows:]

    ccw = ccw_ref[...]
    nc = ccw.shape[0]
    extc = jnp.concatenate([hc_ref[...], p["qkv"]], axis=0)
    yc = extc[0:rows] * ccw[0:1]
    for j in range(1, nc):
        yc = yc + extc[j * nseq:j * nseq + rows] * ccw[j:j + 1]
    ccs_ref[...] = extc[rows:]
    qc, kc, vc = _gdn_qkv(yc)
    qc_ref[...] = qc
    kc_ref[...] = kc
    vc_ref[...] = vc


def _proj_prompt(x, mod, lw):
    b, t, d = x.shape
    wcat = lw["wcat"]
    ncat = wcat.shape[1]
    wa, wb = d // 2, d // 4
    nt = t // ROW_TILE
    row = lambda w: pl.BlockSpec((1, ROW_TILE, w), lambda bi, ti: (bi, ti, 0))
    per_b = lambda r, w: pl.BlockSpec((1, r, w), lambda bi, ti: (bi, 0, 0))
    nb, nc = lw["conv_b_w"].shape[0], lw["conv_c_w"].shape[0]
    hc = wb // HEAD_DIM
    cpt = ROW_TILE // GDN_CHUNK
    nch = t // GDN_CHUNK
    out_shape = [
        jax.ShapeDtypeStruct((b, t, wa), F32),
        jax.ShapeDtypeStruct((b, t, wa), F32),
        jax.ShapeDtypeStruct((b, t, wa), F32),
        jax.ShapeDtypeStruct((b, nt, ROW_TILE, wa), BF16),
        jax.ShapeDtypeStruct((b, nt, wa // HEAD_DIM * VT_ROWS, ROW_TILE), BF16),
        jax.ShapeDtypeStruct((b, nt, 1, wa), F32),
        jax.ShapeDtypeStruct((b, t, wa), BF16),
        jax.ShapeDtypeStruct((b, t, wb), BF16),
        jax.ShapeDtypeStruct((b, nb - 1, wb), F32),
        jax.ShapeDtypeStruct((b, hc, t, HEAD_DIM), F32),
        jax.ShapeDtypeStruct((b, hc, t, HEAD_DIM), F32),
        jax.ShapeDtypeStruct((b, hc, t, HEAD_DIM), F32),
        jax.ShapeDtypeStruct((b, hc, nch, HEAD_DIM, GDN_CHUNK), F32),
        jax.ShapeDtypeStruct((b, t, wb), BF16),
        jax.ShapeDtypeStruct((b, t, 8), F32),
        jax.ShapeDtypeStruct((b, nch, 8, GDN_CHUNK), F32),
        jax.ShapeDtypeStruct((b, nc - 1, 3 * wb), F32),
    ]
    blk4 = lambda r, c: pl.BlockSpec((1, 1, r, c), lambda bi, ti: (bi, ti, 0, 0))
    heads = pl.BlockSpec((1, hc, ROW_TILE, HEAD_DIM), lambda bi, ti: (bi, 0, ti, 0))
    out_specs = [
        row(wa), row(wa), row(wa), blk4(ROW_TILE, wa), blk4(wa // HEAD_DIM * VT_ROWS, ROW_TILE), blk4(1, wa),
        row(wa), row(wb), per_b(nb - 1, wb),
        heads, heads, heads,
        pl.BlockSpec((1, hc, cpt, HEAD_DIM, GDN_CHUNK), lambda bi, ti: (bi, 0, ti, 0, 0)),
        row(wb), row(8),
        pl.BlockSpec((1, cpt, 8, GDN_CHUNK), lambda bi, ti: (bi, ti, 0, 0)),
        per_b(nc - 1, 3 * wb),
    ]
    in_specs = [
        row(d),
        pl.BlockSpec((1, 1, 3 * d), lambda bi, ti: (bi, 0, 0)),
        _const_spec((1, d)),
        _const_spec((d, ncat)),
        _const_spec((1, wa)), _const_spec((1, wa)),
        _const_spec((nb, wb)), _const_spec((nc, 3 * wb)),
        _const_spec((1, 8)), _const_spec((1, 8)),
        _const_spec((ROW_TILE, ROW_TILE)),
    ]
    return pl.pallas_call(
        _proj_prompt_kernel,
        grid=(b, nt),
        in_specs=in_specs,
        out_specs=out_specs,
        out_shape=out_shape,
        scratch_shapes=[pltpu.VMEM((8, wb), F32), pltpu.VMEM((8, 3 * wb), F32)],
        compiler_params=_cparams("parallel", "arbitrary"),
        name="proj_prompt",
    )(x, mod.reshape(b, 1, 3 * d), lw["norm_w"], wcat, lw["qw"], lw["kw"],
      lw["conv_b_w"], lw["conv_c_w"], lw["alog8"], lw["dtb8"], lw["cum_prompt"])


def _proj_sample(x_tm, shift, scale, lw, hist_b, hist_c, nseq):
    rows, d = x_tm.shape
    wa, wb = d // 2, d // 4
    nb, nc = lw["conv_b_w"].shape[0], lw["conv_c_w"].shape[0]
    out_shape = [
        jax.ShapeDtypeStruct((rows, wa), F32), jax.ShapeDtypeStruct((rows, wa), F32),
        jax.ShapeDtypeStruct((rows, wa), F32), jax.ShapeDtypeStruct((rows, wa), F32),
        jax.ShapeDtypeStruct((rows, wb), F32),
        jax.ShapeDtypeStruct(((nb - 1) * nseq, wb), F32),
        jax.ShapeDtypeStruct((rows, wb), F32), jax.ShapeDtypeStruct((rows, wb), F32),
        jax.ShapeDtypeStruct((rows, wb), F32), jax.ShapeDtypeStruct((rows, wb), F32),
        jax.ShapeDtypeStruct((rows, 8), F32),
        jax.ShapeDtypeStruct(((nc - 1) * nseq, 3 * wb), F32),
    ]
    return pl.pallas_call(
        functools.partial(_proj_sample_kernel, nseq=nseq),
        out_shape=out_shape,
        compiler_params=pltpu.CompilerParams(vmem_limit_bytes=VMEM_LIMIT),
        name="proj_sample",
    )(x_tm, shift, scale, lw["norm_w"], lw["wcat"], lw["qw"], lw["kw"],
      lw["conv_b_w"], lw["conv_c_w"], lw["alog8"], lw["dtb8"], lw["cum_sample"], hist_b, hist_c)


def _attn_prompt_kernel(cfar_ref, q_ref, kb_ref, vt_ref, km_ref, bt_ref, o_ref,
                        sel_s, acc_s, *, nheads, nblk):
    i = pl.program_id(1)
    blk = q_ref.shape[1]
    qt = q_ref[0].T * LOG2E
    qtb = qt.astype(BF16)
    lane = lax.broadcasted_iota(jnp.int32, (1, 2 * HEAD_DIM), 1)
    sub = lax.broadcasted_iota(jnp.int32, (2 * HEAD_DIM, 1), 0)
    lane_half = [lane < HEAD_DIM, lane >= HEAD_DIM]
    sub_half = [sub < HEAD_DIM, sub >= HEAD_DIM]

    km = km_ref[0]
    n = lax.broadcasted_iota(jnp.int32, (nblk, blk), 0)
    n_f = n.astype(F32)
    past = n < i
    for h in range(nheads):
        p, par = divmod(h, 2)
        kmp = jnp.where(lane_half[par], km[:, p * 128:(p + 1) * 128], 0.0)
        gate = jnp.dot(kmp, qt[p * 128:(p + 1) * 128, :], precision=HIGHEST, preferred_element_type=F32)
        gate = jnp.where(past, gate, NEG_INF)
        picked = jnp.zeros((nblk, blk), F32)
        for _ in range(MOBA_TOPK):
            top = jnp.max(gate, axis=0, keepdims=True)
            first = jnp.min(jnp.where(gate == top, n_f, float(nblk)), axis=0, keepdims=True)
            chosen = n_f == first
            picked = jnp.where(chosen, 1.0, picked)
            gate = jnp.where(chosen, NEG_INF, gate)
        sel_s[h * nblk:(h + 1) * nblk, :] = jnp.where(past, picked, 0.0)

    qm = []
    for h in range(nheads):
        p, par = divmod(h, 2)
        qm.append(jnp.where(sub_half[par], qtb[p * 128:(p + 1) * 128, :], jnp.zeros((), BF16)))

    def scores(kblk, h):
        p = h // 2
        return jnp.dot(kblk[:, p * 128:(p + 1) * 128], qm[h], preferred_element_type=F32)

    def pv_dot(vblk, h, pr):
        res = jnp.dot(vblk[h * VT_ROWS:(h + 1) * VT_ROWS, :], pr.astype(BF16), preferred_element_type=F32)
        return res[:HEAD_DIM], res[HEAD_DIM:HEAD_DIM + 1]


    def block(js, kind, m_all, l_all):
        nb = len(js)
        kblk = kb_ref[0, js[0]] if nb == 1 else jnp.concatenate([kb_ref[0, j] for j in js], axis=0)
        sts = [scores(kblk, h) for h in range(min(QK_AHEAD, nheads))]
        m_rows, l_rows = [], []
        pend = None

        def finish(h, alpha, ons, pvs):
            rs = slice(h * HEAD_DIM, (h + 1) * HEAD_DIM)
            if kind == "own":
                acc_s[rs, :] = pvs[0]
            else:
                new = alpha * acc_s[rs, :]
                for on, pv in zip(ons, pvs):
                    new = new + jnp.where(on, pv, 0.0)
                acc_s[rs, :] = new

        for h in range(nheads):
            st = sts[h]
            if h + QK_AHEAD < nheads:
                sts.append(scores(kblk, h + QK_AHEAD))
            if kind == "own":
                st = bt_ref[0, h] * LOG2E + st
                m_new = jnp.max(st, axis=0, keepdims=True)
                pv, l_new = pv_dot(vt_ref[0, js[0]], h, jnp.exp2(st - m_new))
                alpha, ons, pvs = None, None, [pv]
            else:
                if kind == "near":
                    st = bt_ref[1, h] * LOG2E + st
                    shift = 0.0
                else:
                    shift = cfar_ref[h] * LOG2E
                parts = [st[s * blk:(s + 1) * blk] for s in range(nb)]
                ons = [sel_s[pl.ds(h * nblk + j, 1), :] > 0.0 for j in js]
                m_old = m_all[h:h + 1, :]
                m_new = m_old
                for on, part in zip(ons, parts):
                    m_new = jnp.maximum(m_new, jnp.where(on, jnp.max(part, axis=0, keepdims=True) + shift, NEG_INF))
                alpha = jnp.exp2(m_old - m_new)
                l_new = alpha * l_all[h:h + 1, :]
                pvs = []
                for j, on, part in zip(js, ons, parts):
                    pv, psum = pv_dot(vt_ref[0, j], h, jnp.exp2(part - (m_new - shift)))
                    pvs.append(pv)
                    l_new = l_new + jnp.where(on, psum, 0.0)
            m_rows.append(m_new)
            l_rows.append(l_new)
            if pend is not None:
                finish(*pend)
            pend = (h, alpha, ons, pvs)
        finish(*pend)
        return jnp.concatenate(m_rows, axis=0), jnp.concatenate(l_rows, axis=0)

    keep = lambda m, l: (m, l)
    m_all, l_all = block([i], "own", None, None)
    m_all, l_all = lax.cond(i >= 1, lambda m, l: block([i - 1], "near", m, l), keep, m_all, l_all)
    nfar = i - 1
    odd = (nfar >= 1) & ((nfar & 1) == 1)
    m_all, l_all = lax.cond(odd, lambda m, l: block([nfar - 1], "far", m, l), keep, m_all, l_all)
    m_all, l_all = lax.fori_loop(0, nfar // 2, lambda jj, c: block([2 * jj, 2 * jj + 1], "far", *c), (m_all, l_all))

    for h in range(nheads):
        rs = slice(h * HEAD_DIM, (h + 1) * HEAD_DIM)
        acc_s[rs, :] = acc_s[rs, :] / l_all[h:h + 1, :]
    o_ref[0] = acc_s[...].T.astype(o_ref.dtype)


def _attn_prompt(q, kb, vt, kmean, bt, cfar):
    b, t, wa = q.shape
    nheads = wa // HEAD_DIM
    nblk = t // MOBA_BLOCK
    grid_spec = pltpu.PrefetchScalarGridSpec(
        num_scalar_prefetch=0,
        grid=(b, nblk),
        in_specs=[
            pl.BlockSpec(memory_space=pltpu.SMEM),
            pl.BlockSpec((1, MOBA_BLOCK, wa), lambda bi, i: (bi, i, 0)),
            pl.BlockSpec((1, nblk, MOBA_BLOCK, wa), lambda bi, i: (bi, 0, 0, 0)),
            pl.BlockSpec((1, nblk, nheads * VT_ROWS, MOBA_BLOCK), lambda bi, i: (bi, 0, 0, 0)),
            pl.BlockSpec((1, nblk, wa), lambda bi, i: (bi, 0, 0)),
            pl.BlockSpec((2, nheads, MOBA_BLOCK, MOBA_BLOCK), lambda bi, i: (0, 0, 0, 0)),
        ],
        out_specs=pl.BlockSpec((1, MOBA_BLOCK, wa), lambda bi, i: (bi, i, 0)),
        scratch_shapes=[
            pltpu.VMEM((nheads * nblk, MOBA_BLOCK), F32),
            pltpu.VMEM((wa, MOBA_BLOCK), F32),
        ],
    )
    return pl.pallas_call(
        functools.partial(_attn_prompt_kernel, nheads=nheads, nblk=nblk),
        grid_spec=grid_spec,
        out_shape=jax.ShapeDtypeStruct((b, t, wa), BF16),
        compiler_params=_cparams("parallel", "arbitrary"),
        name="attn_prompt",
    )(cfar, q, kb, vt, kmean, bt)


SAMPLE_PAGES_PER_STEP = 16


def _sample_scores_kernel(pt_ref, qt_ref, *rest, nheads, nblk, page, tq):
    npg = SAMPLE_PAGES_PER_STEP
    k_refs = rest[:npg]
    s_ref, idx_ref, qb_s, gate_s = rest[npg:]
    s = pl.program_id(1)
    nrows = tq * nheads
    ppb = MOBA_BLOCK // page

    @pl.when(s == 0)
    def _():
        qt = qt_ref[0]
        for t in range(tq):
            for h in range(nheads):
                qb_s[t * nheads + h] = jnp.broadcast_to(qt[h * HEAD_DIM:(h + 1) * HEAD_DIM, t:t + 1],
                                                        (HEAD_DIM, page))
        gate_s[...] = jnp.full(gate_s.shape, NEG_INF, F32)

    lane = lax.broadcasted_iota(jnp.int32, gate_s.shape, 1)
    bsum = None
    for e in range(npg):
        groups = []
        for t in range(tq):
            rows = [jnp.sum(k_refs[e][h] * qb_s[t * nheads + h], axis=0, keepdims=True) for h in range(nheads)]
            groups.append(jnp.concatenate(rows, axis=0))
        sp = jnp.concatenate(groups, axis=0)
        s_ref[0, e // ppb, :, (e % ppb) * page:(e % ppb + 1) * page] = sp
        rs = jnp.sum(sp, axis=1, keepdims=True)
        bsum = rs if e % ppb == 0 else bsum + rs
        if e % ppb == ppb - 1:
            n = s * (npg // ppb) + e // ppb
            gate_s[...] = jnp.where(lane == n, bsum * (1.0 / MOBA_BLOCK), gate_s[...])

    @pl.when(s == pl.num_programs(1) - 1)
    def _():
        gate = gate_s[...]
        rank = jnp.zeros(gate.shape, F32)
        for sh in range(1, gate.shape[1]):
            other = pltpu.roll(gate, sh, 1)
            rank = rank + jnp.where(other > gate, 1.0, 0.0)
            rank = rank + jnp.where((other == gate) & (lane >= sh), 1.0, 0.0)
        lane_f = lane.astype(F32)
        out = jnp.zeros(gate.shape, F32)
        for j in range(MOBA_TOPK):
            pick = jnp.sum(jnp.where((rank == j) & (lane < nblk), lane_f, 0.0), axis=1, keepdims=True)
            out = jnp.where(lane == j, pick, out)
        idx_ref[0] = out.astype(jnp.int32)


def _sample_scores(qt, cache_kt, page_table, layer):
    nseq, wa, tq = qt.shape
    nheads = wa // HEAD_DIM
    page = cache_kt.shape[-1]
    n_pages = page_table.shape[1]
    nblk = n_pages * page // MOBA_BLOCK
    npg = SAMPLE_PAGES_PER_STEP
    nrows = tq * nheads

    def page_spec(e):
        return pl.BlockSpec((None, None, nheads, HEAD_DIM, page),
                            lambda b, s, pt: (layer, pt[b, s * npg + e], 0, 0, 0))

    grid_spec = pltpu.PrefetchScalarGridSpec(
        num_scalar_prefetch=1,
        grid=(nseq, n_pages // npg),
        in_specs=[pl.BlockSpec((1, wa, tq), lambda b, s, pt: (b, 0, 0))] + [page_spec(e) for e in range(npg)],
        out_specs=[pl.BlockSpec((1, npg * page // MOBA_BLOCK, nrows, MOBA_BLOCK), lambda b, s, pt: (b, s, 0, 0)),
                   pl.BlockSpec((1, nrows, 128), lambda b, s, pt: (b, 0, 0))],
        scratch_shapes=[pltpu.VMEM((nrows, HEAD_DIM, page), F32), pltpu.VMEM((nrows, 128), F32)],
    )
    return pl.pallas_call(
        functools.partial(_sample_scores_kernel, nheads=nheads, nblk=nblk, page=page, tq=tq),
        grid_spec=grid_spec,
        out_shape=[jax.ShapeDtypeStruct((nseq, nblk, nrows, MOBA_BLOCK), F32),
                   jax.ShapeDtypeStruct((nseq, nrows, 128), jnp.int32)],
        compiler_params=_cparams("parallel", "arbitrary"),
        name="sample_scores",
    )(page_table, qt, *([cache_kt] * npg))


def _sample_attend_kernel(pt_ref, idx_ref, s_ref, qt_ref, knt_ref, vnt_ref, blast_ref, cfar_ref, bown_ref,
                          cv_ref, o_ref, vbuf, sems, *, nheads, nblk, page, tq, layer):
    b = pl.program_id(0)
    nrows = tq * nheads
    ppb = MOBA_BLOCK // page

    def slab_copies(p):
        h = p % nheads
        out = []
        for j in range(MOBA_TOPK):
            n = idx_ref[b, p * MOBA_TOPK + j]
            for e in range(ppb):
                pg = pt_ref[b, n * ppb + e]
                out.append(pltpu.make_async_copy(cv_ref.at[layer, pg, h], vbuf.at[p, j * ppb + e], sems.at[p]))
        return out

    copies = [slab_copies(p) for p in range(nrows)]
    for row_copies in copies:
        for cp in row_copies:
            cp.start()

    qt = qt_ref[0]
    own, past = [], [[] for _ in range(MOBA_TOPK)]
    for p in range(nrows):
        t, h = divmod(p, nheads)
        hs = slice(h * HEAD_DIM, (h + 1) * HEAD_DIM)
        qb = jnp.broadcast_to(qt[hs, t:t + 1], (HEAD_DIM, knt_ref.shape[2]))
        own.append(jnp.sum(knt_ref[0, hs, :] * qb, axis=0, keepdims=True) + bown_ref[p:p + 1, :])
        for j in range(MOBA_TOPK):
            n = idx_ref[b, p * MOBA_TOPK + j]
            bias = jnp.where(n == nblk - 1, blast_ref[p:p + 1, :], cfar_ref[p:p + 1, :])
            past[j].append(s_ref[0, n, p:p + 1, :] + bias)
    l_own = jnp.concatenate(own, axis=0)
    l_past = [jnp.concatenate(rows, axis=0) for rows in past]
    m = jnp.max(l_own, axis=1, keepdims=True)
    for lg in l_past:
        m = jnp.maximum(m, jnp.max(lg, axis=1, keepdims=True))
    w_own = jnp.exp(l_own - m)
    w_past = [jnp.exp(lg - m) for lg in l_past]
    denom = jnp.sum(w_own, axis=1, keepdims=True)
    for w in w_past:
        denom = denom + jnp.sum(w, axis=1, keepdims=True)
    inv = 1.0 / denom
    w_own = w_own * inv
    w_past = [w * inv for w in w_past]

    for row_copies in copies:
        for cp in row_copies:
            cp.wait()

    o_ref[0] = jnp.zeros(o_ref.shape[1:], F32)
    for p in range(nrows):
        t, h = divmod(p, nheads)
        hs = slice(h * HEAD_DIM, (h + 1) * HEAD_DIM)
        acc = w_own[p:p + 1, :] * vnt_ref[0, hs, :]
        for j in range(MOBA_TOPK):
            for e in range(ppb):
                acc = acc + w_past[j][p:p + 1, e * page:(e + 1) * page] * vbuf[p, j * ppb + e]
        o_ref[0, hs, t:t + 1] = jnp.sum(acc, axis=1, keepdims=True)


def _sample_attend(scores, idx, qt, knt, vnt, cache_vt, page_table, layer, blast, cfar, bown):
    nseq, nblk, nrows, _ = scores.shape
    wa, tq = qt.shape[1], qt.shape[2]
    nheads = wa // HEAD_DIM
    page = cache_vt.shape[-1]
    assert knt.shape[2] == page and bown.shape[1] == page
    seq = lambda shape: pl.BlockSpec((1,) + shape, lambda b, pt, ix: (b,) + (0,) * len(shape))
    whole = lambda shape: pl.BlockSpec(shape, lambda b, pt, ix: (0,) * len(shape))
    grid_spec = pltpu.PrefetchScalarGridSpec(
        num_scalar_prefetch=2,
        grid=(nseq,),
        in_specs=[seq((nblk, nrows, MOBA_BLOCK)), seq((wa, tq)), seq((wa, page)), seq((wa, page)),
                  whole(blast.shape), whole(cfar.shape), whole(bown.shape),
                  pl.BlockSpec(memory_space=pl.ANY)],
        out_specs=seq((wa, 128)),
        scratch_shapes=[pltpu.VMEM((nrows, MOBA_TOPK * (MOBA_BLOCK // page), HEAD_DIM, page), F32),
                        pltpu.SemaphoreType.DMA((nrows,))],
    )
    return pl.pallas_call(
        functools.partial(_sample_attend_kernel, nheads=nheads, nblk=nblk, page=page, tq=tq, layer=layer),
        grid_spec=grid_spec,
        out_shape=jax.ShapeDtypeStruct((nseq, wa, 128), F32),
        compiler_params=_cparams("arbitrary"),
        name="sample_attend",
    )(page_table, idx, scores, qt, knt, vnt, blast, cfar, bown, cache_vt)


def _gdn_kernel(q_ref, k_ref, v_ref, kt_ref, gbc_ref, gbr_ref, s0_ref, onw_ref, o_ref, sout_ref,
                s_s, lhs_s, add_s, *, nheads, nchunk, chunks_per_iter):
    tc = pl.program_id(1)
    c_len = GDN_CHUNK
    hd = HEAD_DIM

    @pl.when(tc == 0)
    def _():
        s_s[...] = s0_ref[0]

    row = lax.broadcasted_iota(jnp.int32, (c_len, c_len), 0)
    col = lax.broadcasted_iota(jnp.int32, (c_len, c_len), 1)
    tri = row >= col
    tri_s = row > col
    eye = row == col
    eye_f = jnp.where(eye, 1.0, 0.0)
    blk = lambda size: (row // size) == (col // size)

    def chunk_a(it, carry):
        chains = [(it * chunks_per_iter + cc, h) for cc in range(chunks_per_iter) for h in range(nheads)]
        st = []
        for c, h in chains:
            gbc = gbc_ref[0, pl.ds(c * c_len, c_len), :]
            gbr = gbr_ref[0, c]
            q = q_ref[0, h, pl.ds(c * c_len, c_len), :]
            k = k_ref[0, h, pl.ds(c * c_len, c_len), :]
            v = v_ref[0, h, pl.ds(c * c_len, c_len), :]
            bcol = gbc[:, h:h + 1]
            gcol = gbc[:, nheads + h:nheads + h + 1]
            grow = gbr[nheads + h:nheads + h + 1, :]
            glast = gcol[c_len - 1:c_len, :]
            st.append(dict(c=c, h=h, q=q, k=k, v=v, bcol=bcol, grow=grow, glast=glast,
                           decay=jnp.where(tri, jnp.exp(gcol - grow), 0.0), egc=jnp.exp(gcol)))
        for d in st:
            d["kq"] = _mm1(jnp.concatenate([d["k"], d["q"]], axis=0), d["k"], _NT)
        for d in st:
            kk, qk = d["kq"][:c_len], d["kq"][c_len:]
            d["n"] = jnp.where(tri_s, d["bcol"] * kk * d["decay"], 0.0)
            d["qkd"] = jnp.where(tri, qk * d["decay"], 0.0)
            d["x"] = jnp.concatenate([d["v"] * d["bcol"], d["k"] * (d["bcol"] * d["egc"])], axis=1)
            d["a"] = -jnp.where(blk(INV_BASE), d["n"], 0.0)
        for d in st:
            d["p2"] = _mm2(d["a"], d["a"])
        for d in st:
            d["res"] = _mm2(jnp.concatenate([d["p2"], eye_f + d["a"]], axis=0), d["p2"])
        for d in st:
            d["p4"] = d["res"][:c_len]
            d["t"] = eye_f + d["a"] + d["res"][c_len:]
        for d in st:
            d["res"] = _mm2(d["t"], d["p4"])
        for d in st:
            d["t"] = d["t"] + d["res"]
        size = 2 * INV_BASE
        while size <= c_len:
            off = blk(size) & jnp.logical_not(blk(size // 2))
            for d in st:
                d["y"] = _mm1(jnp.where(off, d["n"], 0.0), d["t"])
            for d in st:
                d["res"] = _mm1(d["t"], d["y"])
            for d in st:
                d["t"] = d["t"] - d["res"]
            size *= 2
        for d in st:
            d["res"] = _mm2(d["t"], d["x"])
        for d in st:
            d["x"] = d["res"]
        for d in st:
            kdt = kt_ref[0, d["h"], d["c"]] * jnp.exp(d["glast"] - d["grow"])
            d["res"] = _mm2(jnp.concatenate([d["qkd"], kdt], axis=0), d["x"])
        for d in st:
            top = jnp.concatenate([d["q"] * d["egc"], jnp.where(eye, jnp.exp(d["glast"]), 0.0)], axis=0)
            lhs_s[d["c"], d["h"]] = top - d["res"][:, hd:]
            add_s[d["c"], d["h"]] = d["res"][:, :hd]
        return carry

    lax.fori_loop(0, nchunk // chunks_per_iter, chunk_a, 0)

    onw = onw_ref[...]

    state = [s_s[h] for h in range(nheads)]
    for c in range(nchunk):
        res = [_mm3(lhs_s[c, h], state[h]) + add_s[c, h] for h in range(nheads)]
        for h in range(nheads):
            o = res[h][:c_len]
            state[h] = res[h][c_len:]
            o = o * lax.rsqrt(jnp.mean(o * o, axis=-1, keepdims=True) + RMS_EPS) * onw
            o_ref[0, h, c * c_len:(c + 1) * c_len, :] = o
    for h in range(nheads):
        s_s[h] = state[h]
    sout_ref[0] = s_s[...]


def _gdn(qh, kh, vh, kt, gbc, gbr, s0, onw, nchunk):
    b, nheads, t, hd = qh.shape
    tb = nchunk * GDN_CHUNK
    nt = t // tb
    head = pl.BlockSpec((1, nheads, tb, hd), lambda bi, ti: (bi, 0, ti, 0))
    state = pl.BlockSpec((1, nheads, hd, hd), lambda bi, ti: (bi, 0, 0, 0))
    return pl.pallas_call(
        functools.partial(_gdn_kernel, nheads=nheads, nchunk=nchunk, chunks_per_iter=min(4, nchunk)),
        grid=(b, nt),
        in_specs=[head, head, head,
                  pl.BlockSpec((1, nheads, nchunk, hd, GDN_CHUNK), lambda bi, ti: (bi, 0, ti, 0, 0)),
                  pl.BlockSpec((1, tb, 8), lambda bi, ti: (bi, ti, 0)),
                  pl.BlockSpec((1, nchunk, 8, GDN_CHUNK), lambda bi, ti: (bi, ti, 0, 0)),
                  state, _const_spec((1, hd))],
        out_specs=[head, state],
        out_shape=[jax.ShapeDtypeStruct((b, nheads, t, hd), F32),
                   jax.ShapeDtypeStruct((b, nheads, hd, hd), F32)],
        scratch_shapes=[
            pltpu.VMEM((nheads, hd, hd), F32),
            pltpu.VMEM((nchunk, nheads, GDN_CHUNK + hd, hd), F32),
            pltpu.VMEM((nchunk, nheads, GDN_CHUNK + hd, hd), F32),
        ],
        compiler_params=_cparams("parallel", "arbitrary"),
        name="deltanet",
    )(qh, kh, vh, kt, gbc, gbr, s0, onw)


def _out_kernel(x_ref, gate_ref, att_ref, za_ref, yb_ref, yc_ref, zc_ref, w_ref, o_ref):
    if len(yc_ref.shape) == 3:
        yc = jnp.concatenate([yc_ref[h] for h in range(yc_ref.shape[0])], axis=-1)
    else:
        yc = yc_ref[...]
    f32 = lambda ref: ref[...].astype(F32)
    cat = jnp.concatenate([(f32(za_ref) * f32(att_ref)).astype(BF16), yb_ref[...].astype(BF16),
                           (yc * f32(zc_ref)).astype(BF16)], axis=-1)
    y = jnp.dot(cat, w_ref[...], preferred_element_type=F32)
    o_ref[...] = x_ref[...] + gate_ref[...] * y


def _out_proj(x2, gate2, att2, za2, yb2, yc, zc2, wout, rows_per_gate):
    rows, d = x2.shape
    tr = min(ROW_TILE, rows)
    wa, wb = att2.shape[1], yb2.shape[1]
    row = lambda w: pl.BlockSpec((tr, w), lambda i: (i, 0))
    if rows_per_gate == 1:
        gate_spec = row(d)
    else:
        per = rows_per_gate // tr
        gate2 = gate2.reshape(gate2.shape[0], 1, d)
        gate_spec = pl.BlockSpec((None, 1, d), lambda i: (i // per, 0, 0))
    if yc.ndim == 4:
        per_seq = yc.shape[2] // tr
        yc_spec = pl.BlockSpec((None, yc.shape[1], tr, yc.shape[3]), lambda i: (i // per_seq, 0, i % per_seq, 0))
    else:
        yc_spec = row(wb)
    return pl.pallas_call(
        _out_kernel,
        grid=(rows // tr,),
        in_specs=[row(d), gate_spec, row(wa), row(wa), row(wb), yc_spec, row(wb), _const_spec((d, d))],
        out_specs=row(d),
        out_shape=jax.ShapeDtypeStruct((rows, d), F32),
        compiler_params=_cparams("parallel"),
        name="out_proj",
    )(x2, gate2, att2, za2, yb2, yc, zc2, wout)


def _layer_weights(l, norm_w, w_in, q_norm_w, k_norm_w, conv_b_w, conv_c_w, a_log, dt_bias, o_norm_w,
                   w_out, nseq, ts):
    d = w_in.shape[1]
    wa, wb = d // 2, d // 4
    nheads_a = wa // HEAD_DIM
    hc = wb // HEAD_DIM
    n_main = 4 * wa + 4 * wb + 4 * wb
    w = w_in[l]
    wcat = jnp.concatenate([w[:, :n_main], jnp.pad(w[:, n_main:], ((0, 0), (0, 128 - 2 * hc)))], axis=1)
    r = np.arange(ROW_TILE)
    cum_prompt = ((r[:, None] // GDN_CHUNK == r[None, :] // GDN_CHUNK) & (r[:, None] >= r[None, :]))
    rs = np.arange(nseq * ts)
    cum_sample = ((rs[:, None] % nseq == rs[None, :] % nseq) & (rs[:, None] >= rs[None, :]))
    return dict(
        norm_w=norm_w[l].reshape(1, d),
        wcat=wcat.astype(BF16),
        qw=jnp.tile(q_norm_w[l], nheads_a).reshape(1, wa),
        kw=jnp.tile(k_norm_w[l], nheads_a).reshape(1, wa),
        conv_b_w=conv_b_w[l], conv_c_w=conv_c_w[l],
        alog8=jnp.concatenate([jnp.zeros((hc,), F32), a_log[l]]).reshape(1, 2 * hc),
        dtb8=jnp.concatenate([jnp.zeros((hc,), F32), dt_bias[l]]).reshape(1, 2 * hc),
        cum_prompt=jnp.asarray(cum_prompt, F32),
        cum_sample=jnp.asarray(cum_sample, F32),
        onw=o_norm_w[l].reshape(1, HEAD_DIM),
        wout=w_out[l].astype(BF16),
    )


def _heads(a, nheads):
    b, t, _ = a.shape
    return a.reshape(b, t, nheads, HEAD_DIM).transpose(0, 2, 1, 3)


def _gdn_layout(qc, kc, vc, gb, hc):
    b, t, _ = qc.shape
    nch = t // GDN_CHUNK
    kt = kc.reshape(b, nch, GDN_CHUNK, hc, HEAD_DIM).transpose(0, 3, 1, 4, 2)
    gbr = gb.reshape(b, nch, GDN_CHUNK, 2 * hc).transpose(0, 1, 3, 2)
    return _heads(qc, hc), _heads(kc, hc), _heads(vc, hc), kt, gb, gbr


def kernel(x_prompt, x_sample, c_prompt, c_sample, cache_k, cache_v, page_table, state_conv_b,
           state_conv_c, state_delta, norm_w, ada_w, ada_b, w_in, q_norm_w, k_norm_w, rel_bias,
           conv_b_w, conv_c_w, a_log, dt_bias, o_norm_w, w_out):
    bp, tp, d = x_prompt.shape
    bs, ts, _ = x_sample.shape
    depth = ada_w.shape[0]
    wa, wb = d // 2, d // 4
    ha, hc = wa // HEAD_DIM, wb // HEAD_DIM
    page = cache_k.shape[2]
    past_len = page_table.shape[1] * page
    nblk_p = tp // MOBA_BLOCK
    nblk_s = past_len // MOBA_BLOCK
    assert tp % ROW_TILE == 0 and ROW_TILE == MOBA_BLOCK and past_len % MOBA_BLOCK == 0
    assert MOBA_BLOCK % page == 0 and page_table.shape[1] % SAMPLE_PAGES_PER_STEP == 0 and ts <= GDN_CHUNK
    assert SAMPLE_PAGES_PER_STEP % (MOBA_BLOCK // page) == 0 and page == 128

    mod = _modulation(jnp.concatenate([c_prompt, c_sample], axis=0), ada_w, ada_b)

    tab_h = rel_bias.T.astype(F32)
    kq = np.arange(MOBA_BLOCK)
    d_own = kq[None, :] - kq[:, None]
    dist_p = np.stack([np.broadcast_to(d_own + o, (ha, MOBA_BLOCK, MOBA_BLOCK)) for o in (0, MOBA_BLOCK)])
    bt = _bias_tiles(jnp.asarray(dist_p.reshape(-1, MOBA_BLOCK), jnp.int32),
                     jnp.repeat(jnp.tile(tab_h, (2, 1)), MOBA_BLOCK, axis=0))
    bt = bt.reshape(2, ha, MOBA_BLOCK, MOBA_BLOCK)
    cfar_p = bt[1, :, 0, MOBA_BLOCK - 1]

    tq = np.repeat(np.arange(ts), ha)
    pos_q = past_len + tq
    d_last = pos_q[:, None] - ((nblk_s - 1) * MOBA_BLOCK + kq[None, :])
    own_w = 128
    tk = np.arange(own_w)
    d_own_s = np.where(tk[None, :] < ts, tq[:, None] - tk[None, :], -1)
    d_far = np.broadcast_to(pos_q[:, None] - (nblk_s - 2) * MOBA_BLOCK - (MOBA_BLOCK - 1), (ts * ha, 128))
    tab_s = jnp.tile(tab_h, (ts, 1))
    blast = _bias_tiles(jnp.asarray(d_last, jnp.int32), tab_s)
    bown = _bias_tiles(jnp.asarray(d_own_s, jnp.int32), tab_s)
    cfar_s = _bias_tiles(jnp.asarray(d_far, jnp.int32), tab_s)[:, :1]

    cache_kt = cache_k.transpose(0, 1, 3, 4, 2)
    cache_vt = cache_v.transpose(0, 1, 3, 4, 2)

    hp = x_prompt
    hs_tm = x_sample.transpose(1, 0, 2).reshape(ts * bs, d)
    outs = [[] for _ in range(10)]
    for l in range(depth):
        lw = _layer_weights(l, norm_w, w_in, q_norm_w, k_norm_w, conv_b_w, conv_c_w, a_log, dt_bias,
                            o_norm_w, w_out, bs, ts)
        mod_p, mod_s = mod[l, :bp], mod[l, bp:]

        (q, k, v, kb, vt, kmean, za, yb, cbs, qh, kh, vh, kt, zc, gb, gbr, ccs) = _proj_prompt(hp, mod_p, lw)
        att = _attn_prompt(q, kb, vt, kmean.reshape(bp, nblk_p, wa), bt, cfar_p)
        nchunk_p = 8
        yc, s_new = _gdn(qh, kh, vh, kt, gb, gbr, jnp.zeros((bp, hc, HEAD_DIM, HEAD_DIM), F32),
                         lw["onw"], nchunk_p)
        hp = _out_proj(hp.reshape(bp * tp, d), mod_p[:, 2 * d:], att.reshape(bp * tp, wa),
                       za.reshape(bp * tp, wa), yb.reshape(bp * tp, wb), yc, zc.reshape(bp * tp, wb),
                       lw["wout"], tp).reshape(bp, tp, d)
        outs[0].append(k.reshape(bp, tp, ha, HEAD_DIM)); outs[1].append(v.reshape(bp, tp, ha, HEAD_DIM))
        outs[4].append(cbs); outs[6].append(ccs); outs[8].append(s_new)

        rep = lambda a: jnp.tile(a, (ts, 1))
        hist_b = state_conv_b[l].transpose(1, 0, 2).reshape(-1, wb)
        hist_c = state_conv_c[l].transpose(1, 0, 2).reshape(-1, 3 * wb)
        (q, k, v, za, yb, cbs, qc, kc, vc, zc, gb, ccs) = _proj_sample(
            hs_tm, rep(mod_s[:, :d]), rep(mod_s[:, d:2 * d]), lw, hist_b, hist_c, bs)
        seq_major = lambda a: a.reshape(ts, bs, -1).transpose(1, 0, 2)
        qt = seq_major(q).transpose(0, 2, 1)
        new_t = lambda a: jnp.pad(seq_major(a), ((0, 0), (0, own_w - ts), (0, 0))).transpose(0, 2, 1)
        scores, idx = _sample_scores(qt, cache_kt, page_table, l)
        att_t = _sample_attend(scores, idx[:, :, :MOBA_TOPK].reshape(bs, -1), qt, new_t(k), new_t(v),
                               cache_vt, page_table, l, blast, cfar_s, bown)
        att = att_t[:, :, :ts].transpose(2, 0, 1).reshape(ts * bs, wa)
        pad_t = lambda a: jnp.pad(seq_major(a), ((0, 0), (0, GDN_CHUNK - ts), (0, 0)))
        gbs = seq_major(gb)
        gb_pad = jnp.concatenate(
            [pad_t(gb[:, :hc]), jnp.pad(gbs[:, :, hc:], ((0, 0), (0, GDN_CHUNK - ts), (0, 0)), mode="edge")],
            axis=-1)
        yc, s_new = _gdn(*_gdn_layout(pad_t(qc), pad_t(kc), pad_t(vc), gb_pad, hc), state_delta[l],
                         lw["onw"], 1)
        yc = yc[:, :, :ts, :].transpose(2, 0, 1, 3).reshape(ts * bs, wb)
        hs_tm = _out_proj(hs_tm, rep(mod_s[:, 2 * d:]), att, za, yb, yc, zc, lw["wout"], 1)
        outs[2].append(seq_major(k).reshape(bs, ts, ha, HEAD_DIM))
        outs[3].append(seq_major(v).reshape(bs, ts, ha, HEAD_DIM))
        outs[5].append(cbs.reshape(-1, bs, wb).transpose(1, 0, 2))
        outs[7].append(ccs.reshape(-1, bs, 3 * wb).transpose(1, 0, 2))
        outs[9].append(s_new)

    y_sample = hs_tm.reshape(ts, bs, d).transpose(1, 0, 2)
    return (hp, y_sample) + tuple(jnp.stack(o) for o in outs)
```

```python
import functools
import math

import jax
import jax.numpy as jnp
import numpy as np
from jax import lax
from jax.experimental import pallas as pl
from jax.experimental.pallas import tpu as pltpu

F32 = jnp.float32
BF16 = jnp.bfloat16
HIGHEST = lax.Precision.HIGHEST

HEAD_DIM = 64
MOBA_BLOCK = 256
MOBA_TOPK = 3
GDN_CHUNK = 64
VT_PAD = 16
VT_ROWS = HEAD_DIM + VT_PAD
QK_AHEAD = 4
LOG2E = math.log2(math.e)
INV_BASE = 8
REL_BUCKETS = 32
REL_MAX_DIST = 128
RMS_EPS = 1e-6
ROW_TILE = 256
VMEM_LIMIT = 56 * 1024 * 1024
NEG_INF = float("-inf")

_NT = (((1,), (1,)), ((), ()))


def _cparams(*sem):
    return pltpu.CompilerParams(dimension_semantics=sem, vmem_limit_bytes=VMEM_LIMIT)


def _const_spec(shape):
    zeros = (0,) * len(shape)
    return pl.BlockSpec(shape, lambda *_: zeros)


def _silu(x):
    return x * jax.nn.sigmoid(x)


def _softplus(x):
    return jnp.maximum(x, 0.0) + jnp.log(1.0 + jnp.exp(-jnp.abs(x)))


def _split(a):
    hi = a.astype(BF16)
    lo = (a - hi.astype(F32)).astype(BF16)
    return hi, lo


def _dot(a, b, dims=None):
    if dims is None:
        return jnp.dot(a, b, preferred_element_type=F32)
    return lax.dot_general(a, b, dims, preferred_element_type=F32)


def _mm1(a, b, dims=None):
    return _dot(a.astype(BF16), b.astype(BF16), dims)


def _mm2(a, b, dims=None):
    ah = a.astype(BF16)
    bh, bl = _split(b)
    return _dot(ah, bh, dims) + _dot(ah, bl, dims)


def _mm3(a, b, dims=None):
    ah, al = _split(a)
    bh, bl = _split(b)
    return _dot(ah, bh, dims) + _dot(al, bh, dims) + _dot(ah, bl, dims)


def _group_sum(x):
    lane = lax.broadcasted_iota(jnp.int32, (1, 2 * HEAD_DIM), 1)
    first = lane < HEAD_DIM
    outs = []
    for c in range(x.shape[1] // (2 * HEAD_DIM)):
        xc = x[:, c * 2 * HEAD_DIM:(c + 1) * 2 * HEAD_DIM]
        s0 = jnp.sum(jnp.where(first, xc, 0.0), axis=-1, keepdims=True)
        s1 = jnp.sum(jnp.where(first, 0.0, xc), axis=-1, keepdims=True)
        outs.append(jnp.where(first, s0, s1))
    return jnp.concatenate(outs, axis=1)


def _mod_kernel(c_ref, w_ref, b_ref, o_ref):
    o_ref[0] = jnp.dot(c_ref[...], w_ref[0], precision=HIGHEST, preferred_element_type=F32) + b_ref[0]


def _modulation(c_all, ada_w, ada_b):
    depth, d, n3 = ada_w.shape
    rows = c_all.shape[0]
    tn = 512
    return pl.pallas_call(
        _mod_kernel,
        grid=(depth, n3 // tn),
        in_specs=[
            pl.BlockSpec((rows, d), lambda l, n: (0, 0)),
            pl.BlockSpec((1, d, tn), lambda l, n: (l, 0, n)),
            pl.BlockSpec((1, 1, tn), lambda l, n: (l, 0, n)),
        ],
        out_specs=pl.BlockSpec((1, rows, tn), lambda l, n: (l, 0, n)),
        out_shape=jax.ShapeDtypeStruct((depth, rows, n3), F32),
        compiler_params=_cparams("parallel", "parallel"),
        name="modulation",
    )(c_all, ada_w, ada_b.reshape(depth, 1, n3))


def _bias_kernel(dist_ref, tab_ref, o_ref):
    dist = dist_ref[...]
    n = jnp.maximum(dist, 0)
    max_exact = REL_BUCKETS // 2
    nf = jnp.maximum(n, max_exact).astype(F32)
    large = max_exact + (jnp.log(nf / max_exact) / math.log(REL_MAX_DIST / max_exact)
                         * (REL_BUCKETS - max_exact)).astype(jnp.int32)
    bucket = jnp.where(n < max_exact, n, jnp.minimum(large, REL_BUCKETS - 1))
    tab = tab_ref[...]
    acc = jnp.zeros(dist.shape, F32)
    for b in range(REL_BUCKETS):
        acc = jnp.where(bucket == b, tab[:, b:b + 1], acc)
    o_ref[...] = jnp.where(dist >= 0, acc, NEG_INF)


def _bias_tiles(dist, tab):
    rows, cols = dist.shape
    tr = min(rows, 256)
    return pl.pallas_call(
        _bias_kernel,
        grid=(rows // tr,),
        in_specs=[pl.BlockSpec((tr, cols), lambda i: (i, 0)),
                  pl.BlockSpec((tr, REL_BUCKETS), lambda i: (i, 0))],
        out_specs=pl.BlockSpec((tr, cols), lambda i: (i, 0)),
        out_shape=jax.ShapeDtypeStruct((rows, cols), F32),
        compiler_params=_cparams("parallel"),
        name="bias_tiles",
    )(dist, tab)


def _proj_common(x, shift, scale, normw, w_ref, qw, kw, alog, dtb, cum_ref):
    ms = jnp.mean(x * x, axis=-1, keepdims=True)
    hn = (x * lax.rsqrt(ms + RMS_EPS) * normw) * (1.0 + scale) + shift
    z = jnp.dot(hn.astype(BF16), w_ref[...], preferred_element_type=F32)
    wa = qw.shape[-1]
    wb = wa // 2
    o = 0
    q = z[:, o:o + wa]; o += wa
    k = z[:, o:o + wa]; o += wa
    v = z[:, o:o + wa]; o += wa
    za = z[:, o:o + wa]; o += wa
    hb = z[:, o:o + wb]; o += wb
    bg = z[:, o:o + wb]; o += wb
    cg = z[:, o:o + wb]; o += wb
    zb = z[:, o:o + wb]; o += wb
    qkv = z[:, o:o + 3 * wb]; o += 3 * wb
    zc = z[:, o:o + wb]; o += wb
    z8 = z[:, o:o + 8]

    qn = q * lax.rsqrt(_group_sum(q * q) * (1.0 / HEAD_DIM) + RMS_EPS) * qw
    kn = k * lax.rsqrt(_group_sum(k * k) * (1.0 / HEAD_DIM) + RMS_EPS) * kw

    lane8 = lax.broadcasted_iota(jnp.int32, z8.shape, 1)
    beta = jax.nn.sigmoid(z8)
    g = -jnp.exp(alog) * _softplus(z8 + dtb)
    bg8 = jnp.where(lane8 < 4, beta, g)
    cum = jnp.dot(cum_ref[...], bg8, precision=HIGHEST, preferred_element_type=F32)
    gb = jnp.where(lane8 < 4, bg8, cum)
    return dict(q=qn * (HEAD_DIM ** -0.5), k=kn, v=v, za=_silu(za), u=cg * hb, zbg=_silu(zb) * bg,
                qkv=qkv, zc=_silu(zc), gb=gb)


def _gdn_qkv(y):
    wb = y.shape[-1] // 3
    y = _silu(y)
    qc, kc, vc = y[:, :wb], y[:, wb:2 * wb], y[:, 2 * wb:]
    qc = qc * lax.rsqrt(_group_sum(qc * qc) + RMS_EPS) * (HEAD_DIM ** -0.5)
    kc = kc * lax.rsqrt(_group_sum(kc * kc) + RMS_EPS)
    return qc, kc, vc


def _proj_prompt_kernel(x_ref, mod_ref, normw_ref, w_ref, qw_ref, kw_ref,
                        cbw_ref, ccw_ref, alog_ref, dtb_ref, cum_ref,
                        q_ref, k_ref, v_ref, kb_ref, vt_ref, km_ref, za_ref, yb_ref, cbs_ref,
                        qc_ref, kc_ref, vc_ref, kt_ref, zc_ref, gb_ref, gbr_ref, ccs_ref,
                        carry_b, carry_c):
    ti = pl.program_id(1)
    d = x_ref.shape[-1]
    rows = x_ref.shape[1]

    @pl.when(ti == 0)
    def _():
        carry_b[...] = jnp.zeros_like(carry_b)
        carry_c[...] = jnp.zeros_like(carry_c)

    mod = mod_ref[0]
    p = _proj_common(x_ref[0], mod[:, :d], mod[:, d:2 * d], normw_ref[...], w_ref, qw_ref[...],
                     kw_ref[...], alog_ref[...], dtb_ref[...], cum_ref)
    q_ref[0] = p["q"]
    k_ref[0] = p["k"]
    v_ref[0] = p["v"]
    km_ref[0, 0] = jnp.mean(p["k"], axis=0, keepdims=True)
    za_ref[0] = p["za"].astype(za_ref.dtype)
    zc_ref[0] = p["zc"].astype(zc_ref.dtype)
    gb = p["gb"]
    gb_ref[0] = gb
    kb_ref[0, 0] = p["k"].astype(BF16)
    vtb = p["v"].T.astype(BF16)
    ones = jnp.ones((VT_PAD, rows), BF16)
    pieces = []
    for h in range(vtb.shape[0] // HEAD_DIM):
        pieces += [vtb[h * HEAD_DIM:(h + 1) * HEAD_DIM], ones]
    vt_ref[0, 0] = jnp.concatenate(pieces, axis=0)
    ncol = gb.shape[1]
    eye8 = jnp.where(lax.broadcasted_iota(jnp.int32, (ncol, ncol), 0)
                     == lax.broadcasted_iota(jnp.int32, (ncol, ncol), 1), 1.0, 0.0)
    gbr = lax.dot_general(eye8, gb, _NT, precision=HIGHEST, preferred_element_type=F32)
    for cc in range(rows // GDN_CHUNK):
        gbr_ref[0, cc] = gbr[:, cc * GDN_CHUNK:(cc + 1) * GDN_CHUNK]

    u = p["u"]
    ext = jnp.concatenate([carry_b[...], u], axis=0)
    cbw = cbw_ref[...]
    nb = cbw.shape[0]
    yb = u * cbw[nb - 1:nb]
    for j in range(nb - 1):
        s = nb - 1 - j
        yb = yb + ext[8 - s:8 - s + rows] * cbw[j:j + 1]
    yb_ref[0] = (p["zbg"] * yb).astype(yb_ref.dtype)
    carry_b[...] = u[rows - 8:]
    cbs_ref[0] = u[rows - (nb - 1):]

    qkv = p["qkv"]
    extc = jnp.concatenate([carry_c[...], qkv], axis=0)
    ccw = ccw_ref[...]
    nc = ccw.shape[0]
    yc = qkv * ccw[nc - 1:nc]
    for j in range(nc - 1):
        s = nc - 1 - j
        yc = yc + extc[8 - s:8 - s + rows] * ccw[j:j + 1]
    carry_c[...] = qkv[rows - 8:]
    ccs_ref[0] = qkv[rows - (nc - 1):]
    qc, kc, vc = _gdn_qkv(yc)
    kct = kc.T
    for h in range(qc.shape[1] // HEAD_DIM):
        hs = slice(h * HEAD_DIM, (h + 1) * HEAD_DIM)
        qc_ref[0, h] = qc[:, hs]
        kc_ref[0, h] = kc[:, hs]
        vc_ref[0, h] = vc[:, hs]
        for cc in range(rows // GDN_CHUNK):
            kt_ref[0, h, cc] = kct[hs, cc * GDN_CHUNK:(cc + 1) * GDN_CHUNK]


def _proj_sample_kernel(x_ref, shift_ref, scale_ref, normw_ref, w_ref, qw_ref, kw_ref,
                        cbw_ref, ccw_ref, alog_ref, dtb_ref, cum_ref, hb_ref, hc_ref,
                        q_ref, k_ref, v_ref, za_ref, yb_ref, cbs_ref,
                        qc_ref, kc_ref, vc_ref, zc_ref, gb_ref, ccs_ref, *, nseq):
    rows = x_ref.shape[0]
    p = _proj_common(x_ref[...], shift_ref[...], scale_ref[...], normw_ref[...], w_ref, qw_ref[...],
                     kw_ref[...], alog_ref[...], dtb_ref[...], cum_ref)
    q_ref[...] = p["q"]
    k_ref[...] = p["k"]
    v_ref[...] = p["v"]
    za_ref[...] = p["za"]
    zc_ref[...] = p["zc"]
    gb_ref[...] = p["gb"]

    cbw = cbw_ref[...]
    nb = cbw.shape[0]
    ext = jnp.concatenate([hb_ref[...], p["u"]], axis=0)
    yb = ext[0:rows] * cbw[0:1]
    for j in range(1, nb):
        yb = yb + ext[j * nseq:j * nseq + rows] * cbw[j:j + 1]
    yb_ref[...] = p["zbg"] * yb
    cbs_ref[...] = ext[rows:]

    ccw = ccw_ref[...]
    nc = ccw.shape[0]
    extc = jnp.concatenate([hc_ref[...], p["qkv"]], axis=0)
    yc = extc[0:rows] * ccw[0:1]
    for j in range(1, nc):
        yc = yc + extc[j * nseq:j * nseq + rows] * ccw[j:j + 1]
    ccs_ref[...] = extc[rows:]
    qc, kc, vc = _gdn_qkv(yc)
    qc_ref[...] = qc
    kc_ref[...] = kc
    vc_ref[...] = vc


def _proj_prompt(x, mod, lw):
    b, t, d = x.shape
    wcat = lw["wcat"]
    ncat = wcat.shape[1]
    wa, wb = d // 2, d // 4
    nt = t // ROW_TILE
    row = lambda w: pl.BlockSpec((1, ROW_TILE, w), lambda bi, ti: (bi, ti, 0))
    per_b = lambda r, w: pl.BlockSpec((1, r, w), lambda bi, ti: (bi, 0, 0))
    nb, nc = lw["conv_b_w"].shape[0], lw["conv_c_w"].shape[0]
    hc = wb // HEAD_DIM
    cpt = ROW_TILE // GDN_CHUNK
    nch = t // GDN_CHUNK
    out_shape = [
        jax.ShapeDtypeStruct((b, t, wa), F32),
        jax.ShapeDtypeStruct((b, t, wa), F32),
        jax.ShapeDtypeStruct((b, t, wa), F32),
        jax.ShapeDtypeStruct((b, nt, ROW_TILE, wa), BF16),
        jax.ShapeDtypeStruct((b, nt, wa // HEAD_DIM * VT_ROWS, ROW_TILE), BF16),
        jax.ShapeDtypeStruct((b, nt, 1, wa), F32),
        jax.ShapeDtypeStruct((b, t, wa), BF16),
        jax.ShapeDtypeStruct((b, t, wb), BF16),
        jax.ShapeDtypeStruct((b, nb - 1, wb), F32),
        jax.ShapeDtypeStruct((b, hc, t, HEAD_DIM), F32),
        jax.ShapeDtypeStruct((b, hc, t, HEAD_DIM), F32),
        jax.ShapeDtypeStruct((b, hc, t, HEAD_DIM), F32),
        jax.ShapeDtypeStruct((b, hc, nch, HEAD_DIM, GDN_CHUNK), F32),
        jax.ShapeDtypeStruct((b, t, wb), BF16),
        jax.ShapeDtypeStruct((b, t, 8), F32),
        jax.ShapeDtypeStruct((b, nch, 8, GDN_CHUNK), F32),
        jax.ShapeDtypeStruct((b, nc - 1, 3 * wb), F32),
    ]
    blk4 = lambda r, c: pl.BlockSpec((1, 1, r, c), lambda bi, ti: (bi, ti, 0, 0))
    heads = pl.BlockSpec((1, hc, ROW_TILE, HEAD_DIM), lambda bi, ti: (bi, 0, ti, 0))
    out_specs = [
        row(wa), row(wa), row(wa), blk4(ROW_TILE, wa), blk4(wa // HEAD_DIM * VT_ROWS, ROW_TILE), blk4(1, wa),
        row(wa), row(wb), per_b(nb - 1, wb),
        heads, heads, heads,
        pl.BlockSpec((1, hc, cpt, HEAD_DIM, GDN_CHUNK), lambda bi, ti: (bi, 0, ti, 0, 0)),
        row(wb), row(8),
        pl.BlockSpec((1, cpt, 8, GDN_CHUNK), lambda bi, ti: (bi, ti, 0, 0)),
        per_b(nc - 1, 3 * wb),
    ]
    in_specs = [
        row(d),
        pl.BlockSpec((1, 1, 3 * d), lambda bi, ti: (bi, 0, 0)),
        _const_spec((1, d)),
        _const_spec((d, ncat)),
        _const_spec((1, wa)), _const_spec((1, wa)),
        _const_spec((nb, wb)), _const_spec((nc, 3 * wb)),
        _const_spec((1, 8)), _const_spec((1, 8)),
        _const_spec((ROW_TILE, ROW_TILE)),
    ]
    return pl.pallas_call(
        _proj_prompt_kernel,
        grid=(b, nt),
        in_specs=in_specs,
        out_specs=out_specs,
        out_shape=out_shape,
        scratch_shapes=[pltpu.VMEM((8, wb), F32), pltpu.VMEM((8, 3 * wb), F32)],
        compiler_params=_cparams("parallel", "arbitrary"),
        name="proj_prompt",
    )(x, mod.reshape(b, 1, 3 * d), lw["norm_w"], wcat, lw["qw"], lw["kw"],
      lw["conv_b_w"], lw["conv_c_w"], lw["alog8"], lw["dtb8"], lw["cum_prompt"])


def _proj_sample(x_tm, shift, scale, lw, hist_b, hist_c, nseq):
    rows, d = x_tm.shape
    wa, wb = d // 2, d // 4
    nb, nc = lw["conv_b_w"].shape[0], lw["conv_c_w"].shape[0]
    out_shape = [
        jax.ShapeDtypeStruct((rows, wa), F32), jax.ShapeDtypeStruct((rows, wa), F32),
        jax.ShapeDtypeStruct((rows, wa), F32), jax.ShapeDtypeStruct((rows, wa), F32),
        jax.ShapeDtypeStruct((rows, wb), F32),
        jax.ShapeDtypeStruct(((nb - 1) * nseq, wb), F32),
        jax.ShapeDtypeStruct((rows, wb), F32), jax.ShapeDtypeStruct((rows, wb), F32),
        jax.ShapeDtypeStruct((rows, wb), F32), jax.ShapeDtypeStruct((rows, wb), F32),
        jax.ShapeDtypeStruct((rows, 8), F32),
        jax.ShapeDtypeStruct(((nc - 1) * nseq, 3 * wb), F32),
    ]
    return pl.pallas_call(
        functools.partial(_proj_sample_kernel, nseq=nseq),
        out_shape=out_shape,
        compiler_params=pltpu.CompilerParams(vmem_limit_bytes=VMEM_LIMIT),
        name="proj_sample",
    )(x_tm, shift, scale, lw["norm_w"], lw["wcat"], lw["qw"], lw["kw"],
      lw["conv_b_w"], lw["conv_c_w"], lw["alog8"], lw["dtb8"], lw["cum_sample"], hist_b, hist_c)


def _attn_prompt_kernel(cfar_ref, q_ref, kb_ref, vt_ref, km_ref, bt_ref, o_ref,
                        sel_s, acc_s, *, nheads, nblk):
    i = pl.program_id(1)
    blk = q_ref.shape[1]
    qt = q_ref[0].T * LOG2E
    qtb = qt.astype(BF16)
    lane = lax.broadcasted_iota(jnp.int32, (1, 2 * HEAD_DIM), 1)
    sub = lax.broadcasted_iota(jnp.int32, (2 * HEAD_DIM, 1), 0)
    lane_half = [lane < HEAD_DIM, lane >= HEAD_DIM]
    sub_half = [sub < HEAD_DIM, sub >= HEAD_DIM]

    km = km_ref[0]
    n = lax.broadcasted_iota(jnp.int32, (nblk, blk), 0)
    n_f = n.astype(F32)
    past = n < i
    for h in range(nheads):
        p, par = divmod(h, 2)
        kmp = jnp.where(lane_half[par], km[:, p * 128:(p + 1) * 128], 0.0)
        gate = jnp.dot(kmp, qt[p * 128:(p + 1) * 128, :], precision=HIGHEST, preferred_element_type=F32)
        gate = jnp.where(past, gate, NEG_INF)
        picked = jnp.zeros((nblk, blk), F32)
        for _ in range(MOBA_TOPK):
            top = jnp.max(gate, axis=0, keepdims=True)
            first = jnp.min(jnp.where(gate == top, n_f, float(nblk)), axis=0, keepdims=True)
            chosen = n_f == first
            picked = jnp.where(chosen, 1.0, picked)
            gate = jnp.where(chosen, NEG_INF, gate)
        sel_s[h * nblk:(h + 1) * nblk, :] = jnp.where(past, picked, 0.0)

    qm = []
    for h in range(nheads):
        p, par = divmod(h, 2)
        qm.append(jnp.where(sub_half[par], qtb[p * 128:(p + 1) * 128, :], jnp.zeros((), BF16)))

    def scores(kblk, h):
        p = h // 2
        return jnp.dot(kblk[:, p * 128:(p + 1) * 128], qm[h], preferred_element_type=F32)

    def pv_dot(vblk, h, pr):
        res = jnp.dot(vblk[h * VT_ROWS:(h + 1) * VT_ROWS, :], pr.astype(BF16), preferred_element_type=F32)
        return res[:HEAD_DIM], res[HEAD_DIM:HEAD_DIM + 1]


    def block(js, kind, m_all, l_all):
        nb = len(js)
        kblk = kb_ref[0, js[0]] if nb == 1 else jnp.concatenate([kb_ref[0, j] for j in js], axis=0)
        sts = [scores(kblk, h) for h in range(min(QK_AHEAD, nheads))]
        m_rows, l_rows = [], []
        pend = None

        def finish(h, alpha, ons, pvs):
            rs = slice(h * HEAD_DIM, (h + 1) * HEAD_DIM)
            if kind == "own":
                acc_s[rs, :] = pvs[0]
            else:
                new = alpha * acc_s[rs, :]
                for on, pv in zip(ons, pvs):
                    new = new + jnp.where(on, pv, 0.0)
                acc_s[rs, :] = new

        for h in range(nheads):
            st = sts[h]
            if h + QK_AHEAD < nheads:
                sts.append(scores(kblk, h + QK_AHEAD))
            if kind == "own":
                st = bt_ref[0, h] * LOG2E + st
                m_new = jnp.max(st, axis=0, keepdims=True)
                pv, l_new = pv_dot(vt_ref[0, js[0]], h, jnp.exp2(st - m_new))
                alpha, ons, pvs = None, None, [pv]
            else:
                if kind == "near":
                    st = bt_ref[1, h] * LOG2E + st
                    shift = 0.0
                else:
                    shift = cfar_ref[h] * LOG2E
                parts = [st[s * blk:(s + 1) * blk] for s in range(nb)]
                ons = [sel_s[pl.ds(h * nblk + j, 1), :] > 0.0 for j in js]
                m_old = m_all[h:h + 1, :]
                m_new = m_old
                for on, part in zip(ons, parts):
                    m_new = jnp.maximum(m_new, jnp.where(on, jnp.max(part, axis=0, keepdims=True) + shift, NEG_INF))
                alpha = jnp.exp2(m_old - m_new)
                l_new = alpha * l_all[h:h + 1, :]
                pvs = []
                for j, on, part in zip(js, ons, parts):
                    pv, psum = pv_dot(vt_ref[0, j], h, jnp.exp2(part - (m_new - shift)))
                    pvs.append(pv)
                    l_new = l_new + jnp.where(on, psum, 0.0)
            m_rows.append(m_new)
            l_rows.append(l_new)
            if pend is not None:
                finish(*pend)
            pend = (h, alpha, ons, pvs)
        finish(*pend)
        return jnp.concatenate(m_rows, axis=0), jnp.concatenate(l_rows, axis=0)

    keep = lambda m, l: (m, l)
    m_all, l_all = block([i], "own", None, None)
    m_all, l_all = lax.cond(i >= 1, lambda m, l: block([i - 1], "near", m, l), keep, m_all, l_all)
    nfar = i - 1
    odd = (nfar >= 1) & ((nfar & 1) == 1)
    m_all, l_all = lax.cond(odd, lambda m, l: block([nfar - 1], "far", m, l), keep, m_all, l_all)
    m_all, l_all = lax.fori_loop(0, nfar // 2, lambda jj, c: block([2 * jj, 2 * jj + 1], "far", *c), (m_all, l_all))

    for h in range(nheads):
        rs = slice(h * HEAD_DIM, (h + 1) * HEAD_DIM)
        acc_s[rs, :] = acc_s[rs, :] / l_all[h:h + 1, :]
    o_ref[0] = acc_s[...].T.astype(o_ref.dtype)


def _attn_prompt(q, kb, vt, kmean, bt, cfar):
    b, t, wa = q.shape
    nheads = wa // HEAD_DIM
    nblk = t // MOBA_BLOCK
    grid_spec = pltpu.PrefetchScalarGridSpec(
        num_scalar_prefetch=0,
        grid=(b, nblk),
        in_specs=[
            pl.BlockSpec(memory_space=pltpu.SMEM),
            pl.BlockSpec((1, MOBA_BLOCK, wa), lambda bi, i: (bi, i, 0)),
            pl.BlockSpec((1, nblk, MOBA_BLOCK, wa), lambda bi, i: (bi, 0, 0, 0)),
            pl.BlockSpec((1, nblk, nheads * VT_ROWS, MOBA_BLOCK), lambda bi, i: (bi, 0, 0, 0)),
            pl.BlockSpec((1, nblk, wa), lambda bi, i: (bi, 0, 0)),
            pl.BlockSpec((2, nheads, MOBA_BLOCK, MOBA_BLOCK), lambda bi, i: (0, 0, 0, 0)),
        ],
        out_specs=pl.BlockSpec((1, MOBA_BLOCK, wa), lambda bi, i: (bi, i, 0)),
        scratch_shapes=[
            pltpu.VMEM((nheads * nblk, MOBA_BLOCK), F32),
            pltpu.VMEM((wa, MOBA_BLOCK), F32),
        ],
    )
    return pl.pallas_call(
        functools.partial(_attn_prompt_kernel, nheads=nheads, nblk=nblk),
        grid_spec=grid_spec,
        out_shape=jax.ShapeDtypeStruct((b, t, wa), BF16),
        compiler_params=_cparams("parallel", "arbitrary"),
        name="attn_prompt",
    )(cfar, q, kb, vt, kmean, bt)


SAMPLE_PAGES_PER_STEP = 16


def _sample_scores_kernel(pt_ref, q_ref, *rest, nheads, nblk, page, tq):
    npg = SAMPLE_PAGES_PER_STEP
    k_refs = rest[:npg]
    s_ref, idx_ref, qhl_s, gate_s = rest[npg:]
    s = pl.program_id(1)
    nrows = tq * nheads
    wa = nheads * HEAD_DIM
    ppb = MOBA_BLOCK // page

    @pl.when(s == 0)
    def _():
        r = lax.broadcasted_iota(jnp.int32, (nrows, wa), 0)
        c = lax.broadcasted_iota(jnp.int32, (nrows, wa), 1)
        hi, lo = _split(jnp.where(c // HEAD_DIM == r % nheads, q_ref[0], 0.0))
        qhl_s[:nrows, :] = hi
        qhl_s[nrows:, :] = lo
        gate_s[...] = jnp.full(gate_s.shape, NEG_INF, F32)

    lane = lax.broadcasted_iota(jnp.int32, gate_s.shape, 1)
    qhl = qhl_s[...]
    bsum = None
    for e in range(npg):
        kh, kl = _split(k_refs[e][...].reshape(wa, page))
        both = jnp.dot(qhl, kh, preferred_element_type=F32)
        sp = both[:nrows] + both[nrows:] + jnp.dot(qhl[:nrows], kl, preferred_element_type=F32)
        s_ref[0, e // ppb, :, (e % ppb) * page:(e % ppb + 1) * page] = sp
        rs = jnp.sum(sp, axis=1, keepdims=True)
        bsum = rs if e % ppb == 0 else bsum + rs
        if e % ppb == ppb - 1:
            n = s * (npg // ppb) + e // ppb
            gate_s[...] = jnp.where(lane == n, bsum * (1.0 / MOBA_BLOCK), gate_s[...])

    @pl.when(s == pl.num_programs(1) - 1)
    def _():
        gate = gate_s[...]
        rank = jnp.zeros(gate.shape, F32)
        for sh in range(1, gate.shape[1]):
            other = pltpu.roll(gate, sh, 1)
            rank = rank + jnp.where(other > gate, 1.0, 0.0)
            rank = rank + jnp.where((other == gate) & (lane >= sh), 1.0, 0.0)
        lane_f = lane.astype(F32)
        out = jnp.zeros(gate.shape, F32)
        for j in range(MOBA_TOPK):
            pick = jnp.sum(jnp.where((rank == j) & (lane < nblk), lane_f, 0.0), axis=1, keepdims=True)
            out = jnp.where(lane == j, pick, out)
        idx_ref[0] = out.astype(jnp.int32)


def _sample_scores(q_rows, cache_kt, page_table, layer):
    nseq, nrows, wa = q_rows.shape
    nheads = wa // HEAD_DIM
    tq = nrows // nheads
    page = cache_kt.shape[-1]
    n_pages = page_table.shape[1]
    nblk = n_pages * page // MOBA_BLOCK
    npg = SAMPLE_PAGES_PER_STEP

    def page_spec(e):
        return pl.BlockSpec((None, None, nheads, HEAD_DIM, page),
                            lambda b, s, pt: (layer, pt[b, s * npg + e], 0, 0, 0))

    grid_spec = pltpu.PrefetchScalarGridSpec(
        num_scalar_prefetch=1,
        grid=(nseq, n_pages // npg),
        in_specs=[pl.BlockSpec((1, nrows, wa), lambda b, s, pt: (b, 0, 0))] + [page_spec(e) for e in range(npg)],
        out_specs=[pl.BlockSpec((1, npg * page // MOBA_BLOCK, nrows, MOBA_BLOCK), lambda b, s, pt: (b, s, 0, 0)),
                   pl.BlockSpec((1, nrows, 128), lambda b, s, pt: (b, 0, 0))],
        scratch_shapes=[pltpu.VMEM((2 * nrows, wa), BF16), pltpu.VMEM((nrows, 128), F32)],
    )
    return pl.pallas_call(
        functools.partial(_sample_scores_kernel, nheads=nheads, nblk=nblk, page=page, tq=tq),
        grid_spec=grid_spec,
        out_shape=[jax.ShapeDtypeStruct((nseq, nblk, nrows, MOBA_BLOCK), F32),
                   jax.ShapeDtypeStruct((nseq, nrows, 128), jnp.int32)],
        compiler_params=_cparams("parallel", "arbitrary"),
        name="sample_scores",
    )(page_table, q_rows, *([cache_kt] * npg))


def _sample_attend_kernel(pt_ref, idx_ref, s_ref, qt_ref, knt_ref, vnt_ref, blast_ref, cfar_ref, bown_ref,
                          cv_ref, o_ref, vbuf, sems, *, nheads, nblk, page, tq, layer):
    b = pl.program_id(0)
    nrows = tq * nheads
    ppb = MOBA_BLOCK // page

    slot = b % 2

    def slab_copies(seq, half):
        out = []
        for p in range(nrows):
            h = p % nheads
            for j in range(MOBA_TOPK):
                n = idx_ref[seq, p * MOBA_TOPK + j]
                for e in range(ppb):
                    pg = pt_ref[seq, n * ppb + e]
                    out.append(pltpu.make_async_copy(cv_ref.at[layer, pg, h], vbuf.at[half, p, j * ppb + e],
                                                     sems.at[half, p]))
        return out

    @pl.when(b == 0)
    def _():
        for cp in slab_copies(b, slot):
            cp.start()

    @pl.when(b + 1 < pl.num_programs(0))
    def _():
        for cp in slab_copies(b + 1, 1 - slot):
            cp.start()

    qt = qt_ref[0]
    own, past = [], [[] for _ in range(MOBA_TOPK)]
    for p in range(nrows):
        t, h = divmod(p, nheads)
        hs = slice(h * HEAD_DIM, (h + 1) * HEAD_DIM)
        qb = jnp.broadcast_to(qt[hs, t:t + 1], (HEAD_DIM, knt_ref.shape[2]))
        own.append(jnp.sum(knt_ref[0, hs, :] * qb, axis=0, keepdims=True) + bown_ref[p:p + 1, :])
        for j in range(MOBA_TOPK):
            n = idx_ref[b, p * MOBA_TOPK + j]
            bias = jnp.where(n == nblk - 1, blast_ref[p:p + 1, :], cfar_ref[p:p + 1, :])
            past[j].append(s_ref[0, n, p:p + 1, :] + bias)
    l_own = jnp.concatenate(own, axis=0)
    l_past = [jnp.concatenate(rows, axis=0) for rows in past]
    m = jnp.max(l_own, axis=1, keepdims=True)
    for lg in l_past:
        m = jnp.maximum(m, jnp.max(lg, axis=1, keepdims=True))
    w_own = jnp.exp(l_own - m)
    w_past = [jnp.exp(lg - m) for lg in l_past]
    denom = jnp.sum(w_own, axis=1, keepdims=True)
    for w in w_past:
        denom = denom + jnp.sum(w, axis=1, keepdims=True)
    inv = 1.0 / denom
    w_own = w_own * inv
    w_past = [w * inv for w in w_past]

    for cp in slab_copies(b, slot):
        cp.wait()

    o_ref[0] = jnp.zeros(o_ref.shape[1:], F32)
    for p in range(nrows):
        t, h = divmod(p, nheads)
        hs = slice(h * HEAD_DIM, (h + 1) * HEAD_DIM)
        acc = w_own[p:p + 1, :] * vnt_ref[0, hs, :]
        for j in range(MOBA_TOPK):
            for e in range(ppb):
                acc = acc + w_past[j][p:p + 1, e * page:(e + 1) * page] * vbuf[slot, p, j * ppb + e]
        o_ref[0, hs, t:t + 1] = jnp.sum(acc, axis=1, keepdims=True)


def _sample_attend(scores, idx, qt, knt, vnt, cache_vt, page_table, layer, blast, cfar, bown):
    nseq, nblk, nrows, _ = scores.shape
    wa, tq = qt.shape[1], qt.shape[2]
    nheads = wa // HEAD_DIM
    page = cache_vt.shape[-1]
    assert knt.shape[2] == page and bown.shape[1] == page
    seq = lambda shape: pl.BlockSpec((1,) + shape, lambda b, pt, ix: (b,) + (0,) * len(shape))
    whole = lambda shape: pl.BlockSpec(shape, lambda b, pt, ix: (0,) * len(shape))
    grid_spec = pltpu.PrefetchScalarGridSpec(
        num_scalar_prefetch=2,
        grid=(nseq,),
        in_specs=[seq((nblk, nrows, MOBA_BLOCK)), seq((wa, tq)), seq((wa, page)), seq((wa, page)),
                  whole(blast.shape), whole(cfar.shape), whole(bown.shape),
                  pl.BlockSpec(memory_space=pl.ANY)],
        out_specs=seq((wa, 128)),
        scratch_shapes=[pltpu.VMEM((2, nrows, MOBA_TOPK * (MOBA_BLOCK // page), HEAD_DIM, page), F32),
                        pltpu.SemaphoreType.DMA((2, nrows))],
    )
    return pl.pallas_call(
        functools.partial(_sample_attend_kernel, nheads=nheads, nblk=nblk, page=page, tq=tq, layer=layer),
        grid_spec=grid_spec,
        out_shape=jax.ShapeDtypeStruct((nseq, wa, 128), F32),
        compiler_params=_cparams("arbitrary"),
        name="sample_attend",
    )(page_table, idx, scores, qt, knt, vnt, blast, cfar, bown, cache_vt)


def _gdn_kernel(q_ref, k_ref, v_ref, kt_ref, gbc_ref, gbr_ref, s0_ref, onw_ref, o_ref, sout_ref,
                s_s, lhs_s, add_s, *, nheads, nchunk, chunks_per_iter):
    tc = pl.program_id(1)
    c_len = GDN_CHUNK
    hd = HEAD_DIM

    @pl.when(tc == 0)
    def _():
        s_s[...] = s0_ref[0]

    row = lax.broadcasted_iota(jnp.int32, (c_len, c_len), 0)
    col = lax.broadcasted_iota(jnp.int32, (c_len, c_len), 1)
    tri = row >= col
    tri_s = row > col
    eye = row == col
    eye_f = jnp.where(eye, 1.0, 0.0)
    blk = lambda size: (row // size) == (col // size)

    def chunk_a(it, carry):
        chains = [(it * chunks_per_iter + cc, h) for cc in range(chunks_per_iter) for h in range(nheads)]
        st = []
        for c, h in chains:
            gbc = gbc_ref[0, pl.ds(c * c_len, c_len), :]
            gbr = gbr_ref[0, c]
            q = q_ref[0, h, pl.ds(c * c_len, c_len), :]
            k = k_ref[0, h, pl.ds(c * c_len, c_len), :]
            v = v_ref[0, h, pl.ds(c * c_len, c_len), :]
            bcol = gbc[:, h:h + 1]
            gcol = gbc[:, nheads + h:nheads + h + 1]
            grow = gbr[nheads + h:nheads + h + 1, :]
            glast = gcol[c_len - 1:c_len, :]
            st.append(dict(c=c, h=h, q=q, k=k, v=v, bcol=bcol, grow=grow, glast=glast,
                           decay=jnp.where(tri, jnp.exp(gcol - grow), 0.0), egc=jnp.exp(gcol)))
        for d in st:
            d["kq"] = _mm1(jnp.concatenate([d["k"], d["q"]], axis=0), d["k"], _NT)
        for d in st:
            kk, qk = d["kq"][:c_len], d["kq"][c_len:]
            d["n"] = jnp.where(tri_s, d["bcol"] * kk * d["decay"], 0.0)
            d["qkd"] = jnp.where(tri, qk * d["decay"], 0.0)
            d["x"] = jnp.concatenate([d["v"] * d["bcol"], d["k"] * (d["bcol"] * d["egc"])], axis=1)
            d["a"] = -jnp.where(blk(INV_BASE), d["n"], 0.0)
        for d in st:
            d["p2"] = _mm2(d["a"], d["a"])
        for d in st:
            d["res"] = _mm2(jnp.concatenate([d["p2"], eye_f + d["a"]], axis=0), d["p2"])
        for d in st:
            d["p4"] = d["res"][:c_len]
            d["t"] = eye_f + d["a"] + d["res"][c_len:]
        for d in st:
            d["res"] = _mm2(d["t"], d["p4"])
        for d in st:
            d["t"] = d["t"] + d["res"]
        size = 2 * INV_BASE
        while size <= c_len:
            off = blk(size) & jnp.logical_not(blk(size // 2))
            for d in st:
                d["y"] = _mm1(jnp.where(off, d["n"], 0.0), d["t"])
            for d in st:
                d["res"] = _mm1(d["t"], d["y"])
            for d in st:
                d["t"] = d["t"] - d["res"]
            size *= 2
        for d in st:
            d["res"] = _mm2(d["t"], d["x"])
        for d in st:
            d["x"] = d["res"]
        for d in st:
            kdt = kt_ref[0, d["h"], d["c"]] * jnp.exp(d["glast"] - d["grow"])
            d["res"] = _mm2(jnp.concatenate([d["qkd"], kdt], axis=0), d["x"])
        for d in st:
            top = jnp.concatenate([d["q"] * d["egc"], jnp.where(eye, jnp.exp(d["glast"]), 0.0)], axis=0)
            lhs_s[d["c"], d["h"]] = top - d["res"][:, hd:]
            add_s[d["c"], d["h"]] = d["res"][:, :hd]
        return carry

    lax.fori_loop(0, nchunk // chunks_per_iter, chunk_a, 0)

    onw = onw_ref[...]

    state = [s_s[h] for h in range(nheads)]
    for c in range(nchunk):
        res = [_mm3(lhs_s[c, h], state[h]) + add_s[c, h] for h in range(nheads)]
        for h in range(nheads):
            o = res[h][:c_len]
            state[h] = res[h][c_len:]
            o = o * lax.rsqrt(jnp.mean(o * o, axis=-1, keepdims=True) + RMS_EPS) * onw
            o_ref[0, h, c * c_len:(c + 1) * c_len, :] = o
    for h in range(nheads):
        s_s[h] = state[h]
    sout_ref[0] = s_s[...]


def _gdn(qh, kh, vh, kt, gbc, gbr, s0, onw, nchunk):
    b, nheads, t, hd = qh.shape
    tb = nchunk * GDN_CHUNK
    nt = t // tb
    head = pl.BlockSpec((1, nheads, tb, hd), lambda bi, ti: (bi, 0, ti, 0))
    state = pl.BlockSpec((1, nheads, hd, hd), lambda bi, ti: (bi, 0, 0, 0))
    return pl.pallas_call(
        functools.partial(_gdn_kernel, nheads=nheads, nchunk=nchunk, chunks_per_iter=min(4, nchunk)),
        grid=(b, nt),
        in_specs=[head, head, head,
                  pl.BlockSpec((1, nheads, nchunk, hd, GDN_CHUNK), lambda bi, ti: (bi, 0, ti, 0, 0)),
                  pl.BlockSpec((1, tb, 8), lambda bi, ti: (bi, ti, 0)),
                  pl.BlockSpec((1, nchunk, 8, GDN_CHUNK), lambda bi, ti: (bi, ti, 0, 0)),
                  state, _const_spec((1, hd))],
        out_specs=[head, state],
        out_shape=[jax.ShapeDtypeStruct((b, nheads, t, hd), F32),
                   jax.ShapeDtypeStruct((b, nheads, hd, hd), F32)],
        scratch_shapes=[
            pltpu.VMEM((nheads, hd, hd), F32),
            pltpu.VMEM((nchunk, nheads, GDN_CHUNK + hd, hd), F32),
            pltpu.VMEM((nchunk, nheads, GDN_CHUNK + hd, hd), F32),
        ],
        compiler_params=_cparams("parallel", "arbitrary"),
        name="deltanet",
    )(qh, kh, vh, kt, gbc, gbr, s0, onw)


def _out_kernel(x_ref, gate_ref, att_ref, za_ref, yb_ref, yc_ref, zc_ref, w_ref, o_ref):
    if len(yc_ref.shape) == 3:
        yc = jnp.concatenate([yc_ref[h] for h in range(yc_ref.shape[0])], axis=-1)
    else:
        yc = yc_ref[...]
    f32 = lambda ref: ref[...].astype(F32)
    cat = jnp.concatenate([(f32(za_ref) * f32(att_ref)).astype(BF16), yb_ref[...].astype(BF16),
                           (yc * f32(zc_ref)).astype(BF16)], axis=-1)
    y = jnp.dot(cat, w_ref[...], preferred_element_type=F32)
    o_ref[...] = x_ref[...] + gate_ref[...] * y


def _out_proj(x2, gate2, att2, za2, yb2, yc, zc2, wout, rows_per_gate):
    rows, d = x2.shape
    tr = min(ROW_TILE, rows)
    wa, wb = att2.shape[1], yb2.shape[1]
    row = lambda w: pl.BlockSpec((tr, w), lambda i: (i, 0))
    if rows_per_gate == 1:
        gate_spec = row(d)
    else:
        per = rows_per_gate // tr
        gate2 = gate2.reshape(gate2.shape[0], 1, d)
        gate_spec = pl.BlockSpec((None, 1, d), lambda i: (i // per, 0, 0))
    if yc.ndim == 4:
        per_seq = yc.shape[2] // tr
        yc_spec = pl.BlockSpec((None, yc.shape[1], tr, yc.shape[3]), lambda i: (i // per_seq, 0, i % per_seq, 0))
    else:
        yc_spec = row(wb)
    return pl.pallas_call(
        _out_kernel,
        grid=(rows // tr,),
        in_specs=[row(d), gate_spec, row(wa), row(wa), row(wb), yc_spec, row(wb), _const_spec((d, d))],
        out_specs=row(d),
        out_shape=jax.ShapeDtypeStruct((rows, d), F32),
        compiler_params=_cparams("parallel"),
        name="out_proj",
    )(x2, gate2, att2, za2, yb2, yc, zc2, wout)


def _layer_weights(l, norm_w, w_in, q_norm_w, k_norm_w, conv_b_w, conv_c_w, a_log, dt_bias, o_norm_w,
                   w_out, nseq, ts):
    d = w_in.shape[1]
    wa, wb = d // 2, d // 4
    nheads_a = wa // HEAD_DIM
    hc = wb // HEAD_DIM
    n_main = 4 * wa + 4 * wb + 4 * wb
    w = w_in[l]
    wcat = jnp.concatenate([w[:, :n_main], jnp.pad(w[:, n_main:], ((0, 0), (0, 128 - 2 * hc)))], axis=1)
    r = np.arange(ROW_TILE)
    cum_prompt = ((r[:, None] // GDN_CHUNK == r[None, :] // GDN_CHUNK) & (r[:, None] >= r[None, :]))
    rs = np.arange(nseq * ts)
    cum_sample = ((rs[:, None] % nseq == rs[None, :] % nseq) & (rs[:, None] >= rs[None, :]))
    return dict(
        norm_w=norm_w[l].reshape(1, d),
        wcat=wcat.astype(BF16),
        qw=jnp.tile(q_norm_w[l], nheads_a).reshape(1, wa),
        kw=jnp.tile(k_norm_w[l], nheads_a).reshape(1, wa),
        conv_b_w=conv_b_w[l], conv_c_w=conv_c_w[l],
        alog8=jnp.concatenate([jnp.zeros((hc,), F32), a_log[l]]).reshape(1, 2 * hc),
        dtb8=jnp.concatenate([jnp.zeros((hc,), F32), dt_bias[l]]).reshape(1, 2 * hc),
        cum_prompt=jnp.asarray(cum_prompt, F32),
        cum_sample=jnp.asarray(cum_sample, F32),
        onw=o_norm_w[l].reshape(1, HEAD_DIM),
        wout=w_out[l].astype(BF16),
    )


def _heads(a, nheads):
    b, t, _ = a.shape
    return a.reshape(b, t, nheads, HEAD_DIM).transpose(0, 2, 1, 3)


def _gdn_layout(qc, kc, vc, gb, hc):
    b, t, _ = qc.shape
    nch = t // GDN_CHUNK
    kt = kc.reshape(b, nch, GDN_CHUNK, hc, HEAD_DIM).transpose(0, 3, 1, 4, 2)
    gbr = gb.reshape(b, nch, GDN_CHUNK, 2 * hc).transpose(0, 1, 3, 2)
    return _heads(qc, hc), _heads(kc, hc), _heads(vc, hc), kt, gb, gbr


def kernel(x_prompt, x_sample, c_prompt, c_sample, cache_k, cache_v, page_table, state_conv_b,
           state_conv_c, state_delta, norm_w, ada_w, ada_b, w_in, q_norm_w, k_norm_w, rel_bias,
           conv_b_w, conv_c_w, a_log, dt_bias, o_norm_w, w_out):
    bp, tp, d = x_prompt.shape
    bs, ts, _ = x_sample.shape
    depth = ada_w.shape[0]
    wa, wb = d // 2, d // 4
    ha, hc = wa // HEAD_DIM, wb // HEAD_DIM
    page = cache_k.shape[2]
    past_len = page_table.shape[1] * page
    nblk_p = tp // MOBA_BLOCK
    nblk_s = past_len // MOBA_BLOCK
    assert tp % ROW_TILE == 0 and ROW_TILE == MOBA_BLOCK and past_len % MOBA_BLOCK == 0
    assert MOBA_BLOCK % page == 0 and page_table.shape[1] % SAMPLE_PAGES_PER_STEP == 0 and ts <= GDN_CHUNK
    assert SAMPLE_PAGES_PER_STEP % (MOBA_BLOCK // page) == 0 and page == 128

    mod = _modulation(jnp.concatenate([c_prompt, c_sample], axis=0), ada_w, ada_b)

    tab_h = rel_bias.T.astype(F32)
    kq = np.arange(MOBA_BLOCK)
    d_own = kq[None, :] - kq[:, None]
    dist_p = np.stack([np.broadcast_to(d_own + o, (ha, MOBA_BLOCK, MOBA_BLOCK)) for o in (0, MOBA_BLOCK)])
    bt = _bias_tiles(jnp.asarray(dist_p.reshape(-1, MOBA_BLOCK), jnp.int32),
                     jnp.repeat(jnp.tile(tab_h, (2, 1)), MOBA_BLOCK, axis=0))
    bt = bt.reshape(2, ha, MOBA_BLOCK, MOBA_BLOCK)
    cfar_p = bt[1, :, 0, MOBA_BLOCK - 1]

    tq = np.repeat(np.arange(ts), ha)
    pos_q = past_len + tq
    d_last = pos_q[:, None] - ((nblk_s - 1) * MOBA_BLOCK + kq[None, :])
    own_w = 128
    tk = np.arange(own_w)
    d_own_s = np.where(tk[None, :] < ts, tq[:, None] - tk[None, :], -1)
    d_far = np.broadcast_to(pos_q[:, None] - (nblk_s - 2) * MOBA_BLOCK - (MOBA_BLOCK - 1), (ts * ha, 128))
    tab_s = jnp.tile(tab_h, (ts, 1))
    blast = _bias_tiles(jnp.asarray(d_last, jnp.int32), tab_s)
    bown = _bias_tiles(jnp.asarray(d_own_s, jnp.int32), tab_s)
    cfar_s = _bias_tiles(jnp.asarray(d_far, jnp.int32), tab_s)[:, :1]

    cache_kt = cache_k.transpose(0, 1, 3, 4, 2)
    cache_vt = cache_v.transpose(0, 1, 3, 4, 2)

    hp = x_prompt
    hs_tm = x_sample.transpose(1, 0, 2).reshape(ts * bs, d)
    outs = [[] for _ in range(10)]
    for l in range(depth):
        lw = _layer_weights(l, norm_w, w_in, q_norm_w, k_norm_w, conv_b_w, conv_c_w, a_log, dt_bias,
                            o_norm_w, w_out, bs, ts)
        mod_p, mod_s = mod[l, :bp], mod[l, bp:]

        (q, k, v, kb, vt, kmean, za, yb, cbs, qh, kh, vh, kt, zc, gb, gbr, ccs) = _proj_prompt(hp, mod_p, lw)
        att = _attn_prompt(q, kb, vt, kmean.reshape(bp, nblk_p, wa), bt, cfar_p)
        nchunk_p = 8
        yc, s_new = _gdn(qh, kh, vh, kt, gb, gbr, jnp.zeros((bp, hc, HEAD_DIM, HEAD_DIM), F32),
                         lw["onw"], nchunk_p)
        hp = _out_proj(hp.reshape(bp * tp, d), mod_p[:, 2 * d:], att.reshape(bp * tp, wa),
                       za.reshape(bp * tp, wa), yb.reshape(bp * tp, wb), yc, zc.reshape(bp * tp, wb),
                       lw["wout"], tp).reshape(bp, tp, d)
        outs[0].append(k.reshape(bp, tp, ha, HEAD_DIM)); outs[1].append(v.reshape(bp, tp, ha, HEAD_DIM))
        outs[4].append(cbs); outs[6].append(ccs); outs[8].append(s_new)

        rep = lambda a: jnp.tile(a, (ts, 1))
        hist_b = state_conv_b[l].transpose(1, 0, 2).reshape(-1, wb)
        hist_c = state_conv_c[l].transpose(1, 0, 2).reshape(-1, 3 * wb)
        (q, k, v, za, yb, cbs, qc, kc, vc, zc, gb, ccs) = _proj_sample(
            hs_tm, rep(mod_s[:, :d]), rep(mod_s[:, d:2 * d]), lw, hist_b, hist_c, bs)
        seq_major = lambda a: a.reshape(ts, bs, -1).transpose(1, 0, 2)
        qt = seq_major(q).transpose(0, 2, 1)
        new_t = lambda a: jnp.pad(seq_major(a), ((0, 0), (0, own_w - ts), (0, 0))).transpose(0, 2, 1)
        scores, idx = _sample_scores(jnp.repeat(seq_major(q), ha, axis=1), cache_kt, page_table, l)
        att_t = _sample_attend(scores, idx[:, :, :MOBA_TOPK].reshape(bs, -1), qt, new_t(k), new_t(v),
                               cache_vt, page_table, l, blast, cfar_s, bown)
        att = att_t[:, :, :ts].transpose(2, 0, 1).reshape(ts * bs, wa)
        pad_t = lambda a: jnp.pad(seq_major(a), ((0, 0), (0, GDN_CHUNK - ts), (0, 0)))
        gbs = seq_major(gb)
        gb_pad = jnp.concatenate(
            [pad_t(gb[:, :hc]), jnp.pad(gbs[:, :, hc:], ((0, 0), (0, GDN_CHUNK - ts), (0, 0)), mode="edge")],
            axis=-1)
        yc, s_new = _gdn(*_gdn_layout(pad_t(qc), pad_t(kc), pad_t(vc), gb_pad, hc), state_delta[l],
                         lw["onw"], 1)
        yc = yc[:, :, :ts, :].transpose(2, 0, 1, 3).reshape(ts * bs, wb)
        hs_tm = _out_proj(hs_tm, rep(mod_s[:, 2 * d:]), att, za, yb, yc, zc, lw["wout"], 1)
        outs[2].append(seq_major(k).reshape(bs, ts, ha, HEAD_DIM))
        outs[3].append(seq_major(v).reshape(bs, ts, ha, HEAD_DIM))
        outs[5].append(cbs.reshape(-1, bs, wb).transpose(1, 0, 2))
        outs[7].append(ccs.reshape(-1, bs, 3 * wb).transpose(1, 0, 2))
        outs[9].append(s_new)

    y_sample = hs_tm.reshape(ts, bs, d).transpose(1, 0, 2)
    return (hp, y_sample) + tuple(jnp.stack(o) for o in outs)
```

```python
import functools
import math

import jax
import jax.numpy as jnp
import numpy as np
from jax import lax
from jax.experimental import pallas as pl
from jax.experimental.pallas import tpu as pltpu

F32 = jnp.float32
BF16 = jnp.bfloat16
HIGHEST = lax.Precision.HIGHEST

HEAD_DIM = 64
MOBA_BLOCK = 256
MOBA_TOPK = 3
GDN_CHUNK = 64
VT_PAD = 16
VT_ROWS = HEAD_DIM + VT_PAD
QK_AHEAD = 4
LOG2E = math.log2(math.e)
INV_BASE = 8
REL_BUCKETS = 32
REL_MAX_DIST = 128
RMS_EPS = 1e-6
ROW_TILE = 256
OUT_TILE = 512
VMEM_LIMIT = 56 * 1024 * 1024
NEG_INF = float("-inf")

_NT = (((1,), (1,)), ((), ()))


def _cparams(*sem):
    return pltpu.CompilerParams(dimension_semantics=sem, vmem_limit_bytes=VMEM_LIMIT)


def _const_spec(shape):
    zeros = (0,) * len(shape)
    return pl.BlockSpec(shape, lambda *_: zeros)


def _silu(x):
    return x * jax.nn.sigmoid(x)


def _softplus(x):
    return jnp.maximum(x, 0.0) + jnp.log(1.0 + jnp.exp(-jnp.abs(x)))


def _split(a):
    hi = a.astype(BF16)
    lo = (a - hi.astype(F32)).astype(BF16)
    return hi, lo


def _dot(a, b, dims=None):
    if dims is None:
        return jnp.dot(a, b, preferred_element_type=F32)
    return lax.dot_general(a, b, dims, preferred_element_type=F32)


def _mm1(a, b, dims=None):
    return _dot(a.astype(BF16), b.astype(BF16), dims)


def _mm2(a, b, dims=None):
    ah = a.astype(BF16)
    bh, bl = _split(b)
    return _dot(ah, bh, dims) + _dot(ah, bl, dims)


def _mm3(a, b, dims=None):
    ah, al = _split(a)
    bh, bl = _split(b)
    return _dot(ah, bh, dims) + _dot(al, bh, dims) + _dot(ah, bl, dims)


def _group_sum(x):
    lane = lax.broadcasted_iota(jnp.int32, (1, 2 * HEAD_DIM), 1)
    first = lane < HEAD_DIM
    outs = []
    for c in range(x.shape[1] // (2 * HEAD_DIM)):
        xc = x[:, c * 2 * HEAD_DIM:(c + 1) * 2 * HEAD_DIM]
        s0 = jnp.sum(jnp.where(first, xc, 0.0), axis=-1, keepdims=True)
        s1 = jnp.sum(jnp.where(first, 0.0, xc), axis=-1, keepdims=True)
        outs.append(jnp.where(first, s0, s1))
    return jnp.concatenate(outs, axis=1)


def _mod_kernel(c_ref, w_ref, b_ref, o_ref):
    o_ref[0] = jnp.dot(c_ref[...], w_ref[0], precision=HIGHEST, preferred_element_type=F32) + b_ref[0]


def _modulation(c_all, ada_w, ada_b):
    depth, d, n3 = ada_w.shape
    rows = c_all.shape[0]
    tn = 512
    return pl.pallas_call(
        _mod_kernel,
        grid=(depth, n3 // tn),
        in_specs=[
            pl.BlockSpec((rows, d), lambda l, n: (0, 0)),
            pl.BlockSpec((1, d, tn), lambda l, n: (l, 0, n)),
            pl.BlockSpec((1, 1, tn), lambda l, n: (l, 0, n)),
        ],
        out_specs=pl.BlockSpec((1, rows, tn), lambda l, n: (l, 0, n)),
        out_shape=jax.ShapeDtypeStruct((depth, rows, n3), F32),
        compiler_params=_cparams("parallel", "parallel"),
        name="modulation",
    )(c_all, ada_w, ada_b.reshape(depth, 1, n3))


def _bias_kernel(dist_ref, tab_ref, o_ref):
    dist = dist_ref[...]
    n = jnp.maximum(dist, 0)
    max_exact = REL_BUCKETS // 2
    nf = jnp.maximum(n, max_exact).astype(F32)
    large = max_exact + (jnp.log(nf / max_exact) / math.log(REL_MAX_DIST / max_exact)
                         * (REL_BUCKETS - max_exact)).astype(jnp.int32)
    bucket = jnp.where(n < max_exact, n, jnp.minimum(large, REL_BUCKETS - 1))
    tab = tab_ref[...]
    acc = jnp.zeros(dist.shape, F32)
    for b in range(REL_BUCKETS):
        acc = jnp.where(bucket == b, tab[:, b:b + 1], acc)
    o_ref[...] = jnp.where(dist >= 0, acc, NEG_INF)


def _bias_tiles(dist, tab):
    rows, cols = dist.shape
    tr = min(rows, 256)
    return pl.pallas_call(
        _bias_kernel,
        grid=(rows // tr,),
        in_specs=[pl.BlockSpec((tr, cols), lambda i: (i, 0)),
                  pl.BlockSpec((tr, REL_BUCKETS), lambda i: (i, 0))],
        out_specs=pl.BlockSpec((tr, cols), lambda i: (i, 0)),
        out_shape=jax.ShapeDtypeStruct((rows, cols), F32),
        compiler_params=_cparams("parallel"),
        name="bias_tiles",
    )(dist, tab)


def _proj_common(x, shift, scale, normw, w_ref, qw, kw, alog, dtb, cum_ref):
    ms = jnp.mean(x * x, axis=-1, keepdims=True)
    hn = (x * lax.rsqrt(ms + RMS_EPS) * normw) * (1.0 + scale) + shift
    z = jnp.dot(hn.astype(BF16), w_ref[...], preferred_element_type=F32)
    wa = qw.shape[-1]
    wb = wa // 2
    o = 0
    q = z[:, o:o + wa]; o += wa
    k = z[:, o:o + wa]; o += wa
    v = z[:, o:o + wa]; o += wa
    za = z[:, o:o + wa]; o += wa
    hb = z[:, o:o + wb]; o += wb
    bg = z[:, o:o + wb]; o += wb
    cg = z[:, o:o + wb]; o += wb
    zb = z[:, o:o + wb]; o += wb
    qkv = z[:, o:o + 3 * wb]; o += 3 * wb
    zc = z[:, o:o + wb]; o += wb
    z8 = z[:, o:o + 8]

    qn = q * lax.rsqrt(_group_sum(q * q) * (1.0 / HEAD_DIM) + RMS_EPS) * qw
    kn = k * lax.rsqrt(_group_sum(k * k) * (1.0 / HEAD_DIM) + RMS_EPS) * kw

    lane8 = lax.broadcasted_iota(jnp.int32, z8.shape, 1)
    beta = jax.nn.sigmoid(z8)
    g = -jnp.exp(alog) * _softplus(z8 + dtb)
    bg8 = jnp.where(lane8 < 4, beta, g)
    cum = jnp.dot(cum_ref[...], bg8, precision=HIGHEST, preferred_element_type=F32)
    gb = jnp.where(lane8 < 4, bg8, cum)
    return dict(q=qn * (HEAD_DIM ** -0.5), k=kn, v=v, za=_silu(za), u=cg * hb, zbg=_silu(zb) * bg,
                qkv=qkv, zc=_silu(zc), gb=gb)


def _gdn_qkv(y):
    wb = y.shape[-1] // 3
    y = _silu(y)
    qc, kc, vc = y[:, :wb], y[:, wb:2 * wb], y[:, 2 * wb:]
    qc = qc * lax.rsqrt(_group_sum(qc * qc) + RMS_EPS) * (HEAD_DIM ** -0.5)
    kc = kc * lax.rsqrt(_group_sum(kc * kc) + RMS_EPS)
    return qc, kc, vc


def _proj_prompt_kernel(x_ref, mod_ref, normw_ref, w_ref, qw_ref, kw_ref,
                        cbw_ref, ccw_ref, alog_ref, dtb_ref, cum_ref,
                        q_ref, k_ref, v_ref, kb_ref, vt_ref, km_ref, za_ref, yb_ref, cbs_ref,
                        qc_ref, kc_ref, vc_ref, kt_ref, zc_ref, gb_ref, gbr_ref, ccs_ref,
                        carry_b, carry_c):
    ti = pl.program_id(1)
    d = x_ref.shape[-1]
    rows = x_ref.shape[1]

    @pl.when(ti == 0)
    def _():
        carry_b[...] = jnp.zeros_like(carry_b)
        carry_c[...] = jnp.zeros_like(carry_c)

    mod = mod_ref[0]
    p = _proj_common(x_ref[0], mod[:, :d], mod[:, d:2 * d], normw_ref[...], w_ref, qw_ref[...],
                     kw_ref[...], alog_ref[...], dtb_ref[...], cum_ref)
    q_ref[0] = p["q"]
    k_ref[0] = p["k"]
    v_ref[0] = p["v"]
    km_ref[0, 0] = jnp.mean(p["k"], axis=0, keepdims=True)
    za_ref[0] = p["za"].astype(za_ref.dtype)
    zc_ref[0] = p["zc"].astype(zc_ref.dtype)
    gb = p["gb"]
    gb_ref[0] = gb
    kb_ref[0, 0] = p["k"].astype(BF16)
    vtb = p["v"].T.astype(BF16)
    ones = jnp.ones((VT_PAD, rows), BF16)
    pieces = []
    for h in range(vtb.shape[0] // HEAD_DIM):
        pieces += [vtb[h * HEAD_DIM:(h + 1) * HEAD_DIM], ones]
    vt_ref[0, 0] = jnp.concatenate(pieces, axis=0)
    ncol = gb.shape[1]
    eye8 = jnp.where(lax.broadcasted_iota(jnp.int32, (ncol, ncol), 0)
                     == lax.broadcasted_iota(jnp.int32, (ncol, ncol), 1), 1.0, 0.0)
    gbr = lax.dot_general(eye8, gb, _NT, precision=HIGHEST, preferred_element_type=F32)
    for cc in range(rows // GDN_CHUNK):
        gbr_ref[0, cc] = gbr[:, cc * GDN_CHUNK:(cc + 1) * GDN_CHUNK]

    u = p["u"]
    ext = jnp.concatenate([carry_b[...], u], axis=0)
    cbw = cbw_ref[...]
    nb = cbw.shape[0]
    yb = u * cbw[nb - 1:nb]
    for j in range(nb - 1):
        s = nb - 1 - j
        yb = yb + ext[8 - s:8 - s + rows] * cbw[j:j + 1]
    yb_ref[0] = (p["zbg"] * yb).astype(yb_ref.dtype)
    carry_b[...] = u[rows - 8:]
    cbs_ref[0] = u[rows - (nb - 1):]

    qkv = p["qkv"]
    extc = jnp.concatenate([carry_c[...], qkv], axis=0)
    ccw = ccw_ref[...]
    nc = ccw.shape[0]
    yc = qkv * ccw[nc - 1:nc]
    for j in range(nc - 1):
        s = nc - 1 - j
        yc = yc + extc[8 - s:8 - s + rows] * ccw[j:j + 1]
    carry_c[...] = qkv[rows - 8:]
    ccs_ref[0] = qkv[rows - (nc - 1):]
    qc, kc, vc = _gdn_qkv(yc)
    kct = kc.T
    for h in range(qc.shape[1] // HEAD_DIM):
        hs = slice(h * HEAD_DIM, (h + 1) * HEAD_DIM)
        qc_ref[0, h] = qc[:, hs]
        kc_ref[0, h] = kc[:, hs]
        vc_ref[0, h] = vc[:, hs]
        for cc in range(rows // GDN_CHUNK):
            kt_ref[0, h, cc] = kct[hs, cc * GDN_CHUNK:(cc + 1) * GDN_CHUNK]


def _proj_sample_kernel(x_ref, shift_ref, scale_ref, normw_ref, w_ref, qw_ref, kw_ref,
                        cbw_ref, ccw_ref, alog_ref, dtb_ref, cum_ref, hb_ref, hc_ref,
                        q_ref, k_ref, v_ref, za_ref, yb_ref, cbs_ref,
                        qc_ref, kc_ref, vc_ref, zc_ref, gb_ref, ccs_ref, *, nseq):
    rows = x_ref.shape[0]
    p = _proj_common(x_ref[...], shift_ref[...], scale_ref[...], normw_ref[...], w_ref, qw_ref[...],
                     kw_ref[...], alog_ref[...], dtb_ref[...], cum_ref)
    q_ref[...] = p["q"]
    k_ref[...] = p["k"]
    v_ref[...] = p["v"]
    za_ref[...] = p["za"]
    zc_ref[...] = p["zc"]
    gb_ref[...] = p["gb"]

    cbw = cbw_ref[...]
    nb = cbw.shape[0]
    ext = jnp.concatenate([hb_ref[...], p["u"]], axis=0)
    yb = ext[0:rows] * cbw[0:1]
    for j in range(1, nb):
        yb = yb + ext[j * nseq:j * nseq + rows] * cbw[j:j + 1]
    yb_ref[...] = p["zbg"] * yb
    cbs_ref[...] = ext[rows:]

    ccw = ccw_ref[...]
    nc = ccw.shape[0]
    extc = jnp.concatenate([hc_ref[...], p["qkv"]], axis=0)
    yc = extc[0:rows] * ccw[0:1]
    for j in range(1, nc):
        yc = yc + extc[j * nseq:j * nseq + rows] * ccw[j:j + 1]
    ccs_ref[...] = extc[rows:]
    qc, kc, vc = _gdn_qkv(yc)
    qc_ref[...] = qc
    kc_ref[...] = kc
    vc_ref[...] = vc


def _proj_prompt(x, mod, lw):
    b, t, d = x.shape
    wcat = lw["wcat"]
    ncat = wcat.shape[1]
    wa, wb = d // 2, d // 4
    nt = t // ROW_TILE
    row = lambda w: pl.BlockSpec((1, ROW_TILE, w), lambda bi, ti: (bi, ti, 0))
    per_b = lambda r, w: pl.BlockSpec((1, r, w), lambda bi, ti: (bi, 0, 0))
    nb, nc = lw["conv_b_w"].shape[0], lw["conv_c_w"].shape[0]
    hc = wb // HEAD_DIM
    cpt = ROW_TILE // GDN_CHUNK
    nch = t // GDN_CHUNK
    out_shape = [
        jax.ShapeDtypeStruct((b, t, wa), F32),
        jax.ShapeDtypeStruct((b, t, wa), F32),
        jax.ShapeDtypeStruct((b, t, wa), F32),
        jax.ShapeDtypeStruct((b, nt, ROW_TILE, wa), BF16),
        jax.ShapeDtypeStruct((b, nt, wa // HEAD_DIM * VT_ROWS, ROW_TILE), BF16),
        jax.ShapeDtypeStruct((b, nt, 1, wa), F32),
        jax.ShapeDtypeStruct((b, t, wa), BF16),
        jax.ShapeDtypeStruct((b, t, wb), BF16),
        jax.ShapeDtypeStruct((b, nb - 1, wb), F32),
        jax.ShapeDtypeStruct((b, hc, t, HEAD_DIM), F32),
        jax.ShapeDtypeStruct((b, hc, t, HEAD_DIM), F32),
        jax.ShapeDtypeStruct((b, hc, t, HEAD_DIM), F32),
        jax.ShapeDtypeStruct((b, hc, nch, HEAD_DIM, GDN_CHUNK), F32),
        jax.ShapeDtypeStruct((b, t, wb), BF16),
        jax.ShapeDtypeStruct((b, t, 8), F32),
        jax.ShapeDtypeStruct((b, nch, 8, GDN_CHUNK), F32),
        jax.ShapeDtypeStruct((b, nc - 1, 3 * wb), F32),
    ]
    blk4 = lambda r, c: pl.BlockSpec((1, 1, r, c), lambda bi, ti: (bi, ti, 0, 0))
    heads = pl.BlockSpec((1, hc, ROW_TILE, HEAD_DIM), lambda bi, ti: (bi, 0, ti, 0))
    out_specs = [
        row(wa), row(wa), row(wa), blk4(ROW_TILE, wa), blk4(wa // HEAD_DIM * VT_ROWS, ROW_TILE), blk4(1, wa),
        row(wa), row(wb), per_b(nb - 1, wb),
        heads, heads, heads,
        pl.BlockSpec((1, hc, cpt, HEAD_DIM, GDN_CHUNK), lambda bi, ti: (bi, 0, ti, 0, 0)),
        row(wb), row(8),
        pl.BlockSpec((1, cpt, 8, GDN_CHUNK), lambda bi, ti: (bi, ti, 0, 0)),
        per_b(nc - 1, 3 * wb),
    ]
    in_specs = [
        row(d),
        pl.BlockSpec((1, 1, 3 * d), lambda bi, ti: (bi, 0, 0)),
        _const_spec((1, d)),
        _const_spec((d, ncat)),
        _const_spec((1, wa)), _const_spec((1, wa)),
        _const_spec((nb, wb)), _const_spec((nc, 3 * wb)),
        _const_spec((1, 8)), _const_spec((1, 8)),
        _const_spec((ROW_TILE, ROW_TILE)),
    ]
    return pl.pallas_call(
        _proj_prompt_kernel,
        grid=(b, nt),
        in_specs=in_specs,
        out_specs=out_specs,
        out_shape=out_shape,
        scratch_shapes=[pltpu.VMEM((8, wb), F32), pltpu.VMEM((8, 3 * wb), F32)],
        compiler_params=_cparams("parallel", "arbitrary"),
        name="proj_prompt",
    )(x, mod.reshape(b, 1, 3 * d), lw["norm_w"], wcat, lw["qw"], lw["kw"],
      lw["conv_b_w"], lw["conv_c_w"], lw["alog8"], lw["dtb8"], lw["cum_prompt"])


def _proj_sample(x_tm, shift, scale, lw, hist_b, hist_c, nseq):
    rows, d = x_tm.shape
    wa, wb = d // 2, d // 4
    nb, nc = lw["conv_b_w"].shape[0], lw["conv_c_w"].shape[0]
    out_shape = [
        jax.ShapeDtypeStruct((rows, wa), F32), jax.ShapeDtypeStruct((rows, wa), F32),
        jax.ShapeDtypeStruct((rows, wa), F32), jax.ShapeDtypeStruct((rows, wa), F32),
        jax.ShapeDtypeStruct((rows, wb), F32),
        jax.ShapeDtypeStruct(((nb - 1) * nseq, wb), F32),
        jax.ShapeDtypeStruct((rows, wb), F32), jax.ShapeDtypeStruct((rows, wb), F32),
        jax.ShapeDtypeStruct((rows, wb), F32), jax.ShapeDtypeStruct((rows, wb), F32),
        jax.ShapeDtypeStruct((rows, 8), F32),
        jax.ShapeDtypeStruct(((nc - 1) * nseq, 3 * wb), F32),
    ]
    return pl.pallas_call(
        functools.partial(_proj_sample_kernel, nseq=nseq),
        out_shape=out_shape,
        compiler_params=pltpu.CompilerParams(vmem_limit_bytes=VMEM_LIMIT),
        name="proj_sample",
    )(x_tm, shift, scale, lw["norm_w"], lw["wcat"], lw["qw"], lw["kw"],
      lw["conv_b_w"], lw["conv_c_w"], lw["alog8"], lw["dtb8"], lw["cum_sample"], hist_b, hist_c)


def _attn_prompt_kernel(cfar_ref, q_ref, kb_ref, vt_ref, km_ref, bt_ref, o_ref,
                        sel_s, acc_s, *, nheads, nblk):
    i = pl.program_id(1)
    blk = q_ref.shape[1]
    qt = q_ref[0].T * LOG2E
    qtb = qt.astype(BF16)
    lane = lax.broadcasted_iota(jnp.int32, (1, 2 * HEAD_DIM), 1)
    sub = lax.broadcasted_iota(jnp.int32, (2 * HEAD_DIM, 1), 0)
    lane_half = [lane < HEAD_DIM, lane >= HEAD_DIM]
    sub_half = [sub < HEAD_DIM, sub >= HEAD_DIM]

    km = km_ref[0]
    n = lax.broadcasted_iota(jnp.int32, (nblk, blk), 0)
    n_f = n.astype(F32)
    past = n < i
    for h in range(nheads):
        p, par = divmod(h, 2)
        kmp = jnp.where(lane_half[par], km[:, p * 128:(p + 1) * 128], 0.0)
        gate = jnp.dot(kmp, qt[p * 128:(p + 1) * 128, :], precision=HIGHEST, preferred_element_type=F32)
        gate = jnp.where(past, gate, NEG_INF)
        picked = jnp.zeros((nblk, blk), F32)
        for _ in range(MOBA_TOPK):
            top = jnp.max(gate, axis=0, keepdims=True)
            first = jnp.min(jnp.where(gate == top, n_f, float(nblk)), axis=0, keepdims=True)
            chosen = n_f == first
            picked = jnp.where(chosen, 1.0, picked)
            gate = jnp.where(chosen, NEG_INF, gate)
        sel_s[h * nblk:(h + 1) * nblk, :] = jnp.where(past, picked, 0.0)

    qm = []
    for h in range(nheads):
        p, par = divmod(h, 2)
        qm.append(jnp.where(sub_half[par], qtb[p * 128:(p + 1) * 128, :], jnp.zeros((), BF16)))

    def scores(kblk, h):
        p = h // 2
        return jnp.dot(kblk[:, p * 128:(p + 1) * 128], qm[h], preferred_element_type=F32)

    def pv_dot(vblk, h, pr):
        res = jnp.dot(vblk[h * VT_ROWS:(h + 1) * VT_ROWS, :], pr.astype(BF16), preferred_element_type=F32)
        return res[:HEAD_DIM], res[HEAD_DIM:HEAD_DIM + 1]


    def block(js, kind, m_all, l_all):
        nb = len(js)
        kblk = kb_ref[0, js[0]] if nb == 1 else jnp.concatenate([kb_ref[0, j] for j in js], axis=0)
        sts = [scores(kblk, h) for h in range(min(QK_AHEAD, nheads))]
        m_rows, l_rows = [], []
        pend = None

        def finish(h, alpha, ons, pvs):
            rs = slice(h * HEAD_DIM, (h + 1) * HEAD_DIM)
            if kind == "own":
                acc_s[rs, :] = pvs[0]
            else:
                new = alpha * acc_s[rs, :]
                for on, pv in zip(ons, pvs):
                    new = new + jnp.where(on, pv, 0.0)
                acc_s[rs, :] = new

        for h in range(nheads):
            st = sts[h]
            if h + QK_AHEAD < nheads:
                sts.append(scores(kblk, h + QK_AHEAD))
            if kind == "own":
                st = bt_ref[0, h] * LOG2E + st
                m_new = jnp.max(st, axis=0, keepdims=True)
                pv, l_new = pv_dot(vt_ref[0, js[0]], h, jnp.exp2(st - m_new))
                alpha, ons, pvs = None, None, [pv]
            else:
                if kind == "near":
                    st = bt_ref[1, h] * LOG2E + st
                    shift = 0.0
                else:
                    shift = cfar_ref[h] * LOG2E
                parts = [st[s * blk:(s + 1) * blk] for s in range(nb)]
                ons = [sel_s[pl.ds(h * nblk + j, 1), :] > 0.0 for j in js]
                m_old = m_all[h:h + 1, :]
                m_new = m_old
                for on, part in zip(ons, parts):
                    m_new = jnp.maximum(m_new, jnp.where(on, jnp.max(part, axis=0, keepdims=True) + shift, NEG_INF))
                alpha = jnp.exp2(m_old - m_new)
                l_new = alpha * l_all[h:h + 1, :]
                pvs = []
                for j, on, part in zip(js, ons, parts):
                    pv, psum = pv_dot(vt_ref[0, j], h, jnp.exp2(part - (m_new - shift)))
                    pvs.append(pv)
                    l_new = l_new + jnp.where(on, psum, 0.0)
            m_rows.append(m_new)
            l_rows.append(l_new)
            if pend is not None:
                finish(*pend)
            pend = (h, alpha, ons, pvs)
        finish(*pend)
        return jnp.concatenate(m_rows, axis=0), jnp.concatenate(l_rows, axis=0)

    keep = lambda m, l: (m, l)
    m_all, l_all = block([i], "own", None, None)
    m_all, l_all = lax.cond(i >= 1, lambda m, l: block([i - 1], "near", m, l), keep, m_all, l_all)
    nfar = i - 1
    odd = (nfar >= 1) & ((nfar & 1) == 1)
    m_all, l_all = lax.cond(odd, lambda m, l: block([nfar - 1], "far", m, l), keep, m_all, l_all)
    m_all, l_all = lax.fori_loop(0, nfar // 2, lambda jj, c: block([2 * jj, 2 * jj + 1], "far", *c), (m_all, l_all))

    for h in range(nheads):
        rs = slice(h * HEAD_DIM, (h + 1) * HEAD_DIM)
        acc_s[rs, :] = acc_s[rs, :] / l_all[h:h + 1, :]
    o_ref[0] = acc_s[...].T.astype(o_ref.dtype)


def _attn_prompt(q, kb, vt, kmean, bt, cfar):
    b, t, wa = q.shape
    nheads = wa // HEAD_DIM
    nblk = t // MOBA_BLOCK
    grid_spec = pltpu.PrefetchScalarGridSpec(
        num_scalar_prefetch=0,
        grid=(b, nblk),
        in_specs=[
            pl.BlockSpec(memory_space=pltpu.SMEM),
            pl.BlockSpec((1, MOBA_BLOCK, wa), lambda bi, i: (bi, i, 0)),
            pl.BlockSpec((1, nblk, MOBA_BLOCK, wa), lambda bi, i: (bi, 0, 0, 0)),
            pl.BlockSpec((1, nblk, nheads * VT_ROWS, MOBA_BLOCK), lambda bi, i: (bi, 0, 0, 0)),
            pl.BlockSpec((1, nblk, wa), lambda bi, i: (bi, 0, 0)),
            pl.BlockSpec((2, nheads, MOBA_BLOCK, MOBA_BLOCK), lambda bi, i: (0, 0, 0, 0)),
        ],
        out_specs=pl.BlockSpec((1, MOBA_BLOCK, wa), lambda bi, i: (bi, i, 0)),
        scratch_shapes=[
            pltpu.VMEM((nheads * nblk, MOBA_BLOCK), F32),
            pltpu.VMEM((wa, MOBA_BLOCK), F32),
        ],
    )
    return pl.pallas_call(
        functools.partial(_attn_prompt_kernel, nheads=nheads, nblk=nblk),
        grid_spec=grid_spec,
        out_shape=jax.ShapeDtypeStruct((b, t, wa), BF16),
        compiler_params=_cparams("parallel", "arbitrary"),
        name="attn_prompt",
    )(cfar, q, kb, vt, kmean, bt)


SAMPLE_PAGES_PER_STEP = 32


def _sample_scores_kernel(pt_ref, q_ref, *rest, nheads, nblk, page, tq):
    npg = SAMPLE_PAGES_PER_STEP
    k_refs = rest[:npg]
    s_ref, idx_ref, qhl_s, gate_s = rest[npg:]
    s = pl.program_id(1)
    nrows = tq * nheads
    wa = nheads * HEAD_DIM
    ppb = MOBA_BLOCK // page

    @pl.when(s == 0)
    def _():
        r = lax.broadcasted_iota(jnp.int32, (nrows, wa), 0)
        c = lax.broadcasted_iota(jnp.int32, (nrows, wa), 1)
        hi, lo = _split(jnp.where(c // HEAD_DIM == r % nheads, q_ref[0], 0.0))
        qhl_s[:nrows, :] = hi
        qhl_s[nrows:, :] = lo
        gate_s[...] = jnp.full(gate_s.shape, NEG_INF, F32)

    lane = lax.broadcasted_iota(jnp.int32, gate_s.shape, 1)
    qhl = qhl_s[...]
    bsum = None
    for e in range(npg):
        kh, kl = _split(k_refs[e][...].reshape(wa, page))
        both = jnp.dot(qhl, kh, preferred_element_type=F32)
        sp = both[:nrows] + both[nrows:] + jnp.dot(qhl[:nrows], kl, preferred_element_type=F32)
        s_ref[0, e // ppb, :, (e % ppb) * page:(e % ppb + 1) * page] = sp
        rs = jnp.sum(sp, axis=1, keepdims=True)
        bsum = rs if e % ppb == 0 else bsum + rs
        if e % ppb == ppb - 1:
            n = s * (npg // ppb) + e // ppb
            gate_s[...] = jnp.where(lane == n, bsum * (1.0 / MOBA_BLOCK), gate_s[...])

    @pl.when(s == pl.num_programs(1) - 1)
    def _():
        gate = gate_s[...]
        rank = jnp.zeros(gate.shape, F32)
        for sh in range(1, gate.shape[1]):
            other = pltpu.roll(gate, sh, 1)
            rank = rank + jnp.where(other > gate, 1.0, 0.0)
            rank = rank + jnp.where((other == gate) & (lane >= sh), 1.0, 0.0)
        lane_f = lane.astype(F32)
        out = jnp.zeros(gate.shape, F32)
        for j in range(MOBA_TOPK):
            pick = jnp.sum(jnp.where((rank == j) & (lane < nblk), lane_f, 0.0), axis=1, keepdims=True)
            out = jnp.where(lane == j, pick, out)
        idx_ref[0] = out.astype(jnp.int32)


def _sample_scores(q_rows, cache_kt, page_table, layer):
    nseq, nrows, wa = q_rows.shape
    nheads = wa // HEAD_DIM
    tq = nrows // nheads
    page = cache_kt.shape[-1]
    n_pages = page_table.shape[1]
    nblk = n_pages * page // MOBA_BLOCK
    npg = SAMPLE_PAGES_PER_STEP

    def page_spec(e):
        return pl.BlockSpec((None, None, nheads, HEAD_DIM, page),
                            lambda b, s, pt: (layer, pt[b, s * npg + e], 0, 0, 0))

    grid_spec = pltpu.PrefetchScalarGridSpec(
        num_scalar_prefetch=1,
        grid=(nseq, n_pages // npg),
        in_specs=[pl.BlockSpec((1, nrows, wa), lambda b, s, pt: (b, 0, 0))] + [page_spec(e) for e in range(npg)],
        out_specs=[pl.BlockSpec((1, npg * page // MOBA_BLOCK, nrows, MOBA_BLOCK), lambda b, s, pt: (b, s, 0, 0)),
                   pl.BlockSpec((1, nrows, 128), lambda b, s, pt: (b, 0, 0))],
        scratch_shapes=[pltpu.VMEM((2 * nrows, wa), BF16), pltpu.VMEM((nrows, 128), F32)],
    )
    return pl.pallas_call(
        functools.partial(_sample_scores_kernel, nheads=nheads, nblk=nblk, page=page, tq=tq),
        grid_spec=grid_spec,
        out_shape=[jax.ShapeDtypeStruct((nseq, nblk, nrows, MOBA_BLOCK), F32),
                   jax.ShapeDtypeStruct((nseq, nrows, 128), jnp.int32)],
        compiler_params=_cparams("parallel", "arbitrary"),
        name="sample_scores",
    )(page_table, q_rows, *([cache_kt] * npg))


def _sample_attend_kernel(pt_ref, idx_ref, s_ref, qt_ref, knt_ref, vnt_ref, blast_ref, cfar_ref, bown_ref,
                          cv_ref, o_ref, vbuf, sems, *, nheads, nblk, page, tq, layer):
    b = pl.program_id(0)
    nrows = tq * nheads
    ppb = MOBA_BLOCK // page

    slot = b % 2

    def slab_copies(seq, half):
        out = []
        for p in range(nrows):
            h = p % nheads
            for j in range(MOBA_TOPK):
                n = idx_ref[seq, p * MOBA_TOPK + j]
                for e in range(ppb):
                    pg = pt_ref[seq, n * ppb + e]
                    out.append(pltpu.make_async_copy(cv_ref.at[layer, pg, h], vbuf.at[half, p, j * ppb + e],
                                                     sems.at[half, p]))
        return out

    @pl.when(b == 0)
    def _():
        for cp in slab_copies(b, slot):
            cp.start()

    @pl.when(b + 1 < pl.num_programs(0))
    def _():
        for cp in slab_copies(b + 1, 1 - slot):
            cp.start()

    qt = qt_ref[0]
    own, past = [], [[] for _ in range(MOBA_TOPK)]
    for p in range(nrows):
        t, h = divmod(p, nheads)
        hs = slice(h * HEAD_DIM, (h + 1) * HEAD_DIM)
        qb = jnp.broadcast_to(qt[hs, t:t + 1], (HEAD_DIM, knt_ref.shape[2]))
        own.append(jnp.sum(knt_ref[0, hs, :] * qb, axis=0, keepdims=True) + bown_ref[p:p + 1, :])
        for j in range(MOBA_TOPK):
            n = idx_ref[b, p * MOBA_TOPK + j]
            bias = jnp.where(n == nblk - 1, blast_ref[p:p + 1, :], cfar_ref[p:p + 1, :])
            past[j].append(s_ref[0, n, p:p + 1, :] + bias)
    l_own = jnp.concatenate(own, axis=0)
    l_past = [jnp.concatenate(rows, axis=0) for rows in past]
    m = jnp.max(l_own, axis=1, keepdims=True)
    for lg in l_past:
        m = jnp.maximum(m, jnp.max(lg, axis=1, keepdims=True))
    w_own = jnp.exp(l_own - m)
    w_past = [jnp.exp(lg - m) for lg in l_past]
    denom = jnp.sum(w_own, axis=1, keepdims=True)
    for w in w_past:
        denom = denom + jnp.sum(w, axis=1, keepdims=True)
    inv = 1.0 / denom
    w_own = w_own * inv
    w_past = [w * inv for w in w_past]

    for cp in slab_copies(b, slot):
        cp.wait()

    o_ref[0] = jnp.zeros(o_ref.shape[1:], F32)
    for p in range(nrows):
        t, h = divmod(p, nheads)
        hs = slice(h * HEAD_DIM, (h + 1) * HEAD_DIM)
        acc = w_own[p:p + 1, :] * vnt_ref[0, hs, :]
        for j in range(MOBA_TOPK):
            for e in range(ppb):
                acc = acc + w_past[j][p:p + 1, e * page:(e + 1) * page] * vbuf[slot, p, j * ppb + e]
        o_ref[0, hs, t:t + 1] = jnp.sum(acc, axis=1, keepdims=True)


def _sample_attend(scores, idx, qt, knt, vnt, cache_vt, page_table, layer, blast, cfar, bown):
    nseq, nblk, nrows, _ = scores.shape
    wa, tq = qt.shape[1], qt.shape[2]
    nheads = wa // HEAD_DIM
    page = cache_vt.shape[-1]
    assert knt.shape[2] == page and bown.shape[1] == page
    seq = lambda shape: pl.BlockSpec((1,) + shape, lambda b, pt, ix: (b,) + (0,) * len(shape))
    whole = lambda shape: pl.BlockSpec(shape, lambda b, pt, ix: (0,) * len(shape))
    grid_spec = pltpu.PrefetchScalarGridSpec(
        num_scalar_prefetch=2,
        grid=(nseq,),
        in_specs=[seq((nblk, nrows, MOBA_BLOCK)), seq((wa, tq)), seq((wa, page)), seq((wa, page)),
                  whole(blast.shape), whole(cfar.shape), whole(bown.shape),
                  pl.BlockSpec(memory_space=pl.ANY)],
        out_specs=seq((wa, 128)),
        scratch_shapes=[pltpu.VMEM((2, nrows, MOBA_TOPK * (MOBA_BLOCK // page), HEAD_DIM, page), F32),
                        pltpu.SemaphoreType.DMA((2, nrows))],
    )
    return pl.pallas_call(
        functools.partial(_sample_attend_kernel, nheads=nheads, nblk=nblk, page=page, tq=tq, layer=layer),
        grid_spec=grid_spec,
        out_shape=jax.ShapeDtypeStruct((nseq, wa, 128), F32),
        compiler_params=_cparams("arbitrary"),
        name="sample_attend",
    )(page_table, idx, scores, qt, knt, vnt, blast, cfar, bown, cache_vt)


def _gdn_kernel(q_ref, k_ref, v_ref, kt_ref, gbc_ref, gbr_ref, s0_ref, onw_ref, o_ref, sout_ref,
                s_s, lhs_s, add_s, *, nheads, nchunk, chunks_per_iter):
    tc = pl.program_id(1)
    c_len = GDN_CHUNK
    hd = HEAD_DIM

    @pl.when(tc == 0)
    def _():
        s_s[...] = s0_ref[0]

    row = lax.broadcasted_iota(jnp.int32, (c_len, c_len), 0)
    col = lax.broadcasted_iota(jnp.int32, (c_len, c_len), 1)
    tri = row >= col
    tri_s = row > col
    eye = row == col
    eye_f = jnp.where(eye, 1.0, 0.0)
    blk = lambda size: (row // size) == (col // size)

    def chunk_a(it, carry):
        chains = [(it * chunks_per_iter + cc, h) for cc in range(chunks_per_iter) for h in range(nheads)]
        st = []
        for c, h in chains:
            gbc = gbc_ref[0, pl.ds(c * c_len, c_len), :]
            gbr = gbr_ref[0, c]
            q = q_ref[0, h, pl.ds(c * c_len, c_len), :]
            k = k_ref[0, h, pl.ds(c * c_len, c_len), :]
            v = v_ref[0, h, pl.ds(c * c_len, c_len), :]
            bcol = gbc[:, h:h + 1]
            gcol = gbc[:, nheads + h:nheads + h + 1]
            grow = gbr[nheads + h:nheads + h + 1, :]
            glast = gcol[c_len - 1:c_len, :]
            st.append(dict(c=c, h=h, q=q, k=k, v=v, bcol=bcol, grow=grow, glast=glast,
                           decay=jnp.where(tri, jnp.exp(gcol - grow), 0.0), egc=jnp.exp(gcol)))
        for d in st:
            d["kq"] = _mm1(jnp.concatenate([d["k"], d["q"]], axis=0), d["k"], _NT)
        for d in st:
            kk, qk = d["kq"][:c_len], d["kq"][c_len:]
            d["n"] = jnp.where(tri_s, d["bcol"] * kk * d["decay"], 0.0)
            d["qkd"] = jnp.where(tri, qk * d["decay"], 0.0)
            d["x"] = jnp.concatenate([d["v"] * d["bcol"], d["k"] * (d["bcol"] * d["egc"])], axis=1)
            d["a"] = -jnp.where(blk(INV_BASE), d["n"], 0.0)
        for d in st:
            d["p2"] = _mm2(d["a"], d["a"])
        for d in st:
            d["res"] = _mm2(jnp.concatenate([d["p2"], eye_f + d["a"]], axis=0), d["p2"])
        for d in st:
            d["p4"] = d["res"][:c_len]
            d["t"] = eye_f + d["a"] + d["res"][c_len:]
        for d in st:
            d["res"] = _mm2(d["t"], d["p4"])
        for d in st:
            d["t"] = d["t"] + d["res"]
        size = 2 * INV_BASE
        while size <= c_len:
            off = blk(size) & jnp.logical_not(blk(size // 2))
            for d in st:
                d["y"] = _mm1(jnp.where(off, d["n"], 0.0), d["t"])
            for d in st:
                d["res"] = _mm1(d["t"], d["y"])
            for d in st:
                d["t"] = d["t"] - d["res"]
            size *= 2
        for d in st:
            d["res"] = _mm2(d["t"], d["x"])
        for d in st:
            d["x"] = d["res"]
        for d in st:
            kdt = kt_ref[0, d["h"], d["c"]] * jnp.exp(d["glast"] - d["grow"])
            d["res"] = _mm2(jnp.concatenate([d["qkd"], kdt], axis=0), d["x"])
        for d in st:
            top = jnp.concatenate([d["q"] * d["egc"], jnp.where(eye, jnp.exp(d["glast"]), 0.0)], axis=0)
            lhs_s[d["c"], d["h"]] = top - d["res"][:, hd:]
            add_s[d["c"], d["h"]] = d["res"][:, :hd]
        return carry

    lax.fori_loop(0, nchunk // chunks_per_iter, chunk_a, 0)

    onw = onw_ref[...]

    state = [s_s[h] for h in range(nheads)]
    for c in range(nchunk):
        res = [_mm3(lhs_s[c, h], state[h]) + add_s[c, h] for h in range(nheads)]
        for h in range(nheads):
            o = res[h][:c_len]
            state[h] = res[h][c_len:]
            o = o * lax.rsqrt(jnp.mean(o * o, axis=-1, keepdims=True) + RMS_EPS) * onw
            o_ref[0, h, c * c_len:(c + 1) * c_len, :] = o
    for h in range(nheads):
        s_s[h] = state[h]
    sout_ref[0] = s_s[...]


def _gdn(qh, kh, vh, kt, gbc, gbr, s0, onw, nchunk):
    b, nheads, t, hd = qh.shape
    tb = nchunk * GDN_CHUNK
    nt = t // tb
    head = pl.BlockSpec((1, nheads, tb, hd), lambda bi, ti: (bi, 0, ti, 0))
    state = pl.BlockSpec((1, nheads, hd, hd), lambda bi, ti: (bi, 0, 0, 0))
    return pl.pallas_call(
        functools.partial(_gdn_kernel, nheads=nheads, nchunk=nchunk, chunks_per_iter=min(4, nchunk)),
        grid=(b, nt),
        in_specs=[head, head, head,
                  pl.BlockSpec((1, nheads, nchunk, hd, GDN_CHUNK), lambda bi, ti: (bi, 0, ti, 0, 0)),
                  pl.BlockSpec((1, tb, 8), lambda bi, ti: (bi, ti, 0)),
                  pl.BlockSpec((1, nchunk, 8, GDN_CHUNK), lambda bi, ti: (bi, ti, 0, 0)),
                  state, _const_spec((1, hd))],
        out_specs=[head, state],
        out_shape=[jax.ShapeDtypeStruct((b, nheads, t, hd), F32),
                   jax.ShapeDtypeStruct((b, nheads, hd, hd), F32)],
        scratch_shapes=[
            pltpu.VMEM((nheads, hd, hd), F32),
            pltpu.VMEM((nchunk, nheads, GDN_CHUNK + hd, hd), F32),
            pltpu.VMEM((nchunk, nheads, GDN_CHUNK + hd, hd), F32),
        ],
        compiler_params=_cparams("parallel", "arbitrary"),
        name="deltanet",
    )(qh, kh, vh, kt, gbc, gbr, s0, onw)


def _out_kernel(x_ref, gate_ref, att_ref, za_ref, yb_ref, yc_ref, zc_ref, w_ref, o_ref):
    if len(yc_ref.shape) == 3:
        yc = jnp.concatenate([yc_ref[h] for h in range(yc_ref.shape[0])], axis=-1)
    else:
        yc = yc_ref[...]
    f32 = lambda ref: ref[...].astype(F32)
    cat = jnp.concatenate([(f32(za_ref) * f32(att_ref)).astype(BF16), yb_ref[...].astype(BF16),
                           (yc * f32(zc_ref)).astype(BF16)], axis=-1)
    y = jnp.dot(cat, w_ref[...], preferred_element_type=F32)
    o_ref[...] = x_ref[...] + gate_ref[...] * y


def _out_proj(x2, gate2, att2, za2, yb2, yc, zc2, wout, rows_per_gate):
    rows, d = x2.shape
    tr = min(OUT_TILE, rows)
    wa, wb = att2.shape[1], yb2.shape[1]
    row = lambda w: pl.BlockSpec((tr, w), lambda i: (i, 0))
    if rows_per_gate == 1:
        gate_spec = row(d)
    else:
        per = rows_per_gate // tr
        gate2 = gate2.reshape(gate2.shape[0], 1, d)
        gate_spec = pl.BlockSpec((None, 1, d), lambda i: (i // per, 0, 0))
    if yc.ndim == 4:
        per_seq = yc.shape[2] // tr
        yc_spec = pl.BlockSpec((None, yc.shape[1], tr, yc.shape[3]), lambda i: (i // per_seq, 0, i % per_seq, 0))
    else:
        yc_spec = row(wb)
    return pl.pallas_call(
        _out_kernel,
        grid=(rows // tr,),
        in_specs=[row(d), gate_spec, row(wa), row(wa), row(wb), yc_spec, row(wb), _const_spec((d, d))],
        out_specs=row(d),
        out_shape=jax.ShapeDtypeStruct((rows, d), F32),
        compiler_params=_cparams("parallel"),
        name="out_proj",
    )(x2, gate2, att2, za2, yb2, yc, zc2, wout)


def _layer_weights(l, norm_w, w_in, q_norm_w, k_norm_w, conv_b_w, conv_c_w, a_log, dt_bias, o_norm_w,
                   w_out, nseq, ts):
    d = w_in.shape[1]
    wa, wb = d // 2, d // 4
    nheads_a = wa // HEAD_DIM
    hc = wb // HEAD_DIM
    n_main = 4 * wa + 4 * wb + 4 * wb
    w = w_in[l]
    wcat = jnp.concatenate([w[:, :n_main], jnp.pad(w[:, n_main:], ((0, 0), (0, 128 - 2 * hc)))], axis=1)
    r = np.arange(ROW_TILE)
    cum_prompt = ((r[:, None] // GDN_CHUNK == r[None, :] // GDN_CHUNK) & (r[:, None] >= r[None, :]))
    rs = np.arange(nseq * ts)
    cum_sample = ((rs[:, None] % nseq == rs[None, :] % nseq) & (rs[:, None] >= rs[None, :]))
    return dict(
        norm_w=norm_w[l].reshape(1, d),
        wcat=wcat.astype(BF16),
        qw=jnp.tile(q_norm_w[l], nheads_a).reshape(1, wa),
        kw=jnp.tile(k_norm_w[l], nheads_a).reshape(1, wa),
        conv_b_w=conv_b_w[l], conv_c_w=conv_c_w[l],
        alog8=jnp.concatenate([jnp.zeros((hc,), F32), a_log[l]]).reshape(1, 2 * hc),
        dtb8=jnp.concatenate([jnp.zeros((hc,), F32), dt_bias[l]]).reshape(1, 2 * hc),
        cum_prompt=jnp.asarray(cum_prompt, F32),
        cum_sample=jnp.asarray(cum_sample, F32),
        onw=o_norm_w[l].reshape(1, HEAD_DIM),
        wout=w_out[l].astype(BF16),
    )


def _heads(a, nheads):
    b, t, _ = a.shape
    return a.reshape(b, t, nheads, HEAD_DIM).transpose(0, 2, 1, 3)


def _gdn_layout(qc, kc, vc, gb, hc):
    b, t, _ = qc.shape
    nch = t // GDN_CHUNK
    kt = kc.reshape(b, nch, GDN_CHUNK, hc, HEAD_DIM).transpose(0, 3, 1, 4, 2)
    gbr = gb.reshape(b, nch, GDN_CHUNK, 2 * hc).transpose(0, 1, 3, 2)
    return _heads(qc, hc), _heads(kc, hc), _heads(vc, hc), kt, gb, gbr


def kernel(x_prompt, x_sample, c_prompt, c_sample, cache_k, cache_v, page_table, state_conv_b,
           state_conv_c, state_delta, norm_w, ada_w, ada_b, w_in, q_norm_w, k_norm_w, rel_bias,
           conv_b_w, conv_c_w, a_log, dt_bias, o_norm_w, w_out):
    bp, tp, d = x_prompt.shape
    bs, ts, _ = x_sample.shape
    depth = ada_w.shape[0]
    wa, wb = d // 2, d // 4
    ha, hc = wa // HEAD_DIM, wb // HEAD_DIM
    page = cache_k.shape[2]
    past_len = page_table.shape[1] * page
    nblk_p = tp // MOBA_BLOCK
    nblk_s = past_len // MOBA_BLOCK
    assert tp % ROW_TILE == 0 and ROW_TILE == MOBA_BLOCK and past_len % MOBA_BLOCK == 0
    assert MOBA_BLOCK % page == 0 and page_table.shape[1] % SAMPLE_PAGES_PER_STEP == 0 and ts <= GDN_CHUNK
    assert SAMPLE_PAGES_PER_STEP % (MOBA_BLOCK // page) == 0 and page == 128

    mod = _modulation(jnp.concatenate([c_prompt, c_sample], axis=0), ada_w, ada_b)

    tab_h = rel_bias.T.astype(F32)
    kq = np.arange(MOBA_BLOCK)
    d_own = kq[None, :] - kq[:, None]
    dist_p = np.stack([np.broadcast_to(d_own + o, (ha, MOBA_BLOCK, MOBA_BLOCK)) for o in (0, MOBA_BLOCK)])
    bt = _bias_tiles(jnp.asarray(dist_p.reshape(-1, MOBA_BLOCK), jnp.int32),
                     jnp.repeat(jnp.tile(tab_h, (2, 1)), MOBA_BLOCK, axis=0))
    bt = bt.reshape(2, ha, MOBA_BLOCK, MOBA_BLOCK)
    cfar_p = bt[1, :, 0, MOBA_BLOCK - 1]

    tq = np.repeat(np.arange(ts), ha)
    pos_q = past_len + tq
    d_last = pos_q[:, None] - ((nblk_s - 1) * MOBA_BLOCK + kq[None, :])
    own_w = 128
    tk = np.arange(own_w)
    d_own_s = np.where(tk[None, :] < ts, tq[:, None] - tk[None, :], -1)
    d_far = np.broadcast_to(pos_q[:, None] - (nblk_s - 2) * MOBA_BLOCK - (MOBA_BLOCK - 1), (ts * ha, 128))
    tab_s = jnp.tile(tab_h, (ts, 1))
    blast = _bias_tiles(jnp.asarray(d_last, jnp.int32), tab_s)
    bown = _bias_tiles(jnp.asarray(d_own_s, jnp.int32), tab_s)
    cfar_s = _bias_tiles(jnp.asarray(d_far, jnp.int32), tab_s)[:, :1]

    cache_kt = cache_k.transpose(0, 1, 3, 4, 2)
    cache_vt = cache_v.transpose(0, 1, 3, 4, 2)

    hp = x_prompt
    hs_tm = x_sample.transpose(1, 0, 2).reshape(ts * bs, d)
    outs = [[] for _ in range(10)]
    for l in range(depth):
        lw = _layer_weights(l, norm_w, w_in, q_norm_w, k_norm_w, conv_b_w, conv_c_w, a_log, dt_bias,
                            o_norm_w, w_out, bs, ts)
        mod_p, mod_s = mod[l, :bp], mod[l, bp:]

        (q, k, v, kb, vt, kmean, za, yb, cbs, qh, kh, vh, kt, zc, gb, gbr, ccs) = _proj_prompt(hp, mod_p, lw)
        att = _attn_prompt(q, kb, vt, kmean.reshape(bp, nblk_p, wa), bt, cfar_p)
        nchunk_p = 8
        yc, s_new = _gdn(qh, kh, vh, kt, gb, gbr, jnp.zeros((bp, hc, HEAD_DIM, HEAD_DIM), F32),
                         lw["onw"], nchunk_p)
        hp = _out_proj(hp.reshape(bp * tp, d), mod_p[:, 2 * d:], att.reshape(bp * tp, wa),
                       za.reshape(bp * tp, wa), yb.reshape(bp * tp, wb), yc, zc.reshape(bp * tp, wb),
                       lw["wout"], tp).reshape(bp, tp, d)
        outs[0].append(k.reshape(bp, tp, ha, HEAD_DIM)); outs[1].append(v.reshape(bp, tp, ha, HEAD_DIM))
        outs[4].append(cbs); outs[6].append(ccs); outs[8].append(s_new)

        rep = lambda a: jnp.tile(a, (ts, 1))
        hist_b = state_conv_b[l].transpose(1, 0, 2).reshape(-1, wb)
        hist_c = state_conv_c[l].transpose(1, 0, 2).reshape(-1, 3 * wb)
        (q, k, v, za, yb, cbs, qc, kc, vc, zc, gb, ccs) = _proj_sample(
            hs_tm, rep(mod_s[:, :d]), rep(mod_s[:, d:2 * d]), lw, hist_b, hist_c, bs)
        seq_major = lambda a: a.reshape(ts, bs, -1).transpose(1, 0, 2)
        qt = seq_major(q).transpose(0, 2, 1)
        new_t = lambda a: jnp.pad(seq_major(a), ((0, 0), (0, own_w - ts), (0, 0))).transpose(0, 2, 1)
        scores, idx = _sample_scores(jnp.repeat(seq_major(q), ha, axis=1), cache_kt, page_table, l)
        att_t = _sample_attend(scores, idx[:, :, :MOBA_TOPK].reshape(bs, -1), qt, new_t(k), new_t(v),
                               cache_vt, page_table, l, blast, cfar_s, bown)
        att = att_t[:, :, :ts].transpose(2, 0, 1).reshape(ts * bs, wa)
        pad_t = lambda a: jnp.pad(seq_major(a), ((0, 0), (0, GDN_CHUNK - ts), (0, 0)))
        gbs = seq_major(gb)
        gb_pad = jnp.concatenate(
            [pad_t(gb[:, :hc]), jnp.pad(gbs[:, :, hc:], ((0, 0), (0, GDN_CHUNK - ts), (0, 0)), mode="edge")],
            axis=-1)
        yc, s_new = _gdn(*_gdn_layout(pad_t(qc), pad_t(kc), pad_t(vc), gb_pad, hc), state_delta[l],
                         lw["onw"], 1)
        yc = yc[:, :, :ts, :].transpose(2, 0, 1, 3).reshape(ts * bs, wb)
        hs_tm = _out_proj(hs_tm, rep(mod_s[:, 2 * d:]), att, za, yb, yc, zc, lw["wout"], 1)
        outs[2].append(seq_major(k).reshape(bs, ts, ha, HEAD_DIM))
        outs[3].append(seq_major(v).reshape(bs, ts, ha, HEAD_DIM))
        outs[5].append(cbs.reshape(-1, bs, wb).transpose(1, 0, 2))
        outs[7].append(ccs.reshape(-1, bs, 3 * wb).transpose(1, 0, 2))
        outs[9].append(s_new)

    y_sample = hs_tm.reshape(ts, bs, d).transpose(1, 0, 2)
    return (hp, y_sample) + tuple(jnp.stack(o) for o in outs)
```

```python
import functools
import math

import jax
import jax.numpy as jnp
import numpy as np
from jax import lax
from jax.experimental import pallas as pl
from jax.experimental.pallas import tpu as pltpu

F32 = jnp.float32
BF16 = jnp.bfloat16
HIGHEST = lax.Precision.HIGHEST

HEAD_DIM = 64
MOBA_BLOCK = 256
MOBA_TOPK = 3
GDN_CHUNK = 64
VT_PAD = 16
VT_ROWS = HEAD_DIM + VT_PAD
QK_AHEAD = 4
LOG2E = math.log2(math.e)
INV_BASE = 8
REL_BUCKETS = 32
REL_MAX_DIST = 128
RMS_EPS = 1e-6
ROW_TILE = 256
OUT_TILE = 1024
VMEM_LIMIT = 56 * 1024 * 1024
NEG_INF = float("-inf")

_NT = (((1,), (1,)), ((), ()))


def _cparams(*sem):
    return pltpu.CompilerParams(dimension_semantics=sem, vmem_limit_bytes=VMEM_LIMIT)


def _const_spec(shape):
    zeros = (0,) * len(shape)
    return pl.BlockSpec(shape, lambda *_: zeros)


def _silu(x):
    return x * jax.nn.sigmoid(x)


def _softplus(x):
    return jnp.maximum(x, 0.0) + jnp.log(1.0 + jnp.exp(-jnp.abs(x)))


def _split(a):
    hi = a.astype(BF16)
    lo = (a - hi.astype(F32)).astype(BF16)
    return hi, lo


def _dot(a, b, dims=None):
    if dims is None:
        return jnp.dot(a, b, preferred_element_type=F32)
    return lax.dot_general(a, b, dims, preferred_element_type=F32)


def _mm1(a, b, dims=None):
    return _dot(a.astype(BF16), b.astype(BF16), dims)


def _mm2(a, b, dims=None):
    ah = a.astype(BF16)
    bh, bl = _split(b)
    return _dot(ah, bh, dims) + _dot(ah, bl, dims)


def _mm3(a, b, dims=None):
    ah, al = _split(a)
    bh, bl = _split(b)
    return _dot(ah, bh, dims) + _dot(al, bh, dims) + _dot(ah, bl, dims)


def _group_sum(x):
    lane = lax.broadcasted_iota(jnp.int32, (1, 2 * HEAD_DIM), 1)
    first = lane < HEAD_DIM
    outs = []
    for c in range(x.shape[1] // (2 * HEAD_DIM)):
        xc = x[:, c * 2 * HEAD_DIM:(c + 1) * 2 * HEAD_DIM]
        s0 = jnp.sum(jnp.where(first, xc, 0.0), axis=-1, keepdims=True)
        s1 = jnp.sum(jnp.where(first, 0.0, xc), axis=-1, keepdims=True)
        outs.append(jnp.where(first, s0, s1))
    return jnp.concatenate(outs, axis=1)


def _mod_kernel(c_ref, w_ref, b_ref, o_ref):
    o_ref[0] = jnp.dot(c_ref[...], w_ref[0], precision=HIGHEST, preferred_element_type=F32) + b_ref[0]


def _modulation(c_all, ada_w, ada_b):
    depth, d, n3 = ada_w.shape
    rows = c_all.shape[0]
    tn = 512
    return pl.pallas_call(
        _mod_kernel,
        grid=(depth, n3 // tn),
        in_specs=[
            pl.BlockSpec((rows, d), lambda l, n: (0, 0)),
            pl.BlockSpec((1, d, tn), lambda l, n: (l, 0, n)),
            pl.BlockSpec((1, 1, tn), lambda l, n: (l, 0, n)),
        ],
        out_specs=pl.BlockSpec((1, rows, tn), lambda l, n: (l, 0, n)),
        out_shape=jax.ShapeDtypeStruct((depth, rows, n3), F32),
        compiler_params=_cparams("parallel", "parallel"),
        name="modulation",
    )(c_all, ada_w, ada_b.reshape(depth, 1, n3))


def _bias_kernel(dist_ref, tab_ref, o_ref):
    dist = dist_ref[...]
    n = jnp.maximum(dist, 0)
    max_exact = REL_BUCKETS // 2
    nf = jnp.maximum(n, max_exact).astype(F32)
    large = max_exact + (jnp.log(nf / max_exact) / math.log(REL_MAX_DIST / max_exact)
                         * (REL_BUCKETS - max_exact)).astype(jnp.int32)
    bucket = jnp.where(n < max_exact, n, jnp.minimum(large, REL_BUCKETS - 1))
    tab = tab_ref[...]
    acc = jnp.zeros(dist.shape, F32)
    for b in range(REL_BUCKETS):
        acc = jnp.where(bucket == b, tab[:, b:b + 1], acc)
    o_ref[...] = jnp.where(dist >= 0, acc, NEG_INF)


def _bias_tiles(dist, tab):
    rows, cols = dist.shape
    tr = min(rows, 256)
    return pl.pallas_call(
        _bias_kernel,
        grid=(rows // tr,),
        in_specs=[pl.BlockSpec((tr, cols), lambda i: (i, 0)),
                  pl.BlockSpec((tr, REL_BUCKETS), lambda i: (i, 0))],
        out_specs=pl.BlockSpec((tr, cols), lambda i: (i, 0)),
        out_shape=jax.ShapeDtypeStruct((rows, cols), F32),
        compiler_params=_cparams("parallel"),
        name="bias_tiles",
    )(dist, tab)


def _proj_common(x, shift, scale, normw, w_ref, qw, kw, alog, dtb, cum_ref):
    ms = jnp.mean(x * x, axis=-1, keepdims=True)
    hn = (x * lax.rsqrt(ms + RMS_EPS) * normw) * (1.0 + scale) + shift
    z = jnp.dot(hn.astype(BF16), w_ref[...], preferred_element_type=F32)
    wa = qw.shape[-1]
    wb = wa // 2
    o = 0
    q = z[:, o:o + wa]; o += wa
    k = z[:, o:o + wa]; o += wa
    v = z[:, o:o + wa]; o += wa
    za = z[:, o:o + wa]; o += wa
    hb = z[:, o:o + wb]; o += wb
    bg = z[:, o:o + wb]; o += wb
    cg = z[:, o:o + wb]; o += wb
    zb = z[:, o:o + wb]; o += wb
    qkv = z[:, o:o + 3 * wb]; o += 3 * wb
    zc = z[:, o:o + wb]; o += wb
    z8 = z[:, o:o + 8]

    qn = q * lax.rsqrt(_group_sum(q * q) * (1.0 / HEAD_DIM) + RMS_EPS) * qw
    kn = k * lax.rsqrt(_group_sum(k * k) * (1.0 / HEAD_DIM) + RMS_EPS) * kw

    lane8 = lax.broadcasted_iota(jnp.int32, z8.shape, 1)
    beta = jax.nn.sigmoid(z8)
    g = -jnp.exp(alog) * _softplus(z8 + dtb)
    bg8 = jnp.where(lane8 < 4, beta, g)
    cum = jnp.dot(cum_ref[...], bg8, precision=HIGHEST, preferred_element_type=F32)
    gb = jnp.where(lane8 < 4, bg8, cum)
    return dict(q=qn * (HEAD_DIM ** -0.5), k=kn, v=v, za=_silu(za), u=cg * hb, zbg=_silu(zb) * bg,
                qkv=qkv, zc=_silu(zc), gb=gb)


def _gdn_qkv(y):
    wb = y.shape[-1] // 3
    y = _silu(y)
    qc, kc, vc = y[:, :wb], y[:, wb:2 * wb], y[:, 2 * wb:]
    qc = qc * lax.rsqrt(_group_sum(qc * qc) + RMS_EPS) * (HEAD_DIM ** -0.5)
    kc = kc * lax.rsqrt(_group_sum(kc * kc) + RMS_EPS)
    return qc, kc, vc


def _proj_prompt_kernel(x_ref, mod_ref, normw_ref, w_ref, qw_ref, kw_ref,
                        cbw_ref, ccw_ref, alog_ref, dtb_ref, cum_ref,
                        q_ref, k_ref, v_ref, kb_ref, vt_ref, km_ref, za_ref, yb_ref, cbs_ref,
                        qc_ref, kc_ref, vc_ref, kt_ref, zc_ref, gb_ref, gbr_ref, ccs_ref,
                        carry_b, carry_c):
    ti = pl.program_id(1)
    d = x_ref.shape[-1]
    rows = x_ref.shape[1]

    @pl.when(ti == 0)
    def _():
        carry_b[...] = jnp.zeros_like(carry_b)
        carry_c[...] = jnp.zeros_like(carry_c)

    mod = mod_ref[0]
    p = _proj_common(x_ref[0], mod[:, :d], mod[:, d:2 * d], normw_ref[...], w_ref, qw_ref[...],
                     kw_ref[...], alog_ref[...], dtb_ref[...], cum_ref)
    q_ref[0] = p["q"]
    k_ref[0] = p["k"]
    v_ref[0] = p["v"]
    km_ref[0, 0] = jnp.mean(p["k"], axis=0, keepdims=True)
    za_ref[0] = p["za"].astype(za_ref.dtype)
    zc_ref[0] = p["zc"].astype(zc_ref.dtype)
    gb = p["gb"]
    gb_ref[0] = gb
    kb_ref[0, 0] = p["k"].astype(BF16)
    vtb = p["v"].T.astype(BF16)
    ones = jnp.ones((VT_PAD, rows), BF16)
    pieces = []
    for h in range(vtb.shape[0] // HEAD_DIM):
        pieces += [vtb[h * HEAD_DIM:(h + 1) * HEAD_DIM], ones]
    vt_ref[0, 0] = jnp.concatenate(pieces, axis=0)
    ncol = gb.shape[1]
    eye8 = jnp.where(lax.broadcasted_iota(jnp.int32, (ncol, ncol), 0)
                     == lax.broadcasted_iota(jnp.int32, (ncol, ncol), 1), 1.0, 0.0)
    gbr = lax.dot_general(eye8, gb, _NT, precision=HIGHEST, preferred_element_type=F32)
    for cc in range(rows // GDN_CHUNK):
        gbr_ref[0, cc] = gbr[:, cc * GDN_CHUNK:(cc + 1) * GDN_CHUNK]

    u = p["u"]
    ext = jnp.concatenate([carry_b[...], u], axis=0)
    cbw = cbw_ref[...]
    nb = cbw.shape[0]
    yb = u * cbw[nb - 1:nb]
    for j in range(nb - 1):
        s = nb - 1 - j
        yb = yb + ext[8 - s:8 - s + rows] * cbw[j:j + 1]
    yb_ref[0] = (p["zbg"] * yb).astype(yb_ref.dtype)
    carry_b[...] = u[rows - 8:]
    cbs_ref[0] = u[rows - (nb - 1):]

    qkv = p["qkv"]
    extc = jnp.concatenate([carry_c[...], qkv], axis=0)
    ccw = ccw_ref[...]
    nc = ccw.shape[0]
    yc = qkv * ccw[nc - 1:nc]
    for j in range(nc - 1):
        s = nc - 1 - j
        yc = yc + extc[8 - s:8 - s + rows] * ccw[j:j + 1]
    carry_c[...] = qkv[rows - 8:]
    ccs_ref[0] = qkv[rows - (nc - 1):]
    qc, kc, vc = _gdn_qkv(yc)
    kct = kc.T
    for h in range(qc.shape[1] // HEAD_DIM):
        hs = slice(h * HEAD_DIM, (h + 1) * HEAD_DIM)
        qc_ref[0, h] = qc[:, hs]
        kc_ref[0, h] = kc[:, hs]
        vc_ref[0, h] = vc[:, hs]
        for cc in range(rows // GDN_CHUNK):
            kt_ref[0, h, cc] = kct[hs, cc * GDN_CHUNK:(cc + 1) * GDN_CHUNK]


def _proj_sample_kernel(x_ref, shift_ref, scale_ref, normw_ref, w_ref, qw_ref, kw_ref,
                        cbw_ref, ccw_ref, alog_ref, dtb_ref, cum_ref, hb_ref, hc_ref,
                        q_ref, k_ref, v_ref, za_ref, yb_ref, cbs_ref,
                        qc_ref, kc_ref, vc_ref, zc_ref, gb_ref, ccs_ref, *, nseq):
    rows = x_ref.shape[0]
    p = _proj_common(x_ref[...], shift_ref[...], scale_ref[...], normw_ref[...], w_ref, qw_ref[...],
                     kw_ref[...], alog_ref[...], dtb_ref[...], cum_ref)
    q_ref[...] = p["q"]
    k_ref[...] = p["k"]
    v_ref[...] = p["v"]
    za_ref[...] = p["za"]
    zc_ref[...] = p["zc"]
    gb_ref[...] = p["gb"]

    cbw = cbw_ref[...]
    nb = cbw.shape[0]
    ext = jnp.concatenate([hb_ref[...], p["u"]], axis=0)
    yb = ext[0:rows] * cbw[0:1]
    for j in range(1, nb):
        yb = yb + ext[j * nseq:j * nseq + rows] * cbw[j:j + 1]
    yb_ref[...] = p["zbg"] * yb
    cbs_ref[...] = ext[rows:]

    ccw = ccw_ref[...]
    nc = ccw.shape[0]
    extc = jnp.concatenate([hc_ref[...], p["qkv"]], axis=0)
    yc = extc[0:rows] * ccw[0:1]
    for j in range(1, nc):
        yc = yc + extc[j * nseq:j * nseq + rows] * ccw[j:j + 1]
    ccs_ref[...] = extc[rows:]
    qc, kc, vc = _gdn_qkv(yc)
    qc_ref[...] = qc
    kc_ref[...] = kc
    vc_ref[...] = vc


def _proj_prompt(x, mod, lw):
    b, t, d = x.shape
    wcat = lw["wcat"]
    ncat = wcat.shape[1]
    wa, wb = d // 2, d // 4
    nt = t // ROW_TILE
    row = lambda w: pl.BlockSpec((1, ROW_TILE, w), lambda bi, ti: (bi, ti, 0))
    per_b = lambda r, w: pl.BlockSpec((1, r, w), lambda bi, ti: (bi, 0, 0))
    nb, nc = lw["conv_b_w"].shape[0], lw["conv_c_w"].shape[0]
    hc = wb // HEAD_DIM
    cpt = ROW_TILE // GDN_CHUNK
    nch = t // GDN_CHUNK
    out_shape = [
        jax.ShapeDtypeStruct((b, t, wa), F32),
        jax.ShapeDtypeStruct((b, t, wa), F32),
        jax.ShapeDtypeStruct((b, t, wa), F32),
        jax.ShapeDtypeStruct((b, nt, ROW_TILE, wa), BF16),
        jax.ShapeDtypeStruct((b, nt, wa // HEAD_DIM * VT_ROWS, ROW_TILE), BF16),
        jax.ShapeDtypeStruct((b, nt, 1, wa), F32),
        jax.ShapeDtypeStruct((b, t, wa), BF16),
        jax.ShapeDtypeStruct((b, t, wb), BF16),
        jax.ShapeDtypeStruct((b, nb - 1, wb), F32),
        jax.ShapeDtypeStruct((b, hc, t, HEAD_DIM), F32),
        jax.ShapeDtypeStruct((b, hc, t, HEAD_DIM), F32),
        jax.ShapeDtypeStruct((b, hc, t, HEAD_DIM), F32),
        jax.ShapeDtypeStruct((b, hc, nch, HEAD_DIM, GDN_CHUNK), F32),
        jax.ShapeDtypeStruct((b, t, wb), BF16),
        jax.ShapeDtypeStruct((b, t, 8), F32),
        jax.ShapeDtypeStruct((b, nch, 8, GDN_CHUNK), F32),
        jax.ShapeDtypeStruct((b, nc - 1, 3 * wb), F32),
    ]
    blk4 = lambda r, c: pl.BlockSpec((1, 1, r, c), lambda bi, ti: (bi, ti, 0, 0))
    heads = pl.BlockSpec((1, hc, ROW_TILE, HEAD_DIM), lambda bi, ti: (bi, 0, ti, 0))
    out_specs = [
        row(wa), row(wa), row(wa), blk4(ROW_TILE, wa), blk4(wa // HEAD_DIM * VT_ROWS, ROW_TILE), blk4(1, wa),
        row(wa), row(wb), per_b(nb - 1, wb),
        heads, heads, heads,
        pl.BlockSpec((1, hc, cpt, HEAD_DIM, GDN_CHUNK), lambda bi, ti: (bi, 0, ti, 0, 0)),
        row(wb), row(8),
        pl.BlockSpec((1, cpt, 8, GDN_CHUNK), lambda bi, ti: (bi, ti, 0, 0)),
        per_b(nc - 1, 3 * wb),
    ]
    in_specs = [
        row(d),
        pl.BlockSpec((1, 1, 3 * d), lambda bi, ti: (bi, 0, 0)),
        _const_spec((1, d)),
        _const_spec((d, ncat)),
        _const_spec((1, wa)), _const_spec((1, wa)),
        _const_spec((nb, wb)), _const_spec((nc, 3 * wb)),
        _const_spec((1, 8)), _const_spec((1, 8)),
        _const_spec((ROW_TILE, ROW_TILE)),
    ]
    return pl.pallas_call(
        _proj_prompt_kernel,
        grid=(b, nt),
        in_specs=in_specs,
        out_specs=out_specs,
        out_shape=out_shape,
        scratch_shapes=[pltpu.VMEM((8, wb), F32), pltpu.VMEM((8, 3 * wb), F32)],
        compiler_params=_cparams("parallel", "arbitrary"),
        name="proj_prompt",
    )(x, mod.reshape(b, 1, 3 * d), lw["norm_w"], wcat, lw["qw"], lw["kw"],
      lw["conv_b_w"], lw["conv_c_w"], lw["alog8"], lw["dtb8"], lw["cum_prompt"])


def _proj_sample(x_tm, shift, scale, lw, hist_b, hist_c, nseq):
    rows, d = x_tm.shape
    wa, wb = d // 2, d // 4
    nb, nc = lw["conv_b_w"].shape[0], lw["conv_c_w"].shape[0]
    out_shape = [
        jax.ShapeDtypeStruct((rows, wa), F32), jax.ShapeDtypeStruct((rows, wa), F32),
        jax.ShapeDtypeStruct((rows, wa), F32), jax.ShapeDtypeStruct((rows, wa), F32),
        jax.ShapeDtypeStruct((rows, wb), F32),
        jax.ShapeDtypeStruct(((nb - 1) * nseq, wb), F32),
        jax.ShapeDtypeStruct((rows, wb), F32), jax.ShapeDtypeStruct((rows, wb), F32),
        jax.ShapeDtypeStruct((rows, wb), F32), jax.ShapeDtypeStruct((rows, wb), F32),
        jax.ShapeDtypeStruct((rows, 8), F32),
        jax.ShapeDtypeStruct(((nc - 1) * nseq, 3 * wb), F32),
    ]
    return pl.pallas_call(
        functools.partial(_proj_sample_kernel, nseq=nseq),
        out_shape=out_shape,
        compiler_params=pltpu.CompilerParams(vmem_limit_bytes=VMEM_LIMIT),
        name="proj_sample",
    )(x_tm, shift, scale, lw["norm_w"], lw["wcat"], lw["qw"], lw["kw"],
      lw["conv_b_w"], lw["conv_c_w"], lw["alog8"], lw["dtb8"], lw["cum_sample"], hist_b, hist_c)


def _attn_prompt_kernel(cfar_ref, q_ref, kb_ref, vt_ref, km_ref, bt_ref, o_ref,
                        sel_s, acc_s, *, nheads, nblk):
    i = pl.program_id(1)
    blk = q_ref.shape[1]
    qt = q_ref[0].T * LOG2E
    qtb = qt.astype(BF16)
    lane = lax.broadcasted_iota(jnp.int32, (1, 2 * HEAD_DIM), 1)
    sub = lax.broadcasted_iota(jnp.int32, (2 * HEAD_DIM, 1), 0)
    lane_half = [lane < HEAD_DIM, lane >= HEAD_DIM]
    sub_half = [sub < HEAD_DIM, sub >= HEAD_DIM]

    km = km_ref[0]
    n = lax.broadcasted_iota(jnp.int32, (nblk, blk), 0)
    n_f = n.astype(F32)
    past = n < i
    for h in range(nheads):
        p, par = divmod(h, 2)
        kmp = jnp.where(lane_half[par], km[:, p * 128:(p + 1) * 128], 0.0)
        gate = jnp.dot(kmp, qt[p * 128:(p + 1) * 128, :], precision=HIGHEST, preferred_element_type=F32)
        gate = jnp.where(past, gate, NEG_INF)
        picked = jnp.zeros((nblk, blk), F32)
        for _ in range(MOBA_TOPK):
            top = jnp.max(gate, axis=0, keepdims=True)
            first = jnp.min(jnp.where(gate == top, n_f, float(nblk)), axis=0, keepdims=True)
            chosen = n_f == first
            picked = jnp.where(chosen, 1.0, picked)
            gate = jnp.where(chosen, NEG_INF, gate)
        sel_s[h * nblk:(h + 1) * nblk, :] = jnp.where(past, picked, 0.0)

    qm = []
    for h in range(nheads):
        p, par = divmod(h, 2)
        qm.append(jnp.where(sub_half[par], qtb[p * 128:(p + 1) * 128, :], jnp.zeros((), BF16)))

    def scores(kblk, h):
        p = h // 2
        return jnp.dot(kblk[:, p * 128:(p + 1) * 128], qm[h], preferred_element_type=F32)

    def pv_dot(vblk, h, pr):
        res = jnp.dot(vblk[h * VT_ROWS:(h + 1) * VT_ROWS, :], pr.astype(BF16), preferred_element_type=F32)
        return res[:HEAD_DIM], res[HEAD_DIM:HEAD_DIM + 1]


    def block(js, kind, m_all, l_all):
        nb = len(js)
        kblk = kb_ref[0, js[0]] if nb == 1 else jnp.concatenate([kb_ref[0, j] for j in js], axis=0)
        sts = [scores(kblk, h) for h in range(min(QK_AHEAD, nheads))]
        m_rows, l_rows = [], []
        pend = None

        def finish(h, alpha, ons, pvs):
            rs = slice(h * HEAD_DIM, (h + 1) * HEAD_DIM)
            if kind == "own":
                acc_s[rs, :] = pvs[0]
            else:
                new = alpha * acc_s[rs, :]
                for on, pv in zip(ons, pvs):
                    new = new + jnp.where(on, pv, 0.0)
                acc_s[rs, :] = new

        for h in range(nheads):
            st = sts[h]
            if h + QK_AHEAD < nheads:
                sts.append(scores(kblk, h + QK_AHEAD))
            if kind == "own":
                st = bt_ref[0, h] * LOG2E + st
                m_new = jnp.max(st, axis=0, keepdims=True)
                pv, l_new = pv_dot(vt_ref[0, js[0]], h, jnp.exp2(st - m_new))
                alpha, ons, pvs = None, None, [pv]
            else:
                if kind == "near":
                    st = bt_ref[1, h] * LOG2E + st
                    shift = 0.0
                else:
                    shift = cfar_ref[h] * LOG2E
                parts = [st[s * blk:(s + 1) * blk] for s in range(nb)]
                ons = [sel_s[pl.ds(h * nblk + j, 1), :] > 0.0 for j in js]
                m_old = m_all[h:h + 1, :]
                m_new = m_old
                for on, part in zip(ons, parts):
                    m_new = jnp.maximum(m_new, jnp.where(on, jnp.max(part, axis=0, keepdims=True) + shift, NEG_INF))
                alpha = jnp.exp2(m_old - m_new)
                l_new = alpha * l_all[h:h + 1, :]
                pvs = []
                for j, on, part in zip(js, ons, parts):
                    pv, psum = pv_dot(vt_ref[0, j], h, jnp.exp2(part - (m_new - shift)))
                    pvs.append(pv)
                    l_new = l_new + jnp.where(on, psum, 0.0)
            m_rows.append(m_new)
            l_rows.append(l_new)
            if pend is not None:
                finish(*pend)
            pend = (h, alpha, ons, pvs)
        finish(*pend)
        return jnp.concatenate(m_rows, axis=0), jnp.concatenate(l_rows, axis=0)

    keep = lambda m, l: (m, l)
    m_all, l_all = block([i], "own", None, None)
    m_all, l_all = lax.cond(i >= 1, lambda m, l: block([i - 1], "near", m, l), keep, m_all, l_all)
    nfar = i - 1
    odd = (nfar >= 1) & ((nfar & 1) == 1)
    m_all, l_all = lax.cond(odd, lambda m, l: block([nfar - 1], "far", m, l), keep, m_all, l_all)
    m_all, l_all = lax.fori_loop(0, nfar // 2, lambda jj, c: block([2 * jj, 2 * jj + 1], "far", *c), (m_all, l_all))

    for h in range(nheads):
        rs = slice(h * HEAD_DIM, (h + 1) * HEAD_DIM)
        acc_s[rs, :] = acc_s[rs, :] / l_all[h:h + 1, :]
    o_ref[0] = acc_s[...].T.astype(o_ref.dtype)


def _attn_prompt(q, kb, vt, kmean, bt, cfar):
    b, t, wa = q.shape
    nheads = wa // HEAD_DIM
    nblk = t // MOBA_BLOCK
    grid_spec = pltpu.PrefetchScalarGridSpec(
        num_scalar_prefetch=0,
        grid=(b, nblk),
        in_specs=[
            pl.BlockSpec(memory_space=pltpu.SMEM),
            pl.BlockSpec((1, MOBA_BLOCK, wa), lambda bi, i: (bi, i, 0)),
            pl.BlockSpec((1, nblk, MOBA_BLOCK, wa), lambda bi, i: (bi, 0, 0, 0)),
            pl.BlockSpec((1, nblk, nheads * VT_ROWS, MOBA_BLOCK), lambda bi, i: (bi, 0, 0, 0)),
            pl.BlockSpec((1, nblk, wa), lambda bi, i: (bi, 0, 0)),
            pl.BlockSpec((2, nheads, MOBA_BLOCK, MOBA_BLOCK), lambda bi, i: (0, 0, 0, 0)),
        ],
        out_specs=pl.BlockSpec((1, MOBA_BLOCK, wa), lambda bi, i: (bi, i, 0)),
        scratch_shapes=[
            pltpu.VMEM((nheads * nblk, MOBA_BLOCK), F32),
            pltpu.VMEM((wa, MOBA_BLOCK), F32),
        ],
    )
    return pl.pallas_call(
        functools.partial(_attn_prompt_kernel, nheads=nheads, nblk=nblk),
        grid_spec=grid_spec,
        out_shape=jax.ShapeDtypeStruct((b, t, wa), BF16),
        compiler_params=_cparams("parallel", "arbitrary"),
        name="attn_prompt",
    )(cfar, q, kb, vt, kmean, bt)


SAMPLE_PAGES_PER_STEP = 64


def _sample_scores_kernel(pt_ref, q_ref, *rest, nheads, nblk, page, tq):
    npg = SAMPLE_PAGES_PER_STEP
    k_refs = rest[:npg]
    s_ref, idx_ref, qhl_s, gate_s = rest[npg:]
    s = pl.program_id(1)
    nrows = tq * nheads
    wa = nheads * HEAD_DIM
    ppb = MOBA_BLOCK // page

    @pl.when(s == 0)
    def _():
        r = lax.broadcasted_iota(jnp.int32, (nrows, wa), 0)
        c = lax.broadcasted_iota(jnp.int32, (nrows, wa), 1)
        hi, lo = _split(jnp.where(c // HEAD_DIM == r % nheads, q_ref[0], 0.0))
        qhl_s[:nrows, :] = hi
        qhl_s[nrows:, :] = lo
        gate_s[...] = jnp.full(gate_s.shape, NEG_INF, F32)

    lane = lax.broadcasted_iota(jnp.int32, gate_s.shape, 1)
    qhl = qhl_s[...]
    bsum = None
    for e in range(npg):
        kh, kl = _split(k_refs[e][...].reshape(wa, page))
        both = jnp.dot(qhl, kh, preferred_element_type=F32)
        sp = both[:nrows] + both[nrows:] + jnp.dot(qhl[:nrows], kl, preferred_element_type=F32)
        s_ref[0, e // ppb, :, (e % ppb) * page:(e % ppb + 1) * page] = sp
        rs = jnp.sum(sp, axis=1, keepdims=True)
        bsum = rs if e % ppb == 0 else bsum + rs
        if e % ppb == ppb - 1:
            n = s * (npg // ppb) + e // ppb
            gate_s[...] = jnp.where(lane == n, bsum * (1.0 / MOBA_BLOCK), gate_s[...])

    @pl.when(s == pl.num_programs(1) - 1)
    def _():
        gate = gate_s[...]
        rank = jnp.zeros(gate.shape, F32)
        for sh in range(1, gate.shape[1]):
            other = pltpu.roll(gate, sh, 1)
            rank = rank + jnp.where(other > gate, 1.0, 0.0)
            rank = rank + jnp.where((other == gate) & (lane >= sh), 1.0, 0.0)
        lane_f = lane.astype(F32)
        out = jnp.zeros(gate.shape, F32)
        for j in range(MOBA_TOPK):
            pick = jnp.sum(jnp.where((rank == j) & (lane < nblk), lane_f, 0.0), axis=1, keepdims=True)
            out = jnp.where(lane == j, pick, out)
        idx_ref[0] = out.astype(jnp.int32)


def _sample_scores(q_rows, cache_kt, page_table, layer):
    nseq, nrows, wa = q_rows.shape
    nheads = wa // HEAD_DIM
    tq = nrows // nheads
    page = cache_kt.shape[-1]
    n_pages = page_table.shape[1]
    nblk = n_pages * page // MOBA_BLOCK
    npg = SAMPLE_PAGES_PER_STEP

    def page_spec(e):
        return pl.BlockSpec((None, None, nheads, HEAD_DIM, page),
                            lambda b, s, pt: (layer, pt[b, s * npg + e], 0, 0, 0))

    grid_spec = pltpu.PrefetchScalarGridSpec(
        num_scalar_prefetch=1,
        grid=(nseq, n_pages // npg),
        in_specs=[pl.BlockSpec((1, nrows, wa), lambda b, s, pt: (b, 0, 0))] + [page_spec(e) for e in range(npg)],
        out_specs=[pl.BlockSpec((1, npg * page // MOBA_BLOCK, nrows, MOBA_BLOCK), lambda b, s, pt: (b, s, 0, 0)),
                   pl.BlockSpec((1, nrows, 128), lambda b, s, pt: (b, 0, 0))],
        scratch_shapes=[pltpu.VMEM((2 * nrows, wa), BF16), pltpu.VMEM((nrows, 128), F32)],
    )
    return pl.pallas_call(
        functools.partial(_sample_scores_kernel, nheads=nheads, nblk=nblk, page=page, tq=tq),
        grid_spec=grid_spec,
        out_shape=[jax.ShapeDtypeStruct((nseq, nblk, nrows, MOBA_BLOCK), F32),
                   jax.ShapeDtypeStruct((nseq, nrows, 128), jnp.int32)],
        compiler_params=_cparams("parallel", "arbitrary"),
        name="sample_scores",
    )(page_table, q_rows, *([cache_kt] * npg))


def _sample_attend_kernel(pt_ref, idx_ref, s_ref, qt_ref, knt_ref, vnt_ref, blast_ref, cfar_ref, bown_ref,
                          cv_ref, o_ref, vbuf, sems, *, nheads, nblk, page, tq, layer):
    b = pl.program_id(0)
    nrows = tq * nheads
    ppb = MOBA_BLOCK // page

    slot = b % 2

    def slab_copies(seq, half):
        out = []
        for p in range(nrows):
            h = p % nheads
            for j in range(MOBA_TOPK):
                n = idx_ref[seq, p * MOBA_TOPK + j]
                for e in range(ppb):
                    pg = pt_ref[seq, n * ppb + e]
                    out.append(pltpu.make_async_copy(cv_ref.at[layer, pg, h], vbuf.at[half, p, j * ppb + e],
                                                     sems.at[half, p]))
        return out

    @pl.when(b == 0)
    def _():
        for cp in slab_copies(b, slot):
            cp.start()

    @pl.when(b + 1 < pl.num_programs(0))
    def _():
        for cp in slab_copies(b + 1, 1 - slot):
            cp.start()

    qt = qt_ref[0]
    own, past = [], [[] for _ in range(MOBA_TOPK)]
    for p in range(nrows):
        t, h = divmod(p, nheads)
        hs = slice(h * HEAD_DIM, (h + 1) * HEAD_DIM)
        qb = jnp.broadcast_to(qt[hs, t:t + 1], (HEAD_DIM, knt_ref.shape[2]))
        own.append(jnp.sum(knt_ref[0, hs, :] * qb, axis=0, keepdims=True) + bown_ref[p:p + 1, :])
        for j in range(MOBA_TOPK):
            n = idx_ref[b, p * MOBA_TOPK + j]
            bias = jnp.where(n == nblk - 1, blast_ref[p:p + 1, :], cfar_ref[p:p + 1, :])
            past[j].append(s_ref[0, n, p:p + 1, :] + bias)
    l_own = jnp.concatenate(own, axis=0)
    l_past = [jnp.concatenate(rows, axis=0) for rows in past]
    m = jnp.max(l_own, axis=1, keepdims=True)
    for lg in l_past:
        m = jnp.maximum(m, jnp.max(lg, axis=1, keepdims=True))
    w_own = jnp.exp(l_own - m)
    w_past = [jnp.exp(lg - m) for lg in l_past]
    denom = jnp.sum(w_own, axis=1, keepdims=True)
    for w in w_past:
        denom = denom + jnp.sum(w, axis=1, keepdims=True)
    inv = 1.0 / denom
    w_own = w_own * inv
    w_past = [w * inv for w in w_past]

    for cp in slab_copies(b, slot):
        cp.wait()

    o_ref[0] = jnp.zeros(o_ref.shape[1:], F32)
    for p in range(nrows):
        t, h = divmod(p, nheads)
        hs = slice(h * HEAD_DIM, (h + 1) * HEAD_DIM)
        acc = w_own[p:p + 1, :] * vnt_ref[0, hs, :]
        for j in range(MOBA_TOPK):
            for e in range(ppb):
                acc = acc + w_past[j][p:p + 1, e * page:(e + 1) * page] * vbuf[slot, p, j * ppb + e]
        o_ref[0, hs, t:t + 1] = jnp.sum(acc, axis=1, keepdims=True)


def _sample_attend(scores, idx, qt, knt, vnt, cache_vt, page_table, layer, blast, cfar, bown):
    nseq, nblk, nrows, _ = scores.shape
    wa, tq = qt.shape[1], qt.shape[2]
    nheads = wa // HEAD_DIM
    page = cache_vt.shape[-1]
    assert knt.shape[2] == page and bown.shape[1] == page
    seq = lambda shape: pl.BlockSpec((1,) + shape, lambda b, pt, ix: (b,) + (0,) * len(shape))
    whole = lambda shape: pl.BlockSpec(shape, lambda b, pt, ix: (0,) * len(shape))
    grid_spec = pltpu.PrefetchScalarGridSpec(
        num_scalar_prefetch=2,
        grid=(nseq,),
        in_specs=[seq((nblk, nrows, MOBA_BLOCK)), seq((wa, tq)), seq((wa, page)), seq((wa, page)),
                  whole(blast.shape), whole(cfar.shape), whole(bown.shape),
                  pl.BlockSpec(memory_space=pl.ANY)],
        out_specs=seq((wa, 128)),
        scratch_shapes=[pltpu.VMEM((2, nrows, MOBA_TOPK * (MOBA_BLOCK // page), HEAD_DIM, page), F32),
                        pltpu.SemaphoreType.DMA((2, nrows))],
    )
    return pl.pallas_call(
        functools.partial(_sample_attend_kernel, nheads=nheads, nblk=nblk, page=page, tq=tq, layer=layer),
        grid_spec=grid_spec,
        out_shape=jax.ShapeDtypeStruct((nseq, wa, 128), F32),
        compiler_params=_cparams("arbitrary"),
        name="sample_attend",
    )(page_table, idx, scores, qt, knt, vnt, blast, cfar, bown, cache_vt)


def _gdn_kernel(q_ref, k_ref, v_ref, kt_ref, gbc_ref, gbr_ref, s0_ref, onw_ref, o_ref, sout_ref,
                s_s, lhs_s, add_s, *, nheads, nchunk, chunks_per_iter):
    tc = pl.program_id(1)
    c_len = GDN_CHUNK
    hd = HEAD_DIM

    @pl.when(tc == 0)
    def _():
        s_s[...] = s0_ref[0]

    row = lax.broadcasted_iota(jnp.int32, (c_len, c_len), 0)
    col = lax.broadcasted_iota(jnp.int32, (c_len, c_len), 1)
    tri = row >= col
    tri_s = row > col
    eye = row == col
    eye_f = jnp.where(eye, 1.0, 0.0)
    blk = lambda size: (row // size) == (col // size)

    def chunk_a(it, carry):
        chains = [(it * chunks_per_iter + cc, h) for cc in range(chunks_per_iter) for h in range(nheads)]
        st = []
        for c, h in chains:
            gbc = gbc_ref[0, pl.ds(c * c_len, c_len), :]
            gbr = gbr_ref[0, c]
            q = q_ref[0, h, pl.ds(c * c_len, c_len), :]
            k = k_ref[0, h, pl.ds(c * c_len, c_len), :]
            v = v_ref[0, h, pl.ds(c * c_len, c_len), :]
            bcol = gbc[:, h:h + 1]
            gcol = gbc[:, nheads + h:nheads + h + 1]
            grow = gbr[nheads + h:nheads + h + 1, :]
            glast = gcol[c_len - 1:c_len, :]
            st.append(dict(c=c, h=h, q=q, k=k, v=v, bcol=bcol, grow=grow, glast=glast,
                           decay=jnp.where(tri, jnp.exp(gcol - grow), 0.0), egc=jnp.exp(gcol)))
        for d in st:
            d["kq"] = _mm1(jnp.concatenate([d["k"], d["q"]], axis=0), d["k"], _NT)
        for d in st:
            kk, qk = d["kq"][:c_len], d["kq"][c_len:]
            d["n"] = jnp.where(tri_s, d["bcol"] * kk * d["decay"], 0.0)
            d["qkd"] = jnp.where(tri, qk * d["decay"], 0.0)
            d["x"] = jnp.concatenate([d["v"] * d["bcol"], d["k"] * (d["bcol"] * d["egc"])], axis=1)
            d["a"] = -jnp.where(blk(INV_BASE), d["n"], 0.0)
        for d in st:
            d["p2"] = _mm2(d["a"], d["a"])
        for d in st:
            d["res"] = _mm2(jnp.concatenate([d["p2"], eye_f + d["a"]], axis=0), d["p2"])
        for d in st:
            d["p4"] = d["res"][:c_len]
            d["t"] = eye_f + d["a"] + d["res"][c_len:]
        for d in st:
            d["res"] = _mm2(d["t"], d["p4"])
        for d in st:
            d["t"] = d["t"] + d["res"]
        size = 2 * INV_BASE
        while size <= c_len:
            off = blk(size) & jnp.logical_not(blk(size // 2))
            for d in st:
                d["y"] = _mm1(jnp.where(off, d["n"], 0.0), d["t"])
            for d in st:
                d["res"] = _mm1(d["t"], d["y"])
            for d in st:
                d["t"] = d["t"] - d["res"]
            size *= 2
        for d in st:
            d["res"] = _mm2(d["t"], d["x"])
        for d in st:
            d["x"] = d["res"]
        for d in st:
            kdt = kt_ref[0, d["h"], d["c"]] * jnp.exp(d["glast"] - d["grow"])
            d["res"] = _mm2(jnp.concatenate([d["qkd"], kdt], axis=0), d["x"])
        for d in st:
            top = jnp.concatenate([d["q"] * d["egc"], jnp.where(eye, jnp.exp(d["glast"]), 0.0)], axis=0)
            lhs_s[d["c"], d["h"]] = top - d["res"][:, hd:]
            add_s[d["c"], d["h"]] = d["res"][:, :hd]
        return carry

    lax.fori_loop(0, nchunk // chunks_per_iter, chunk_a, 0)

    onw = onw_ref[...]

    state = [s_s[h] for h in range(nheads)]
    for c in range(nchunk):
        res = [_mm3(lhs_s[c, h], state[h]) + add_s[c, h] for h in range(nheads)]
        for h in range(nheads):
            o = res[h][:c_len]
            state[h] = res[h][c_len:]
            o = o * lax.rsqrt(jnp.mean(o * o, axis=-1, keepdims=True) + RMS_EPS) * onw
            o_ref[0, h, c * c_len:(c + 1) * c_len, :] = o
    for h in range(nheads):
        s_s[h] = state[h]
    sout_ref[0] = s_s[...]


def _gdn(qh, kh, vh, kt, gbc, gbr, s0, onw, nchunk):
    b, nheads, t, hd = qh.shape
    tb = nchunk * GDN_CHUNK
    nt = t // tb
    head = pl.BlockSpec((1, nheads, tb, hd), lambda bi, ti: (bi, 0, ti, 0))
    state = pl.BlockSpec((1, nheads, hd, hd), lambda bi, ti: (bi, 0, 0, 0))
    return pl.pallas_call(
        functools.partial(_gdn_kernel, nheads=nheads, nchunk=nchunk, chunks_per_iter=min(4, nchunk)),
        grid=(b, nt),
        in_specs=[head, head, head,
                  pl.BlockSpec((1, nheads, nchunk, hd, GDN_CHUNK), lambda bi, ti: (bi, 0, ti, 0, 0)),
                  pl.BlockSpec((1, tb, 8), lambda bi, ti: (bi, ti, 0)),
                  pl.BlockSpec((1, nchunk, 8, GDN_CHUNK), lambda bi, ti: (bi, ti, 0, 0)),
                  state, _const_spec((1, hd))],
        out_specs=[head, state],
        out_shape=[jax.ShapeDtypeStruct((b, nheads, t, hd), F32),
                   jax.ShapeDtypeStruct((b, nheads, hd, hd), F32)],
        scratch_shapes=[
            pltpu.VMEM((nheads, hd, hd), F32),
            pltpu.VMEM((nchunk, nheads, GDN_CHUNK + hd, hd), F32),
            pltpu.VMEM((nchunk, nheads, GDN_CHUNK + hd, hd), F32),
        ],
        compiler_params=_cparams("parallel", "arbitrary"),
        name="deltanet",
    )(qh, kh, vh, kt, gbc, gbr, s0, onw)


def _out_kernel(x_ref, gate_ref, att_ref, za_ref, yb_ref, yc_ref, zc_ref, w_ref, o_ref):
    if len(yc_ref.shape) == 3:
        yc = jnp.concatenate([yc_ref[h] for h in range(yc_ref.shape[0])], axis=-1)
    else:
        yc = yc_ref[...]
    f32 = lambda ref: ref[...].astype(F32)
    cat = jnp.concatenate([(f32(za_ref) * f32(att_ref)).astype(BF16), yb_ref[...].astype(BF16),
                           (yc * f32(zc_ref)).astype(BF16)], axis=-1)
    y = jnp.dot(cat, w_ref[...], preferred_element_type=F32)
    o_ref[...] = x_ref[...] + gate_ref[...] * y


def _out_proj(x2, gate2, att2, za2, yb2, yc, zc2, wout, rows_per_gate):
    rows, d = x2.shape
    tr = min(OUT_TILE, rows)
    wa, wb = att2.shape[1], yb2.shape[1]
    row = lambda w: pl.BlockSpec((tr, w), lambda i: (i, 0))
    if rows_per_gate == 1:
        gate_spec = row(d)
    else:
        per = rows_per_gate // tr
        gate2 = gate2.reshape(gate2.shape[0], 1, d)
        gate_spec = pl.BlockSpec((None, 1, d), lambda i: (i // per, 0, 0))
    if yc.ndim == 4:
        per_seq = yc.shape[2] // tr
        yc_spec = pl.BlockSpec((None, yc.shape[1], tr, yc.shape[3]), lambda i: (i // per_seq, 0, i % per_seq, 0))
    else:
        yc_spec = row(wb)
    return pl.pallas_call(
        _out_kernel,
        grid=(rows // tr,),
        in_specs=[row(d), gate_spec, row(wa), row(wa), row(wb), yc_spec, row(wb), _const_spec((d, d))],
        out_specs=row(d),
        out_shape=jax.ShapeDtypeStruct((rows, d), F32),
        compiler_params=_cparams("parallel"),
        name="out_proj",
    )(x2, gate2, att2, za2, yb2, yc, zc2, wout)


def _layer_weights(l, norm_w, w_in, q_norm_w, k_norm_w, conv_b_w, conv_c_w, a_log, dt_bias, o_norm_w,
                   w_out, nseq, ts):
    d = w_in.shape[1]
    wa, wb = d // 2, d // 4
    nheads_a = wa // HEAD_DIM
    hc = wb // HEAD_DIM
    n_main = 4 * wa + 4 * wb + 4 * wb
    w = w_in[l]
    wcat = jnp.concatenate([w[:, :n_main], jnp.pad(w[:, n_main:], ((0, 0), (0, 128 - 2 * hc)))], axis=1)
    r = np.arange(ROW_TILE)
    cum_prompt = ((r[:, None] // GDN_CHUNK == r[None, :] // GDN_CHUNK) & (r[:, None] >= r[None, :]))
    rs = np.arange(nseq * ts)
    cum_sample = ((rs[:, None] % nseq == rs[None, :] % nseq) & (rs[:, None] >= rs[None, :]))
    return dict(
        norm_w=norm_w[l].reshape(1, d),
        wcat=wcat.astype(BF16),
        qw=jnp.tile(q_norm_w[l], nheads_a).reshape(1, wa),
        kw=jnp.tile(k_norm_w[l], nheads_a).reshape(1, wa),
        conv_b_w=conv_b_w[l], conv_c_w=conv_c_w[l],
        alog8=jnp.concatenate([jnp.zeros((hc,), F32), a_log[l]]).reshape(1, 2 * hc),
        dtb8=jnp.concatenate([jnp.zeros((hc,), F32), dt_bias[l]]).reshape(1, 2 * hc),
        cum_prompt=jnp.asarray(cum_prompt, F32),
        cum_sample=jnp.asarray(cum_sample, F32),
        onw=o_norm_w[l].reshape(1, HEAD_DIM),
        wout=w_out[l].astype(BF16),
    )


def _heads(a, nheads):
    b, t, _ = a.shape
    return a.reshape(b, t, nheads, HEAD_DIM).transpose(0, 2, 1, 3)


def _gdn_layout(qc, kc, vc, gb, hc):
    b, t, _ = qc.shape
    nch = t // GDN_CHUNK
    kt = kc.reshape(b, nch, GDN_CHUNK, hc, HEAD_DIM).transpose(0, 3, 1, 4, 2)
    gbr = gb.reshape(b, nch, GDN_CHUNK, 2 * hc).transpose(0, 1, 3, 2)
    return _heads(qc, hc), _heads(kc, hc), _heads(vc, hc), kt, gb, gbr


def kernel(x_prompt, x_sample, c_prompt, c_sample, cache_k, cache_v, page_table, state_conv_b,
           state_conv_c, state_delta, norm_w, ada_w, ada_b, w_in, q_norm_w, k_norm_w, rel_bias,
           conv_b_w, conv_c_w, a_log, dt_bias, o_norm_w, w_out):
    bp, tp, d = x_prompt.shape
    bs, ts, _ = x_sample.shape
    depth = ada_w.shape[0]
    wa, wb = d // 2, d // 4
    ha, hc = wa // HEAD_DIM, wb // HEAD_DIM
    page = cache_k.shape[2]
    past_len = page_table.shape[1] * page
    nblk_p = tp // MOBA_BLOCK
    nblk_s = past_len // MOBA_BLOCK
    assert tp % ROW_TILE == 0 and ROW_TILE == MOBA_BLOCK and past_len % MOBA_BLOCK == 0
    assert MOBA_BLOCK % page == 0 and page_table.shape[1] % SAMPLE_PAGES_PER_STEP == 0 and ts <= GDN_CHUNK
    assert SAMPLE_PAGES_PER_STEP % (MOBA_BLOCK // page) == 0 and page == 128

    mod = _modulation(jnp.concatenate([c_prompt, c_sample], axis=0), ada_w, ada_b)

    tab_h = rel_bias.T.astype(F32)
    kq = np.arange(MOBA_BLOCK)
    d_own = kq[None, :] - kq[:, None]
    dist_p = np.stack([np.broadcast_to(d_own + o, (ha, MOBA_BLOCK, MOBA_BLOCK)) for o in (0, MOBA_BLOCK)])
    bt = _bias_tiles(jnp.asarray(dist_p.reshape(-1, MOBA_BLOCK), jnp.int32),
                     jnp.repeat(jnp.tile(tab_h, (2, 1)), MOBA_BLOCK, axis=0))
    bt = bt.reshape(2, ha, MOBA_BLOCK, MOBA_BLOCK)
    cfar_p = bt[1, :, 0, MOBA_BLOCK - 1]

    tq = np.repeat(np.arange(ts), ha)
    pos_q = past_len + tq
    d_last = pos_q[:, None] - ((nblk_s - 1) * MOBA_BLOCK + kq[None, :])
    own_w = 128
    tk = np.arange(own_w)
    d_own_s = np.where(tk[None, :] < ts, tq[:, None] - tk[None, :], -1)
    d_far = np.broadcast_to(pos_q[:, None] - (nblk_s - 2) * MOBA_BLOCK - (MOBA_BLOCK - 1), (ts * ha, 128))
    tab_s = jnp.tile(tab_h, (ts, 1))
    blast = _bias_tiles(jnp.asarray(d_last, jnp.int32), tab_s)
    bown = _bias_tiles(jnp.asarray(d_own_s, jnp.int32), tab_s)
    cfar_s = _bias_tiles(jnp.asarray(d_far, jnp.int32), tab_s)[:, :1]

    cache_kt = cache_k.transpose(0, 1, 3, 4, 2)
    cache_vt = cache_v.transpose(0, 1, 3, 4, 2)

    hp = x_prompt
    hs_tm = x_sample.transpose(1, 0, 2).reshape(ts * bs, d)
    outs = [[] for _ in range(10)]
    for l in range(depth):
        lw = _layer_weights(l, norm_w, w_in, q_norm_w, k_norm_w, conv_b_w, conv_c_w, a_log, dt_bias,
                            o_norm_w, w_out, bs, ts)
        mod_p, mod_s = mod[l, :bp], mod[l, bp:]

        (q, k, v, kb, vt, kmean, za, yb, cbs, qh, kh, vh, kt, zc, gb, gbr, ccs) = _proj_prompt(hp, mod_p, lw)
        att = _attn_prompt(q, kb, vt, kmean.reshape(bp, nblk_p, wa), bt, cfar_p)
        nchunk_p = 8
        yc, s_new = _gdn(qh, kh, vh, kt, gb, gbr, jnp.zeros((bp, hc, HEAD_DIM, HEAD_DIM), F32),
                         lw["onw"], nchunk_p)
        hp = _out_proj(hp.reshape(bp * tp, d), mod_p[:, 2 * d:], att.reshape(bp * tp, wa),
                       za.reshape(bp * tp, wa), yb.reshape(bp * tp, wb), yc, zc.reshape(bp * tp, wb),
                       lw["wout"], tp).reshape(bp, tp, d)
        outs[0].append(k.reshape(bp, tp, ha, HEAD_DIM)); outs[1].append(v.reshape(bp, tp, ha, HEAD_DIM))
        outs[4].append(cbs); outs[6].append(ccs); outs[8].append(s_new)

        rep = lambda a: jnp.tile(a, (ts, 1))
        hist_b = state_conv_b[l].transpose(1, 0, 2).reshape(-1, wb)
        hist_c = state_conv_c[l].transpose(1, 0, 2).reshape(-1, 3 * wb)
        (q, k, v, za, yb, cbs, qc, kc, vc, zc, gb, ccs) = _proj_sample(
            hs_tm, rep(mod_s[:, :d]), rep(mod_s[:, d:2 * d]), lw, hist_b, hist_c, bs)
        seq_major = lambda a: a.reshape(ts, bs, -1).transpose(1, 0, 2)
        qt = seq_major(q).transpose(0, 2, 1)
        new_t = lambda a: jnp.pad(seq_major(a), ((0, 0), (0, own_w - ts), (0, 0))).transpose(0, 2, 1)
        scores, idx = _sample_scores(jnp.repeat(seq_major(q), ha, axis=1), cache_kt, page_table, l)
        att_t = _sample_attend(scores, idx[:, :, :MOBA_TOPK].reshape(bs, -1), qt, new_t(k), new_t(v),
                               cache_vt, page_table, l, blast, cfar_s, bown)
        att = att_t[:, :, :ts].transpose(2, 0, 1).reshape(ts * bs, wa)
        pad_t = lambda a: jnp.pad(seq_major(a), ((0, 0), (0, GDN_CHUNK - ts), (0, 0)))
        gbs = seq_major(gb)
        gb_pad = jnp.concatenate(
            [pad_t(gb[:, :hc]), jnp.pad(gbs[:, :, hc:], ((0, 0), (0, GDN_CHUNK - ts), (0, 0)), mode="edge")],
            axis=-1)
        yc, s_new = _gdn(*_gdn_layout(pad_t(qc), pad_t(kc), pad_t(vc), gb_pad, hc), state_delta[l],
                         lw["onw"], 1)
        yc = yc[:, :, :ts, :].transpose(2, 0, 1, 3).reshape(ts * bs, wb)
        hs_tm = _out_proj(hs_tm, rep(mod_s[:, 2 * d:]), att, za, yb, yc, zc, lw["wout"], 1)
        outs[2].append(seq_major(k).reshape(bs, ts, ha, HEAD_DIM))
        outs[3].append(seq_major(v).reshape(bs, ts, ha, HEAD_DIM))
        outs[5].append(cbs.reshape(-1, bs, wb).transpose(1, 0, 2))
        outs[7].append(ccs.reshape(-1, bs, 3 * wb).transpose(1, 0, 2))
        outs[9].append(s_new)

    y_sample = hs_tm.reshape(ts, bs, d).transpose(1, 0, 2)
    return (hp, y_sample) + tuple(jnp.stack(o) for o in outs)
```
